```python
import jax, jax.numpy as jnp
from jax import lax
import numpy as np

D_MODEL = 1024
BATCH = 8
SEQ = 8192
DEPTH = 2

GRID_W = 64
CTX_LEN = 256
N_EVEN = (DEPTH + 1) // 2
N_ODD = DEPTH // 2
EPS = 1e-6

POOL_WINDOWS = (2, 4, 8, 16)
POOL_GROUPS = len(POOL_WINDOWS)
POOL_GROUP_DIM = D_MODEL // 16
POOL_DIM = POOL_GROUPS * POOL_GROUP_DIM

HEAD_DIM = 128
N_Q_HEADS = 6
N_KV_HEADS = 2
Q_PER_KV = N_Q_HEADS // N_KV_HEADS
ATTN_DIM = N_Q_HEADS * HEAD_DIM
KV_DIM = N_KV_HEADS * HEAD_DIM
ROPE_THETA = 10000.0
Q_BLOCK = 128
MIX_WIDTH = POOL_DIM + ATTN_DIM
IN_A_WIDTH = POOL_DIM + ATTN_DIM + 2 * KV_DIM

D_INNER = 2 * D_MODEL
SSM_HEAD_DIM = 64
SSM_HEADS = D_INNER // SSM_HEAD_DIM
SSM_GROUPS = 4
HEADS_PER_GROUP = SSM_HEADS // SSM_GROUPS
D_STATE = 128
D_CONV = 4
SSD_CHUNK = 128
CONV_DIM = D_INNER + 2 * SSM_GROUPS * D_STATE
IN_C_WIDTH = D_INNER + CONV_DIM + 2 * SSM_HEADS

D_FF = 2816
FFN_CONV = 3

kernel_name = "hybrid_pool_attn_ssd_convffn_dit"


def rms_norm(x, w):
    xf = x.astype(jnp.float32)
    y = xf * lax.rsqrt(jnp.mean(xf * xf, axis=-1, keepdims=True) + EPS)
    return (y * w.astype(jnp.float32)).astype(x.dtype)


def adaln(cond, w, b):
    return jax.nn.silu(cond) @ w + b


def modulate(h, shift, scale):
    return h * (1.0 + scale) + shift


def depthwise_conv_centred(u, w, b):
    T = u.shape[1]
    K = w.shape[1]
    left = K // 2
    right = K - 1 - left
    up = jnp.pad(u, ((0, 0), (left, right), (0, 0)))
    out = b
    for k in range(K):
        out = out + up[:, k:k + T] * w[:, k]
    return out


def axial_rope_tables(T):
    rows = T // GRID_W
    row = jnp.repeat(jnp.arange(rows, dtype=jnp.float32), GRID_W)
    col = jnp.tile(jnp.arange(GRID_W, dtype=jnp.float32), rows)
    half = HEAD_DIM // 2
    inv_freq = ROPE_THETA ** (-jnp.arange(0, half, 2, dtype=jnp.float32) / half)
    ang = jnp.concatenate([row[:, None] * inv_freq, col[:, None] * inv_freq], axis=-1)
    return jnp.cos(ang), jnp.sin(ang)


def apply_rope(x, cos, sin):
    cos = cos[None, :, None, :].astype(x.dtype)
    sin = sin[None, :, None, :].astype(x.dtype)
    x1 = x[..., 0::2]
    x2 = x[..., 1::2]
    r1 = x1 * cos - x2 * sin
    r2 = x1 * sin + x2 * cos
    return jnp.stack([r1, r2], axis=-1).reshape(x.shape)


def centred_window_mean(u, w):
    T = u.shape[1]
    left = w // 2
    right = w - 1 - left
    up = jnp.pad(u.astype(jnp.float32), ((0, 0), (left + 1, right), (0, 0)))
    cs = jnp.cumsum(up, axis=1)
    total = cs[:, w:] - cs[:, :T]
    t = jnp.arange(T)
    cnt = (jnp.minimum(t + right, T - 1) - jnp.maximum(t - left, 0) + 1).astype(jnp.float32)
    return (total / cnt[None, :, None]).astype(u.dtype)


def pool_mixer(u, pool_w, pool_scale):
    B, T, _ = u.shape
    parts = []
    for g, w in enumerate(POOL_WINDOWS):
        ug = u[..., g * POOL_GROUP_DIM:(g + 1) * POOL_GROUP_DIM]
        parts.append(centred_window_mean(ug, w) - ug)
    p = jnp.stack(parts, axis=2)
    y = jnp.einsum("btgi,gio->btgo", p, pool_w).reshape(B, T, POOL_DIM)
    return y * pool_scale


def attention_latent(q_l, k_all, v_all):
    B, T = q_l.shape[:2]
    nb = T // Q_BLOCK
    scale = HEAD_DIM ** -0.5
    qb = q_l.reshape(B, nb, Q_BLOCK, N_KV_HEADS, Q_PER_KV, HEAD_DIM).transpose(1, 0, 2, 3, 4, 5)

    def block(q_blk):
        s = jnp.einsum("bqhgd,bkhd->bhgqk", q_blk, k_all).astype(jnp.float32) * scale
        p = jax.nn.softmax(s, axis=-1).astype(v_all.dtype)
        return jnp.einsum("bhgqk,bkhd->bqhgd", p, v_all)

    o = lax.map(block, qb)
    return o.transpose(1, 0, 2, 3, 4, 5).reshape(B, T, ATTN_DIM)


def attention_context(q_c, k_c, v_c):
    B, L = q_c.shape[:2]
    q = q_c.reshape(B, L, N_KV_HEADS, Q_PER_KV, HEAD_DIM)
    s = jnp.einsum("bqhgd,bkhd->bhgqk", q, k_c).astype(jnp.float32) * (HEAD_DIM ** -0.5)
    p = jax.nn.softmax(s, axis=-1).astype(v_c.dtype)
    return jnp.einsum("bhgqk,bkhd->bqhgd", p, v_c).reshape(B, L, ATTN_DIM)


def pool_attention_mixer(h_l, h_c, cos, sin, w_in, pool_w, pool_scale, q_gain, k_gain, w_out, need_ctx):
    def project(h):
        B, T, _ = h.shape
        u = h @ w_in
        a = u[..., :POOL_DIM]
        q = u[..., POOL_DIM:POOL_DIM + ATTN_DIM].reshape(B, T, N_Q_HEADS, HEAD_DIM)
        k = u[..., POOL_DIM + ATTN_DIM:POOL_DIM + ATTN_DIM + KV_DIM].reshape(B, T, N_KV_HEADS, HEAD_DIM)
        v = u[..., POOL_DIM + ATTN_DIM + KV_DIM:].reshape(B, T, N_KV_HEADS, HEAD_DIM)
        return a, rms_norm(q, q_gain), rms_norm(k, k_gain), v

    a_l, q_l, k_l, v_l = project(h_l)
    q_l = apply_rope(q_l, cos, sin)
    k_l = apply_rope(k_l, cos, sin)
    a_c, q_c, k_c, v_c = project(h_c)
    k_all = jnp.concatenate([k_c, k_l], axis=1)
    v_all = jnp.concatenate([v_c, v_l], axis=1)
    o_l = attention_latent(q_l, k_all, v_all)
    y_l = jnp.concatenate([pool_mixer(a_l, pool_w, pool_scale), o_l], axis=-1) @ w_out
    y_c = None
    if need_ctx:
        o_c = attention_context(q_c, k_c, v_c)
        y_c = jnp.concatenate([pool_mixer(a_c, pool_w, pool_scale), o_c], axis=-1) @ w_out
    return y_l, y_c


def ssd_scan(xs, dt, A, Bm, Cm, h0):
    Bsz, T = xs.shape[:2]
    nc = T // SSD_CHUNK
    Q = SSD_CHUNK
    xdt = (xs.astype(jnp.float32) * dt[..., None]).reshape(Bsz, nc, Q, SSM_GROUPS, HEADS_PER_GROUP, SSM_HEAD_DIM)
    a = (dt * A).reshape(Bsz, nc, Q, SSM_GROUPS, HEADS_PER_GROUP)
    a_cum = jnp.cumsum(a, axis=2)
    Bc = Bm.astype(jnp.float32).reshape(Bsz, nc, Q, SSM_GROUPS, D_STATE)
    Cc = Cm.astype(jnp.float32).reshape(Bsz, nc, Q, SSM_GROUPS, D_STATE)
    lower = jnp.tril(jnp.ones((Q, Q), dtype=bool))[None, :, :, None, None]

    def chunk_step(h, inp):
        xq, aq, Bq, Cq = inp
        seg = aq[:, :, None] - aq[:, None, :]
        Lmat = jnp.exp(jnp.where(lower, seg, -jnp.inf))
        cb = jnp.einsum("blgn,bsgn->blsg", Cq, Bq)
        y = (jnp.einsum("blsg,blsgh,bsghp->blghp", cb, Lmat, xq)
             + jnp.einsum("blgn,bghpn,blgh->blghp", Cq, h, jnp.exp(aq)))
        decay_to_end = jnp.exp(aq[:, -1:] - aq)
        h_new = (jnp.exp(aq[:, -1])[..., None, None] * h
                 + jnp.einsum("bsgn,bsgh,bsghp->bghpn", Bq, decay_to_end, xq))
        return h_new, y

    inputs = (jnp.swapaxes(xdt, 0, 1), jnp.swapaxes(a_cum, 0, 1), jnp.swapaxes(Bc, 0, 1), jnp.swapaxes(Cc, 0, 1))
    h_final, ys = lax.scan(chunk_step, h0, inputs)
    y = jnp.swapaxes(ys, 0, 1).reshape(Bsz, T, SSM_HEADS, SSM_HEAD_DIM)
    return y.astype(xs.dtype), h_final


def bidirectional_ssd_mixer(h_l, h_c, w_in, conv_w, conv_b, A_log, dt_bias, D_skip, norm_w, w_out, need_ctx):
    A = -jnp.exp(A_log.astype(jnp.float32))

    def prep(h):
        B, T, _ = h.shape
        u = h @ w_in
        z = u[..., :D_INNER]
        xbc = jax.nn.silu(depthwise_conv_centred(u[..., D_INNER:D_INNER + CONV_DIM], conv_w, conv_b))
        dt_raw = u[..., D_INNER + CONV_DIM:].reshape(B, T, 2, SSM_HEADS)
        xs = xbc[..., :D_INNER].reshape(B, T, SSM_HEADS, SSM_HEAD_DIM)
        Bm = xbc[..., D_INNER:D_INNER + SSM_GROUPS * D_STATE].reshape(B, T, SSM_GROUPS, D_STATE)
        Cm = xbc[..., D_INNER + SSM_GROUPS * D_STATE:].reshape(B, T, SSM_GROUPS, D_STATE)
        dt = jax.nn.softplus(dt_raw.astype(jnp.float32) + dt_bias.astype(jnp.float32))
        return z, xs, Bm, Cm, dt

    def run(z, xs, Bm, Cm, dt, h0_f, h0_b):
        fl = lambda t: jnp.flip(t, axis=1)
        y_f, hf = ssd_scan(xs, dt[:, :, 0], A[0], Bm, Cm, h0_f)
        y_b, hb = ssd_scan(fl(xs), fl(dt[:, :, 1]), A[1], fl(Bm), fl(Cm), h0_b)
        y = y_f + fl(y_b) + D_skip[:, None] * xs
        B, T = xs.shape[:2]
        y = rms_norm(y.reshape(B, T, D_INNER) * jax.nn.silu(z), norm_w)
        return y @ w_out, hf, hb

    zc, xc, Bc, Cc, dtc = prep(h_c)
    zeros = jnp.zeros((h_c.shape[0], SSM_GROUPS, HEADS_PER_GROUP, SSM_HEAD_DIM, D_STATE), jnp.float32)
    y_c = None
    if need_ctx:
        y_c, hc_f, hc_b = run(zc, xc, Bc, Cc, dtc, zeros, zeros)
    else:
        fl = lambda t: jnp.flip(t, axis=1)
        _, hc_f = ssd_scan(xc, dtc[:, :, 0], A[0], Bc, Cc, zeros)
        _, hc_b = ssd_scan(fl(xc), fl(dtc[:, :, 1]), A[1], fl(Bc), fl(Cc), zeros)
    zl, xl, Bl, Cl, dtl = prep(h_l)
    y_l, _, _ = run(zl, xl, Bl, Cl, dtl, hc_f, hc_b)
    return y_l, y_c


def conv_ffn(h, w_up, conv_w, conv_b, w_down):
    u = depthwise_conv_centred(h @ w_up, conv_w, conv_b)
    val = u[..., :D_FF]
    gate = u[..., D_FF:]
    return (jax.nn.silu(gate) * val) @ w_down


def setup_inputs(seed: int = 0) -> dict:
    key = jax.random.key(seed)
    ks = jax.random.split(key, 28)
    f32 = jnp.float32
    nrm = lambda k, s: jax.random.normal(k, s, f32)
    dense = lambda k, s, fan_in: nrm(k, s) * fan_in ** -0.5
    gain = lambda k, s: 1.0 + 0.05 * nrm(k, s)
    dt0 = jnp.exp(jax.random.uniform(ks[17], (N_ODD, 2, SSM_HEADS), f32, np.log(1e-3), np.log(1e-1)))
    return {
        "x": nrm(ks[0], (BATCH, SEQ, D_MODEL)),
        "c": nrm(ks[1], (BATCH, D_MODEL)),
        "ctx": nrm(ks[2], (BATCH, CTX_LEN, D_MODEL)),
        "c_ctx": nrm(ks[3], (D_MODEL,)),
        "ada_w": 0.5 * dense(ks[4], (DEPTH, D_MODEL, 6 * D_MODEL), D_MODEL),
        "ada_b": 0.02 * nrm(ks[5], (DEPTH, 6 * D_MODEL)),
        "norm_w": gain(ks[6], (DEPTH, 4, D_MODEL)),
        "attn_w_in": dense(ks[7], (N_EVEN, D_MODEL, IN_A_WIDTH), D_MODEL),
        "pool_w": dense(ks[8], (N_EVEN, POOL_GROUPS, POOL_GROUP_DIM, POOL_GROUP_DIM), POOL_GROUP_DIM),
        "pool_scale": gain(ks[9], (N_EVEN, POOL_DIM)),
        "q_gain": gain(ks[10], (N_EVEN, HEAD_DIM)),
        "k_gain": gain(ks[11], (N_EVEN, HEAD_DIM)),
        "attn_w_out": dense(ks[12], (N_EVEN, MIX_WIDTH, D_MODEL), MIX_WIDTH),
        "ssm_w_in": dense(ks[13], (N_ODD, D_MODEL, IN_C_WIDTH), D_MODEL),
        "ssm_conv_w": dense(ks[14], (N_ODD, CONV_DIM, D_CONV), D_CONV),
        "ssm_conv_b": 0.02 * nrm(ks[15], (N_ODD, CONV_DIM)),
        "ssm_A_log": jnp.log(jax.random.uniform(ks[16], (N_ODD, 2, SSM_HEADS), f32, 1.0, 16.0)),
        "ssm_dt_bias": dt0 + jnp.log(-jnp.expm1(-dt0)),
        "ssm_D": gain(ks[18], (N_ODD, SSM_HEADS)),
        "ssm_norm_w": gain(ks[19], (N_ODD, D_INNER)),
        "ssm_w_out": dense(ks[20], (N_ODD, D_INNER, D_MODEL), D_INNER),
        "ffn_w_up": dense(ks[21], (DEPTH, D_MODEL, 2 * D_FF), D_MODEL),
        "ffn_conv_w": dense(ks[22], (DEPTH, 2 * D_FF, FFN_CONV), FFN_CONV),
        "ffn_conv_b": 0.02 * nrm(ks[23], (DEPTH, 2 * D_FF)),
        "ffn_w_down": dense(ks[24], (DEPTH, D_FF, D_MODEL), D_FF),
    }


def reference(x, c, ctx, c_ctx, ada_w, ada_b, norm_w, attn_w_in, pool_w, pool_scale, q_gain, k_gain,
              attn_w_out, ssm_w_in, ssm_conv_w, ssm_conv_b, ssm_A_log, ssm_dt_bias, ssm_D, ssm_norm_w,
              ssm_w_out, ffn_w_up, ffn_conv_w, ffn_conv_b, ffn_w_down):
    T = x.shape[1]
    cos, sin = axial_rope_tables(T)
    for i in range(DEPTH):
        last = i == DEPTH - 1
        j = i // 2
        mod_l = jnp.split(adaln(c, ada_w[i], ada_b[i])[:, None, :], 6, axis=-1)
        mod_c = jnp.split(adaln(c_ctx, ada_w[i], ada_b[i])[None, None, :], 6, axis=-1)
        h_l = modulate(rms_norm(x, norm_w[i, 0]), mod_l[0], mod_l[1])
        h_c = modulate(rms_norm(ctx, norm_w[i, 0]), mod_c[0], mod_c[1])
        if i % 2 == 0:
            y_l, y_c = pool_attention_mixer(h_l, h_c, cos, sin, attn_w_in[j], pool_w[j], pool_scale[j],
                                            q_gain[j], k_gain[j], attn_w_out[j], not last)
        else:
            y_l, y_c = bidirectional_ssd_mixer(h_l, h_c, ssm_w_in[j], ssm_conv_w[j], ssm_conv_b[j], ssm_A_log[j],
                                               ssm_dt_bias[j], ssm_D[j], ssm_norm_w[j], ssm_w_out[j], not last)
        x = x + mod_l[2] * rms_norm(y_l, norm_w[i, 1])
        f_l = conv_ffn(modulate(rms_norm(x, norm_w[i, 2]), mod_l[3], mod_l[4]),
                       ffn_w_up[i], ffn_conv_w[i], ffn_conv_b[i], ffn_w_down[i])
        x = x + mod_l[5] * rms_norm(f_l, norm_w[i, 3])
        if not last:
            ctx = ctx + mod_c[2] * rms_norm(y_c, norm_w[i, 1])
            f_c = conv_ffn(modulate(rms_norm(ctx, norm_w[i, 2]), mod_c[3], mod_c[4]),
                           ffn_w_up[i], ffn_conv_w[i], ffn_conv_b[i], ffn_w_down[i])
            ctx = ctx + mod_c[5] * rms_norm(f_c, norm_w[i, 3])
    return x
```

```python
import functools
import math

import jax
import jax.numpy as jnp
from jax import lax
from jax.experimental import pallas as pl
from jax.experimental.pallas import tpu as pltpu

F32 = jnp.float32
BF16 = jnp.bfloat16

EPS = 1e-6
GRID_W = 64
ROPE_THETA = 10000.0

HEAD_DIM = 128
N_Q_HEADS = 6
N_KV_HEADS = 2
Q_PER_KV = N_Q_HEADS // N_KV_HEADS
POOL_WINDOWS = (2, 4, 8, 16)
POOL_GROUP_DIM = 64
POOL_DIM = 256
ATTN_DIM = N_Q_HEADS * HEAD_DIM
KV_DIM = N_KV_HEADS * HEAD_DIM

SSM_HEAD_DIM = 64
SSM_HEADS = 32
SSM_GROUPS = 4
HEADS_PER_GROUP = 8
D_STATE = 128
D_INNER = SSM_HEADS * SSM_HEAD_DIM
GROUP_W = HEADS_PER_GROUP * SSM_HEAD_DIM
BC_DIM = SSM_GROUPS * D_STATE
SSD_CHUNK = 128

HALO = 8
LANES = 128
VMEM_LIMIT = 56 * 1024 * 1024


def _params(*sem):
    return pltpu.CompilerParams(dimension_semantics=sem, vmem_limit_bytes=VMEM_LIMIT)


def _const_spec(shape):
    nd = len(shape)
    return pl.BlockSpec(shape, lambda *_: (0,) * nd, pipeline_mode=pl.Buffered(1))


def _row_spec(c, shared):
    if shared:
        return pl.BlockSpec((1, 1, c), lambda b, i: (0, 0, 0))
    return pl.BlockSpec((1, 1, c), lambda b, i: (b, 0, 0))


def _halo_specs(tm, c, t):
    nb = tm // HALO
    last = t // HALO - 1
    main = pl.BlockSpec((1, tm, c), lambda b, i: (b, i, 0))
    prev = pl.BlockSpec((1, HALO, c), lambda b, i: (b, jnp.maximum(i * nb - 1, 0), 0))
    nxt = pl.BlockSpec((1, HALO, c), lambda b, i: (b, jnp.minimum((i + 1) * nb, last), 0))
    return main, prev, nxt


def _rms(xf, w):
    ms = jnp.mean(xf * xf, axis=-1, keepdims=True)
    return xf * lax.rsqrt(ms + EPS) * w


def _silu(x):
    return x * jax.nn.sigmoid(x)


def _dot(a, b):
    return jnp.dot(a, b, preferred_element_type=F32)


def _ada_kernel(c_ref, w_ref, b_ref, o_ref):
    s = _silu(c_ref[...])
    o_ref[0] = _dot(s.astype(BF16), w_ref[0].astype(BF16)) + b_ref[0]


def _adaln(cond, ada_w, ada_b):
    depth, d, n = ada_w.shape
    rows = cond.shape[0]
    tn = 1536
    return pl.pallas_call(
        _ada_kernel,
        grid=(depth, n // tn),
        in_specs=[
            pl.BlockSpec((rows, d), lambda l, j: (0, 0)),
            pl.BlockSpec((1, d, tn), lambda l, j: (l, 0, j)),
            pl.BlockSpec((1, 1, tn), lambda l, j: (l, 0, j)),
        ],
        out_specs=pl.BlockSpec((1, rows, tn), lambda l, j: (l, 0, j)),
        out_shape=jax.ShapeDtypeStruct((depth, rows, n), F32),
        compiler_params=_params("parallel", "parallel"),
        name="adaln",
    )(cond, ada_w, ada_b.reshape(depth, 1, n))


def _attn_in_kernel(*refs, rope):
    if rope:
        (x_ref, sh_ref, sc_ref, nw_ref, w_ref, qg_ref, kg_ref, cos_ref, sin_ref,
         a_ref, q_ref, k_ref, v_ref) = refs
        cos2, sin2 = cos_ref[...], sin_ref[...]
    else:
        (x_ref, sh_ref, sc_ref, nw_ref, w_ref, qg_ref, kg_ref,
         a_ref, q_ref, k_ref, v_ref) = refs
    h = _rms(x_ref[0], nw_ref[...]) * (1.0 + sc_ref[0]) + sh_ref[0]
    u = _dot(h.astype(BF16), w_ref[...])
    a_ref[0] = u[:, :POOL_DIM]

    def norm_rope(t, gain):
        y = _rms(t, gain)
        if rope:
            y = y * cos2 + pltpu.roll(y, HEAD_DIM // 2, 1) * sin2
        return y.astype(BF16)

    off = POOL_DIM
    for hh in range(N_Q_HEADS):
        q_ref[0, :, hh * HEAD_DIM:(hh + 1) * HEAD_DIM] = norm_rope(
            u[:, off + hh * HEAD_DIM: off + (hh + 1) * HEAD_DIM], qg_ref[...])
    off += ATTN_DIM
    for hh in range(N_KV_HEADS):
        k_ref[0, :, hh * HEAD_DIM:(hh + 1) * HEAD_DIM] = norm_rope(
            u[:, off + hh * HEAD_DIM: off + (hh + 1) * HEAD_DIM], kg_ref[...])
    off += KV_DIM
    v_ref[0] = u[:, off:].astype(BF16)


def _attn_in(x, shift, scale, nw, w_in, qg, kg, cos2, sin2, tm):
    b, t, d = x.shape
    shared = shift.shape[0] == 1
    rope = cos2 is not None
    width = w_in.shape[1]
    in_specs = [
        pl.BlockSpec((1, tm, d), lambda bb, i: (bb, i, 0)),
        _row_spec(d, shared), _row_spec(d, shared),
        _const_spec((1, d)), _const_spec((d, width)),
        _const_spec((1, HEAD_DIM)), _const_spec((1, HEAD_DIM)),
    ]
    args = [x, shift, scale, nw, w_in, qg, kg]
    if rope:
        in_specs += [pl.BlockSpec((tm, HEAD_DIM), lambda bb, i: (i, 0))] * 2
        args += [cos2, sin2]
    outs = [(POOL_DIM, F32), (ATTN_DIM, BF16), (KV_DIM, BF16), (KV_DIM, BF16)]
    return pl.pallas_call(
        functools.partial(_attn_in_kernel, rope=rope),
        grid=(b, t // tm),
        in_specs=in_specs,
        out_specs=[pl.BlockSpec((1, tm, c), lambda bb, i: (bb, i, 0)) for c, _ in outs],
        out_shape=[jax.ShapeDtypeStruct((b, t, c), dt) for c, dt in outs],
        compiler_params=_params("parallel", "parallel"),
        name="attn_in",
    )(*args)


def _attn_kernel(*refs, seg_lens, tk, tq):
    nseg = len(seg_lens)
    q_ref = refs[0]
    kv_refs = refs[1:1 + 2 * nseg]
    o_ref = refs[1 + 2 * nseg]
    m_sc, l_sc, acc_sc = refs[2 + 2 * nseg:]
    c = (HEAD_DIM ** -0.5) * math.log2(math.e)

    q = q_ref[0]
    qs = jnp.concatenate([q[:, g * HEAD_DIM:(g + 1) * HEAD_DIM] for g in range(Q_PER_KV)], axis=0)
    m_sc[...] = jnp.full(m_sc.shape, -jnp.inf, F32)
    l_sc[...] = jnp.zeros(l_sc.shape, F32)
    acc_sc[...] = jnp.zeros(acc_sc.shape, F32)

    def step(kc, vc):
        s = lax.dot_general(qs, kc, (((1,), (1,)), ((), ())), preferred_element_type=F32)
        m_prev = m_sc[...]
        m_new = jnp.maximum(m_prev, jnp.max(s, axis=-1, keepdims=True))
        alpha = jnp.exp2((m_prev - m_new) * c)
        p = jnp.exp2((s - m_new) * c)
        l_sc[...] = alpha * l_sc[...] + jnp.sum(p, axis=-1, keepdims=True)
        acc_sc[...] = alpha * acc_sc[...] + _dot(p.astype(BF16), vc)
        m_sc[...] = m_new

    for sidx, slen in enumerate(seg_lens):
        k_ref, v_ref = kv_refs[2 * sidx], kv_refs[2 * sidx + 1]
        chunk = min(tk, slen)
        n = slen // chunk
        if n == 1:
            step(k_ref[0], v_ref[0])
        else:
            def body(j, carry, k_ref=k_ref, v_ref=v_ref, chunk=chunk):
                st = pl.multiple_of(j * chunk, chunk)
                step(k_ref[0, pl.ds(st, chunk), :], v_ref[0, pl.ds(st, chunk), :])
                return carry
            lax.fori_loop(0, n, body, 0)

    o = acc_sc[...] / l_sc[...]
    for g in range(Q_PER_KV):
        o_ref[0, :, g * HEAD_DIM:(g + 1) * HEAD_DIM] = o[g * tq:(g + 1) * tq].astype(BF16)


def _attention(q, kvs, tq, tk):
    b, t, _ = q.shape
    gw = Q_PER_KV * HEAD_DIM
    seg_lens = tuple(k.shape[1] for k, _ in kvs)
    in_specs = [pl.BlockSpec((1, tq, gw), lambda bb, g, i: (bb, i, g))]
    args = [q]
    for (k, v), slen in zip(kvs, seg_lens):
        in_specs += [pl.BlockSpec((1, slen, HEAD_DIM), lambda bb, g, i: (bb, 0, g))] * 2
        args += [k, v]
    m = Q_PER_KV * tq
    return pl.pallas_call(
        functools.partial(_attn_kernel, seg_lens=seg_lens, tk=tk, tq=tq),
        grid=(b, N_KV_HEADS, t // tq),
        in_specs=in_specs,
        out_specs=pl.BlockSpec((1, tq, gw), lambda bb, g, i: (bb, i, g)),
        out_shape=jax.ShapeDtypeStruct((b, t, ATTN_DIM), BF16),
        scratch_shapes=[pltpu.VMEM((m, 1), F32), pltpu.VMEM((m, 1), F32), pltpu.VMEM((m, HEAD_DIM), F32)],
        compiler_params=_params("parallel", "parallel", "arbitrary"),
        name="attention",
    )(*args)


def _attn_out_kernel(x_ref, g_ref, nw_ref, a_ref, ap_ref, an_ref, o_ref, pw_ref, ps_ref, wa_ref, wo_ref,
                     out_ref, abuf, *, tm, t):
    i = pl.program_id(1)
    last = pl.num_programs(1) - 1
    abuf[0:HALO] = jnp.where(i == 0, 0.0, ap_ref[0])
    abuf[HALO:HALO + tm] = a_ref[0]
    abuf[HALO + tm:] = jnp.where(i == last, 0.0, an_ref[0])

    def sh(d):
        return abuf[HALO + d:HALO + d + tm, :]

    x0 = sh(0)
    s2 = sh(-1) + x0
    s4 = s2 + sh(-2) + sh(1)
    s8 = s4 + sh(-4) + sh(-3) + sh(2) + sh(3)
    s16 = s8 + sh(-8) + sh(-7) + sh(-6) + sh(-5) + sh(4) + sh(5) + sh(6) + sh(7)
    pos = i * tm + lax.broadcasted_iota(jnp.int32, (tm, 1), 0)
    lane = lax.broadcasted_iota(jnp.int32, (1, POOL_DIM), 1)

    def mean(s, w):
        left = w // 2
        right = w - 1 - left
        cnt = jnp.minimum(pos + right, t - 1) - jnp.maximum(pos - left, 0) + 1
        return s / cnt.astype(F32)

    pm = jnp.where(lane < 64, mean(s2, 2),
                   jnp.where(lane < 128, mean(s4, 4),
                             jnp.where(lane < 192, mean(s8, 8), mean(s16, 16))))
    p = pm - x0
    py = _dot(p.astype(BF16), pw_ref[...]) * ps_ref[...]
    y = _dot(py.astype(BF16), wa_ref[...]) + _dot(o_ref[0], wo_ref[...])
    out_ref[0] = x_ref[0] + g_ref[0] * _rms(y, nw_ref[...])


def _attn_out(x, gate, nw, a, o, pool_bd, pool_scale, w_a, w_o, tm):
    b, t, d = x.shape
    shared = gate.shape[0] == 1
    am, ap, an = _halo_specs(tm, POOL_DIM, t)
    return pl.pallas_call(
        functools.partial(_attn_out_kernel, tm=tm, t=t),
        grid=(b, t // tm),
        in_specs=[
            pl.BlockSpec((1, tm, d), lambda bb, i: (bb, i, 0)),
            _row_spec(d, shared), _const_spec((1, d)),
            am, ap, an,
            pl.BlockSpec((1, tm, ATTN_DIM), lambda bb, i: (bb, i, 0)),
            _const_spec((POOL_DIM, POOL_DIM)), _const_spec((1, POOL_DIM)),
            _const_spec((POOL_DIM, d)), _const_spec((ATTN_DIM, d)),
        ],
        out_specs=pl.BlockSpec((1, tm, d), lambda bb, i: (bb, i, 0)),
        out_shape=jax.ShapeDtypeStruct((b, t, d), F32),
        scratch_shapes=[pltpu.VMEM((tm + 2 * HALO, POOL_DIM), F32)],
        compiler_params=_params("parallel", "parallel"),
        name="attn_out",
    )(x, gate, nw, a, a, a, o, pool_bd, pool_scale, w_a, w_o)


def _ffn_kernel(x_ref, xp_ref, xn_ref, sh_ref, sc_ref, g_ref, nw2_ref, nw3_ref,
                wup_ref, cw_ref, cb_ref, wdn_ref, out_ref, xe_sc, h_sc, u_sc, acc_sc, *, tm, fc, nch):
    i = pl.program_id(1)
    last = pl.num_programs(1) - 1
    xe_sc[0:HALO] = xp_ref[0]
    xe_sc[HALO:HALO + tm] = x_ref[0]
    xe_sc[HALO + tm:] = xn_ref[0]
    h = _rms(xe_sc[...], nw2_ref[...]) * (1.0 + sc_ref[0]) + sh_ref[0]
    h_sc[...] = h.astype(BF16)
    acc_sc[...] = jnp.zeros(acc_sc.shape, F32)

    def body(j, carry):
        u_sc[...] = _dot(h_sc[...], wup_ref[j])

        @pl.when(i == 0)
        def _():
            u_sc[0:HALO] = jnp.zeros((HALO, 2 * fc), F32)

        @pl.when(i == last)
        def _():
            u_sc[HALO + tm:] = jnp.zeros((HALO, 2 * fc), F32)

        cw = cw_ref[j]
        cv = cb_ref[j]
        for k in range(3):
            cv = cv + u_sc[HALO - 1 + k:HALO - 1 + k + tm, :] * cw[k:k + 1]
        g = _silu(cv[:, fc:]) * cv[:, :fc]
        acc_sc[...] += _dot(g.astype(BF16), wdn_ref[j])
        return carry

    lax.fori_loop(0, nch, body, 0)
    out_ref[0] = x_ref[0] + g_ref[0] * _rms(acc_sc[...], nw3_ref[...])


def _ffn(x, shift, scale, gate, nw2, nw3, wup, cw, cb, wdn, tm):
    b, t, d = x.shape
    shared = shift.shape[0] == 1
    nch, _, fc2 = wup.shape
    fc = fc2 // 2
    xm, xp, xn = _halo_specs(tm, d, t)
    return pl.pallas_call(
        functools.partial(_ffn_kernel, tm=tm, fc=fc, nch=nch),
        grid=(b, t // tm),
        in_specs=[
            xm, xp, xn,
            _row_spec(d, shared), _row_spec(d, shared), _row_spec(d, shared),
            _const_spec((1, d)), _const_spec((1, d)),
            _const_spec(wup.shape), _const_spec(cw.shape), _const_spec(cb.shape), _const_spec(wdn.shape),
        ],
        out_specs=pl.BlockSpec((1, tm, d), lambda bb, i: (bb, i, 0)),
        out_shape=jax.ShapeDtypeStruct((b, t, d), F32),
        scratch_shapes=[
            pltpu.VMEM((tm + 2 * HALO, d), F32),
            pltpu.VMEM((tm + 2 * HALO, d), BF16),
            pltpu.VMEM((tm + 2 * HALO, fc2), F32),
            pltpu.VMEM((tm, d), F32),
        ],
        compiler_params=_params("parallel", "parallel"),
        name="conv_ffn",
    )(x, x, x, shift, scale, gate, nw2, nw3, wup, cw, cb, wdn)


def _ssm_in_kernel(x_ref, xp_ref, xn_ref, sh_ref, sc_ref, nw_ref, wz_ref, wx_ref, wdt_ref,
                   cw_ref, cb_ref, dtb_ref, z_ref, xs_ref, bm_ref, cm_ref, dt_ref, xe_sc, u_sc, *, tm, cc):
    i = pl.program_id(1)
    last = pl.num_programs(1) - 1
    xe_sc[0:HALO] = xp_ref[0]
    xe_sc[HALO:HALO + tm] = x_ref[0]
    xe_sc[HALO + tm:] = xn_ref[0]
    hf = _rms(xe_sc[...], nw_ref[...]) * (1.0 + sc_ref[0]) + sh_ref[0]
    he = hf.astype(BF16)
    hm = hf[HALO:HALO + tm].astype(BF16)

    for j in range(D_INNER // cc):
        z_ref[0, :, j * cc:(j + 1) * cc] = _dot(hm, wz_ref[:, j * cc:(j + 1) * cc])

    dt_raw = _dot(hm, wdt_ref[...]) + dtb_ref[...]
    dt_ref[0] = jnp.maximum(dt_raw, 0.0) + jnp.log1p(jnp.exp(-jnp.abs(dt_raw)))

    conv_dim = D_INNER + 2 * BC_DIM
    for j in range(conv_dim // cc):
        lo = j * cc
        u_sc[...] = _dot(he, wx_ref[:, lo:lo + cc])

        @pl.when(i == 0)
        def _():
            u_sc[0:HALO] = jnp.zeros((HALO, cc), F32)

        @pl.when(i == last)
        def _():
            u_sc[HALO + tm:] = jnp.zeros((HALO, cc), F32)

        cv = cb_ref[:, lo:lo + cc]
        for k in range(4):
            cv = cv + u_sc[HALO - 2 + k:HALO - 2 + k + tm, :] * cw_ref[k:k + 1, lo:lo + cc]
        act = _silu(cv)
        if lo < D_INNER:
            xs_ref[0, :, lo:lo + cc] = act
        elif lo < D_INNER + BC_DIM:
            bm_ref[0, :, lo - D_INNER:lo - D_INNER + cc] = act.astype(BF16)
        else:
            cm_ref[0, :, lo - D_INNER - BC_DIM:lo - D_INNER - BC_DIM + cc] = act.astype(BF16)


def _ssm_in(x, shift, scale, nw, wz, wx, wdt, cw, cb, dtb, tm):
    b, t, d = x.shape
    shared = shift.shape[0] == 1
    cc = 512
    xm, xp, xn = _halo_specs(tm, d, t)
    outs = [(D_INNER, F32), (D_INNER, F32), (BC_DIM, BF16), (BC_DIM, BF16), (LANES, F32)]
    return pl.pallas_call(
        functools.partial(_ssm_in_kernel, tm=tm, cc=cc),
        grid=(b, t // tm),
        in_specs=[
            xm, xp, xn, _row_spec(d, shared), _row_spec(d, shared), _const_spec((1, d)),
            _const_spec(wz.shape), _const_spec(wx.shape), _const_spec(wdt.shape),
            _const_spec(cw.shape), _const_spec(cb.shape), _const_spec(dtb.shape),
        ],
        out_specs=[pl.BlockSpec((1, tm, c), lambda bb, i: (bb, i, 0)) for c, _ in outs],
        out_shape=[jax.ShapeDtypeStruct((b, t, c), dt) for c, dt in outs],
        scratch_shapes=[pltpu.VMEM((tm + 2 * HALO, d), F32), pltpu.VMEM((tm + 2 * HALO, cc), F32)],
        compiler_params=_params("parallel", "parallel"),
        name="ssm_in",
    )(x, x, x, shift, scale, nw, wz, wx, wdt, cw, cb, dtb)


def _pair_expand(arr, c1, lane_lo):
    return jnp.where(lane_lo, arr[:, c1:c1 + 1], arr[:, c1 + 1:c1 + 2])


def _ssd_chunk(xs, dt, bm, cm, a_row, h_sc, doff, reverse, y_store):
    q = SSD_CHUNK
    row = lax.broadcasted_iota(jnp.int32, (q, q), 0)
    col = lax.broadcasted_iota(jnp.int32, (q, q), 1)
    mask = (col >= row) if reverse else (col <= row)
    tri = jnp.where(mask, 1.0, 0.0).astype(BF16)
    lane_lo = lax.broadcasted_iota(jnp.int32, (1, LANES), 1) < SSM_HEAD_DIM

    a = dt * a_row
    a1 = a.astype(BF16)
    r1 = a - a1.astype(F32)
    a2 = r1.astype(BF16)
    a3 = (r1 - a2.astype(F32)).astype(BF16)
    acum = _dot(tri, a1) + _dot(tri, a2) + _dot(tri, a3)
    tot = acum[0:1] if reverse else acum[q - 1:q]
    e_in = jnp.exp(acum)
    e_out = jnp.exp(tot - acum)
    e_tot = jnp.exp(tot)
    if y_store is not None:
        acum_t = acum.T

    for g in range(SSM_GROUPS):
        b_g = bm[:, g * D_STATE:(g + 1) * D_STATE]
        c_g = cm[:, g * D_STATE:(g + 1) * D_STATE]
        h_g = h_sc[g]
        if y_store is not None:
            cb = lax.dot_general(c_g, b_g, (((1,), (1,)), ((), ())), preferred_element_type=F32)
            y_inter = _dot(c_g, h_g.astype(BF16))
        xd_parts = []
        etot_parts = []
        for j in range(HEADS_PER_GROUP // 2):
            hd = g * HEADS_PER_GROUP + 2 * j
            c1 = doff + hd
            lanes = slice(hd * SSM_HEAD_DIM, (hd + 2) * SSM_HEAD_DIM)
            xdt = xs[:, lanes] * _pair_expand(dt, c1, lane_lo)
            xd_parts.append((xdt * _pair_expand(e_out, c1, lane_lo)).astype(BF16))
            etot_parts.append(_pair_expand(e_tot, c1, lane_lo))
            if y_store is not None:
                lm = []
                for c in (c1, c1 + 1):
                    seg = acum[:, c:c + 1] - acum_t[c:c + 1, :]
                    lm.append(cb * jnp.exp(jnp.where(mask, seg, -jnp.inf)))
                m2 = jnp.concatenate(lm, axis=1).astype(BF16)
                xb = xdt.astype(BF16)
                zero = jnp.zeros_like(xb)
                rhs = jnp.concatenate([jnp.where(lane_lo, xb, zero), jnp.where(lane_lo, zero, xb)], axis=0)
                y_intra = _dot(m2, rhs)
                y_i = y_inter[:, 2 * j * SSM_HEAD_DIM:(2 * j + 2) * SSM_HEAD_DIM]
                y_store(hd // 2, y_intra + y_i * _pair_expand(e_in, c1, lane_lo))
        xd = jnp.concatenate(xd_parts, axis=1)
        upd = lax.dot_general(b_g, xd, (((0,), (0,)), ((), ())), preferred_element_type=F32)
        h_sc[g] = jnp.concatenate(etot_parts, axis=1) * h_g + upd


def _ssd_state_kernel(xs_ref, dt_ref, bm_ref, cm_ref, a_ref, hf_ref, hb_ref, h_sc, *, nchunks):
    q = SSD_CHUNK
    for reverse, out_ref in ((False, hf_ref), (True, hb_ref)):
        h_sc[...] = jnp.zeros(h_sc.shape, F32)
        order = range(nchunks - 1, -1, -1) if reverse else range(nchunks)
        for ci in order:
            rows = slice(ci * q, (ci + 1) * q)
            _ssd_chunk(xs_ref[0, rows], dt_ref[0, rows], bm_ref[0, rows], cm_ref[0, rows], a_ref[...],
                       h_sc, SSM_HEADS if reverse else 0, reverse, None)
        out_ref[0] = h_sc[...]


def _ssd_states(xs, dt, bm, cm, a_row):
    b, t, _ = xs.shape
    st = jax.ShapeDtypeStruct((b, SSM_GROUPS, D_STATE, GROUP_W), F32)
    full = lambda c: pl.BlockSpec((1, t, c), lambda bb: (bb, 0, 0))
    st_spec = pl.BlockSpec((1, SSM_GROUPS, D_STATE, GROUP_W), lambda bb: (bb, 0, 0, 0))
    return pl.pallas_call(
        functools.partial(_ssd_state_kernel, nchunks=t // SSD_CHUNK),
        grid=(b,),
        in_specs=[full(D_INNER), full(LANES), full(BC_DIM), full(BC_DIM), _const_spec((1, LANES))],
        out_specs=[st_spec, st_spec],
        out_shape=[st, st],
        scratch_shapes=[pltpu.VMEM((SSM_GROUPS, D_STATE, GROUP_W), F32)],
        compiler_params=_params("parallel"),
        name="ssd_ctx_states",
    )(xs, dt, bm, cm, a_row)


def _ssd_fwd_kernel(xs_ref, dt_ref, bm_ref, cm_ref, a_ref, h0_ref, y_ref, h_sc, *, cps):
    q = SSD_CHUNK

    @pl.when(pl.program_id(1) == 0)
    def _():
        h_sc[...] = h0_ref[0]

    for ci in range(cps):
        rows = slice(ci * q, (ci + 1) * q)

        def store(pair, val, rows=rows):
            y_ref[0, rows, pair * LANES:(pair + 1) * LANES] = val

        _ssd_chunk(xs_ref[0, rows], dt_ref[0, rows], bm_ref[0, rows], cm_ref[0, rows], a_ref[...],
                   h_sc, 0, False, store)


def _ssd_fwd(xs, dt, bm, cm, a_row, h0, cps):
    b, t, _ = xs.shape
    tm = cps * SSD_CHUNK
    blk = lambda c: pl.BlockSpec((1, tm, c), lambda bb, i: (bb, i, 0))
    return pl.pallas_call(
        functools.partial(_ssd_fwd_kernel, cps=cps),
        grid=(b, t // tm),
        in_specs=[blk(D_INNER), blk(LANES), blk(BC_DIM), blk(BC_DIM), _const_spec((1, LANES)),
                  pl.BlockSpec((1, SSM_GROUPS, D_STATE, GROUP_W), lambda bb, i: (bb, 0, 0, 0))],
        out_specs=blk(D_INNER),
        out_shape=jax.ShapeDtypeStruct((b, t, D_INNER), F32),
        scratch_shapes=[pltpu.VMEM((SSM_GROUPS, D_STATE, GROUP_W), F32)],
        compiler_params=_params("parallel", "arbitrary"),
        name="ssd_fwd",
    )(xs, dt, bm, cm, a_row, h0)


def _ssd_bwd_out_kernel(xs_ref, dt_ref, bm_ref, cm_ref, a_ref, h0_ref, yf_ref, z_ref, x_ref, g_ref,
                        dsk_ref, snw_ref, wo_ref, nw_ref, out_ref, h_sc, yb_sc, *, cps):
    q = SSD_CHUNK

    @pl.when(pl.program_id(1) == 0)
    def _():
        h_sc[...] = h0_ref[0]

    for ci in range(cps - 1, -1, -1):
        rows = slice(ci * q, (ci + 1) * q)

        def store(pair, val, rows=rows):
            yb_sc[rows, pair * LANES:(pair + 1) * LANES] = val

        _ssd_chunk(xs_ref[0, rows], dt_ref[0, rows], bm_ref[0, rows], cm_ref[0, rows], a_ref[...],
                   h_sc, SSM_HEADS, True, store)

    y = yf_ref[0] + yb_sc[...] + dsk_ref[...] * xs_ref[0]
    yn = _rms(y * _silu(z_ref[0]), snw_ref[...])
    o = _dot(yn.astype(BF16), wo_ref[...])
    out_ref[0] = x_ref[0] + g_ref[0] * _rms(o, nw_ref[...])


def _ssd_bwd_out(xs, dt, bm, cm, a_row, h0, yf, z, x, gate, dskip, snw, w_out, nw, cps):
    b, t, d = x.shape
    tm = cps * SSD_CHUNK
    nblk = t // tm
    blk = lambda c: pl.BlockSpec((1, tm, c), lambda bb, i: (bb, nblk - 1 - i, 0))
    return pl.pallas_call(
        functools.partial(_ssd_bwd_out_kernel, cps=cps),
        grid=(b, nblk),
        in_specs=[blk(D_INNER), blk(LANES), blk(BC_DIM), blk(BC_DIM), _const_spec((1, LANES)),
                  pl.BlockSpec((1, SSM_GROUPS, D_STATE, GROUP_W), lambda bb, i: (bb, 0, 0, 0)),
                  blk(D_INNER), blk(D_INNER), blk(d),
                  pl.BlockSpec((1, 1, d), lambda bb, i: (bb, 0, 0)),
                  _const_spec((1, D_INNER)), _const_spec((1, D_INNER)), _const_spec((D_INNER, d)),
                  _const_spec((1, d))],
        out_specs=blk(d),
        out_shape=jax.ShapeDtypeStruct((b, t, d), F32),
        scratch_shapes=[pltpu.VMEM((SSM_GROUPS, D_STATE, GROUP_W), F32), pltpu.VMEM((tm, D_INNER), F32)],
        compiler_params=_params("parallel", "arbitrary"),
        name="ssd_bwd_out",
    )(xs, dt, bm, cm, a_row, h0, yf, z, x, gate, dskip, snw, w_out, nw)


def _rope_tables(t):
    rows = t // GRID_W
    row = jnp.repeat(jnp.arange(rows, dtype=F32), GRID_W)
    col = jnp.tile(jnp.arange(GRID_W, dtype=F32), rows)
    half = HEAD_DIM // 2
    inv_freq = ROPE_THETA ** (-jnp.arange(0, half, 2, dtype=F32) / half)
    ang = jnp.concatenate([row[:, None] * inv_freq, col[:, None] * inv_freq], axis=-1)
    cos, sin = jnp.cos(ang), jnp.sin(ang)
    return jnp.concatenate([cos, cos], axis=-1), jnp.concatenate([-sin, sin], axis=-1)


def _tile(t, pref):
    return pref if t % pref == 0 else t


def kernel(x, c, ctx, c_ctx, ada_w, ada_b, norm_w, attn_w_in, pool_w, pool_scale, q_gain, k_gain, attn_w_out,
           ssm_w_in, ssm_conv_w, ssm_conv_b, ssm_A_log, ssm_dt_bias, ssm_D, ssm_norm_w, ssm_w_out,
           ffn_w_up, ffn_conv_w, ffn_conv_b, ffn_w_down):
    bsz, t, d = x.shape
    lc = ctx.shape[1]
    depth = ada_w.shape[0]
    assert depth == 2 and attn_w_in.shape[0] == 1 and ssm_w_in.shape[0] == 1
    d_ff = ffn_w_down.shape[1]

    rows = -(-(bsz + 1) // HALO) * HALO
    cond = jnp.zeros((rows, d), F32).at[:bsz].set(c).at[bsz].set(c_ctx)
    ada = _adaln(cond, ada_w, ada_b)
    mod_l = [[ada[i, :bsz, k * d:(k + 1) * d].reshape(bsz, 1, d) for k in range(6)] for i in range(depth)]
    mod_c = [[ada[i, bsz:bsz + 1, k * d:(k + 1) * d].reshape(1, 1, d) for k in range(6)] for i in range(depth)]
    nw = lambda i, k: norm_w[i, k].reshape(1, d)

    fc = 256
    nch = d_ff // fc

    def ffn_weights(i):
        wu = ffn_w_up[i].astype(BF16)
        wup = jnp.concatenate([wu[:, :d_ff].reshape(d, nch, fc), wu[:, d_ff:].reshape(d, nch, fc)], axis=-1)
        wup = wup.transpose(1, 0, 2)
        cwt = ffn_conv_w[i].T
        cw = jnp.concatenate([cwt[:, :d_ff].reshape(3, nch, fc), cwt[:, d_ff:].reshape(3, nch, fc)], axis=-1)
        cw = cw.transpose(1, 0, 2)
        cb = jnp.concatenate([ffn_conv_b[i][:d_ff].reshape(nch, 1, fc), ffn_conv_b[i][d_ff:].reshape(nch, 1, fc)],
                             axis=-1)
        wdn = ffn_w_down[i].astype(BF16).reshape(nch, fc, d)
        return wup, cw, cb, wdn

    def ffn(xx, mods, i, tm):
        wup, cw, cb, wdn = ffn_weights(i)
        return _ffn(xx, mods[3], mods[4], mods[5], nw(i, 2), nw(i, 3), wup, cw, cb, wdn, tm)

    perm = jnp.concatenate([jnp.arange(0, HEAD_DIM, 2), jnp.arange(1, HEAD_DIM, 2)])
    cols = jnp.arange(attn_w_in.shape[2])
    qk_lo, qk_hi = POOL_DIM, POOL_DIM + ATTN_DIM + KV_DIM
    qk_cols = (qk_lo + (jnp.arange(qk_hi - qk_lo) // HEAD_DIM) * HEAD_DIM
               + perm[jnp.arange(qk_hi - qk_lo) % HEAD_DIM])
    cols = cols.at[qk_lo:qk_hi].set(qk_cols)
    w_in_a = attn_w_in[0][:, cols].astype(BF16)
    qg = q_gain[0][perm].reshape(1, HEAD_DIM)
    kg = k_gain[0][perm].reshape(1, HEAD_DIM)
    cos2, sin2 = _rope_tables(t)
    eye = jnp.eye(len(POOL_WINDOWS), dtype=F32)
    pool_bd = (eye[:, None, :, None] * pool_w[0][:, :, None, :]).reshape(POOL_DIM, POOL_DIM).astype(BF16)
    ps = pool_scale[0].reshape(1, POOL_DIM)
    w_out_a = attn_w_out[0][:POOL_DIM].astype(BF16)
    w_out_o = attn_w_out[0][POOL_DIM:].astype(BF16)

    tm_l = _tile(t, 512)
    tm_c = _tile(lc, 512)
    a_c, q_c, k_c, v_c = _attn_in(ctx, mod_c[0][0], mod_c[0][1], nw(0, 0), w_in_a, qg, kg, None, None, tm_c)
    a_l, q_l, k_l, v_l = _attn_in(x, mod_l[0][0], mod_l[0][1], nw(0, 0), w_in_a, qg, kg, cos2, sin2, tm_l)
    o_l = _attention(q_l, [(k_c, v_c), (k_l, v_l)], _tile(t, 256), 512)
    o_c = _attention(q_c, [(k_c, v_c)], _tile(lc, 256), 512)
    x = _attn_out(x, mod_l[0][2], nw(0, 1), a_l, o_l, pool_bd, ps, w_out_a, w_out_o, tm_l)
    ctx = _attn_out(ctx, mod_c[0][2], nw(0, 1), a_c, o_c, pool_bd, ps, w_out_a, w_out_o, tm_c)
    x = ffn(x, mod_l[0], 0, tm_l)
    ctx = ffn(ctx, mod_c[0], 0, tm_c)

    w_in_c = ssm_w_in[0].astype(BF16)
    conv_dim = D_INNER + 2 * BC_DIM
    wz = w_in_c[:, :D_INNER]
    wx = w_in_c[:, D_INNER:D_INNER + conv_dim]
    ndt = 2 * SSM_HEADS
    wdt = jnp.zeros((d, LANES), BF16).at[:, :ndt].set(w_in_c[:, D_INNER + conv_dim:])
    dtb = jnp.zeros((1, LANES), F32).at[0, :ndt].set(ssm_dt_bias[0].reshape(ndt))
    a_row = jnp.zeros((1, LANES), F32).at[0, :ndt].set(-jnp.exp(ssm_A_log[0].astype(F32)).reshape(ndt))
    scw = ssm_conv_w[0].T
    scb = ssm_conv_b[0].reshape(1, conv_dim)
    dskip = jnp.repeat(ssm_D[0], SSM_HEAD_DIM).reshape(1, D_INNER)
    snw = ssm_norm_w[0].reshape(1, D_INNER)
    w_out_c = ssm_w_out[0].astype(BF16)

    tm_s = _tile(t, 256)
    tm_sc = _tile(lc, 256)
    _, xs_c, bm_c, cm_c, dt_c = _ssm_in(ctx, mod_c[1][0], mod_c[1][1], nw(1, 0), wz, wx, wdt, scw, scb, dtb, tm_sc)
    hf0, hb0 = _ssd_states(xs_c, dt_c, bm_c, cm_c, a_row)
    z_l, xs_l, bm_l, cm_l, dt_l = _ssm_in(x, mod_l[1][0], mod_l[1][1], nw(1, 0), wz, wx, wdt, scw, scb, dtb, tm_s)
    cps = 2 if t % (2 * SSD_CHUNK) == 0 else 1
    yf = _ssd_fwd(xs_l, dt_l, bm_l, cm_l, a_row, hf0, cps)
    x = _ssd_bwd_out(xs_l, dt_l, bm_l, cm_l, a_row, hb0, yf, z_l, x, mod_l[1][2], dskip, snw, w_out_c,
                     nw(1, 1), cps)
    x = ffn(x, mod_l[1], 1, tm_l)
    return x
```

```python
import functools
import math

import jax
import jax.numpy as jnp
from jax import lax
from jax.experimental import pallas as pl
from jax.experimental.pallas import tpu as pltpu

F32 = jnp.float32
BF16 = jnp.bfloat16

EPS = 1e-6
GRID_W = 64
ROPE_THETA = 10000.0

HEAD_DIM = 128
N_Q_HEADS = 6
N_KV_HEADS = 2
Q_PER_KV = N_Q_HEADS // N_KV_HEADS
POOL_WINDOWS = (2, 4, 8, 16)
POOL_GROUP_DIM = 64
POOL_DIM = 256
ATTN_DIM = N_Q_HEADS * HEAD_DIM
KV_DIM = N_KV_HEADS * HEAD_DIM

SSM_HEAD_DIM = 64
SSM_HEADS = 32
SSM_GROUPS = 4
HEADS_PER_GROUP = 8
D_STATE = 128
D_INNER = SSM_HEADS * SSM_HEAD_DIM
GROUP_W = HEADS_PER_GROUP * SSM_HEAD_DIM
BC_DIM = SSM_GROUPS * D_STATE
SSD_CHUNK = 128

HALO = 8
LANES = 128
VMEM_LIMIT = 56 * 1024 * 1024


def _params(*sem):
    return pltpu.CompilerParams(dimension_semantics=sem, vmem_limit_bytes=VMEM_LIMIT)


def _const_spec(shape):
    nd = len(shape)
    return pl.BlockSpec(shape, lambda *_: (0,) * nd, pipeline_mode=pl.Buffered(1))


def _row_spec(c, shared):
    if shared:
        return pl.BlockSpec((1, 1, c), lambda b, i: (0, 0, 0))
    return pl.BlockSpec((1, 1, c), lambda b, i: (b, 0, 0))


def _halo_specs(tm, c, t):
    nb = tm // HALO
    last = t // HALO - 1
    main = pl.BlockSpec((1, tm, c), lambda b, i: (b, i, 0))
    prev = pl.BlockSpec((1, HALO, c), lambda b, i: (b, jnp.maximum(i * nb - 1, 0), 0))
    nxt = pl.BlockSpec((1, HALO, c), lambda b, i: (b, jnp.minimum((i + 1) * nb, last), 0))
    return main, prev, nxt


def _rms(xf, w):
    ms = jnp.mean(xf * xf, axis=-1, keepdims=True)
    return xf * lax.rsqrt(ms + EPS) * w


def _silu(x):
    return x * jax.nn.sigmoid(x)


def _dot(a, b):
    return jnp.dot(a, b, preferred_element_type=F32)


def _ada_kernel(c_ref, w_ref, b_ref, o_ref):
    s = _silu(c_ref[...])
    o_ref[0] = _dot(s.astype(BF16), w_ref[0].astype(BF16)) + b_ref[0]


def _adaln(cond, ada_w, ada_b):
    depth, d, n = ada_w.shape
    rows = cond.shape[0]
    tn = 1536
    return pl.pallas_call(
        _ada_kernel,
        grid=(depth, n // tn),
        in_specs=[
            pl.BlockSpec((rows, d), lambda l, j: (0, 0)),
            pl.BlockSpec((1, d, tn), lambda l, j: (l, 0, j)),
            pl.BlockSpec((1, 1, tn), lambda l, j: (l, 0, j)),
        ],
        out_specs=pl.BlockSpec((1, rows, tn), lambda l, j: (l, 0, j)),
        out_shape=jax.ShapeDtypeStruct((depth, rows, n), F32),
        compiler_params=_params("parallel", "parallel"),
        name="adaln",
    )(cond, ada_w, ada_b.reshape(depth, 1, n))


def _attn_in_kernel(*refs, rope):
    if rope:
        (x_ref, sh_ref, sc_ref, nw_ref, w_ref, qg_ref, kg_ref, cos_ref, sin_ref,
         a_ref, q_ref, k_ref, v_ref) = refs
        cos2, sin2 = cos_ref[...], sin_ref[...]
    else:
        (x_ref, sh_ref, sc_ref, nw_ref, w_ref, qg_ref, kg_ref,
         a_ref, q_ref, k_ref, v_ref) = refs
    h = _rms(x_ref[0], nw_ref[...]) * (1.0 + sc_ref[0]) + sh_ref[0]
    u = _dot(h.astype(BF16), w_ref[...])
    a_ref[0] = u[:, :POOL_DIM]

    def norm_rope(t, gain):
        y = _rms(t, gain)
        if rope:
            y = y * cos2 + pltpu.roll(y, HEAD_DIM // 2, 1) * sin2
        return y.astype(BF16)

    off = POOL_DIM
    for hh in range(N_Q_HEADS):
        q_ref[0, :, hh * HEAD_DIM:(hh + 1) * HEAD_DIM] = norm_rope(
            u[:, off + hh * HEAD_DIM: off + (hh + 1) * HEAD_DIM], qg_ref[...])
    off += ATTN_DIM
    for hh in range(N_KV_HEADS):
        k_ref[0, :, hh * HEAD_DIM:(hh + 1) * HEAD_DIM] = norm_rope(
            u[:, off + hh * HEAD_DIM: off + (hh + 1) * HEAD_DIM], kg_ref[...])
    off += KV_DIM
    v_ref[0] = u[:, off:].astype(BF16)


def _attn_in(x, shift, scale, nw, w_in, qg, kg, cos2, sin2, tm):
    b, t, d = x.shape
    shared = shift.shape[0] == 1
    rope = cos2 is not None
    width = w_in.shape[1]
    in_specs = [
        pl.BlockSpec((1, tm, d), lambda bb, i: (bb, i, 0)),
        _row_spec(d, shared), _row_spec(d, shared),
        _const_spec((1, d)), _const_spec((d, width)),
        _const_spec((1, HEAD_DIM)), _const_spec((1, HEAD_DIM)),
    ]
    args = [x, shift, scale, nw, w_in, qg, kg]
    if rope:
        in_specs += [pl.BlockSpec((tm, HEAD_DIM), lambda bb, i: (i, 0))] * 2
        args += [cos2, sin2]
    outs = [(POOL_DIM, F32), (ATTN_DIM, BF16), (KV_DIM, BF16), (KV_DIM, BF16)]
    return pl.pallas_call(
        functools.partial(_attn_in_kernel, rope=rope),
        grid=(b, t // tm),
        in_specs=in_specs,
        out_specs=[pl.BlockSpec((1, tm, c), lambda bb, i: (bb, i, 0)) for c, _ in outs],
        out_shape=[jax.ShapeDtypeStruct((b, t, c), dt) for c, dt in outs],
        compiler_params=_params("parallel", "parallel"),
        name="attn_in",
    )(*args)


SUM_ROWS = 16


def _attn_kernel(q_ref, k_ref, vt_ref, o_ref, m_sc, acc_sc, s0_sc, mc0_sc, s1_sc, mc1_sc, *, n, tq):
    slots = ((s0_sc, mc0_sc), (s1_sc, mc1_sc))
    c = (HEAD_DIM ** -0.5) * math.log2(math.e)

    q = q_ref[0]
    qs = jnp.concatenate([q[:, g * HEAD_DIM:(g + 1) * HEAD_DIM] for g in range(Q_PER_KV)], axis=0)
    m_sc[...] = jnp.full(m_sc.shape, -jnp.inf, F32)
    acc_sc[...] = jnp.zeros(acc_sc.shape, F32)

    def produce(j, slot, g):
        s_sc, mc_sc = slots[slot]
        cols = slice(g * tq, (g + 1) * tq)
        s = lax.dot_general(k_ref[0, 0, j], qs[cols], (((1,), (1,)), ((), ())),
                            preferred_element_type=F32)
        s_sc[:, cols] = s
        mc_sc[:, cols] = jnp.max(s, axis=0, keepdims=True)

    def consume(j, slot, g):
        s_sc, mc_sc = slots[slot]
        cols = slice(g * tq, (g + 1) * tq)
        m_prev = m_sc[:, cols]
        m_new = jnp.maximum(m_prev, mc_sc[:, cols])
        alpha = jnp.exp2((m_prev - m_new) * c)
        p = jnp.exp2((s_sc[:, cols] - m_new) * c).astype(BF16)
        acc_sc[:, cols] = alpha * acc_sc[:, cols] + _dot(vt_ref[0, 0, j], p)
        m_sc[:, cols] = m_new

    def step(jp, sp, jc, sc):
        for g in range(Q_PER_KV):
            if jp is not None:
                produce(jp, sp, g)
            if jc is not None:
                consume(jc, sc, g)

    step(0, 0, None, None)

    def body(jj, carry):
        step(2 * jj + 1, 1, 2 * jj, 0)
        step(2 * jj + 2, 0, 2 * jj + 1, 1)
        return carry

    npairs = (n - 1) // 2
    lax.fori_loop(0, npairs, body, 0)
    if n % 2 == 0:
        step(n - 1, 1, n - 2, 0)
    step(None, None, n - 1, (n - 1) % 2)

    acc = acc_sc[...]
    o_t = acc[:HEAD_DIM] / acc[HEAD_DIM:HEAD_DIM + 1]
    for g in range(Q_PER_KV):
        o_ref[0, :, g * HEAD_DIM:(g + 1) * HEAD_DIM] = o_t[:, g * tq:(g + 1) * tq].T.astype(BF16)


def _attention(q, k, v, tq, tk):
    b, t, _ = q.shape
    l = k.shape[1]
    gw = Q_PER_KV * HEAD_DIM
    tk = min(tk, l)
    pad = (-l) % tk
    assert pad <= l
    n = (l + pad) // tk
    k = jnp.concatenate([k, k[:, :pad]], axis=1)
    v = jnp.concatenate([v, jnp.zeros((b, pad, KV_DIM), BF16)], axis=1)
    live = (jnp.arange(l + pad) < l).astype(BF16).reshape(n, 1, tk)
    kc = k.reshape(b, n, tk, N_KV_HEADS, HEAD_DIM).transpose(0, 3, 1, 2, 4)
    vt = v.reshape(b, n, tk, N_KV_HEADS, HEAD_DIM).transpose(0, 3, 1, 4, 2)
    vt = jnp.concatenate([vt, jnp.broadcast_to(live, (b, N_KV_HEADS, n, SUM_ROWS, tk))], axis=3)
    m = Q_PER_KV * tq
    return pl.pallas_call(
        functools.partial(_attn_kernel, n=n, tq=tq),
        grid=(b, N_KV_HEADS, t // tq),
        in_specs=[
            pl.BlockSpec((1, tq, gw), lambda bb, g, i: (bb, i, g)),
            pl.BlockSpec((1, 1) + kc.shape[2:], lambda bb, g, i: (bb, g, 0, 0, 0)),
            pl.BlockSpec((1, 1) + vt.shape[2:], lambda bb, g, i: (bb, g, 0, 0, 0)),
        ],
        out_specs=pl.BlockSpec((1, tq, gw), lambda bb, g, i: (bb, i, g)),
        out_shape=jax.ShapeDtypeStruct((b, t, ATTN_DIM), BF16),
        scratch_shapes=[pltpu.VMEM((1, m), F32), pltpu.VMEM((HEAD_DIM + SUM_ROWS, m), F32)]
        + [pltpu.VMEM((tk, m), F32), pltpu.VMEM((1, m), F32)] * 2,
        compiler_params=_params("parallel", "parallel", "arbitrary"),
        name="attention",
    )(q, kc, vt)


def _attn_out_kernel(x_ref, g_ref, nw_ref, a_ref, ap_ref, an_ref, o_ref, pw_ref, ps_ref, wa_ref, wo_ref,
                     out_ref, abuf, *, tm, t):
    i = pl.program_id(1)
    last = pl.num_programs(1) - 1
    abuf[0:HALO] = jnp.where(i == 0, 0.0, ap_ref[0])
    abuf[HALO:HALO + tm] = a_ref[0]
    abuf[HALO + tm:] = jnp.where(i == last, 0.0, an_ref[0])

    def sh(d):
        return abuf[HALO + d:HALO + d + tm, :]

    x0 = sh(0)
    s2 = sh(-1) + x0
    s4 = s2 + sh(-2) + sh(1)
    s8 = s4 + sh(-4) + sh(-3) + sh(2) + sh(3)
    s16 = s8 + sh(-8) + sh(-7) + sh(-6) + sh(-5) + sh(4) + sh(5) + sh(6) + sh(7)
    pos = i * tm + lax.broadcasted_iota(jnp.int32, (tm, 1), 0)
    lane = lax.broadcasted_iota(jnp.int32, (1, POOL_DIM), 1)

    def mean(s, w):
        left = w // 2
        right = w - 1 - left
        cnt = jnp.minimum(pos + right, t - 1) - jnp.maximum(pos - left, 0) + 1
        return s / cnt.astype(F32)

    pm = jnp.where(lane < 64, mean(s2, 2),
                   jnp.where(lane < 128, mean(s4, 4),
                             jnp.where(lane < 192, mean(s8, 8), mean(s16, 16))))
    p = pm - x0
    py = _dot(p.astype(BF16), pw_ref[...]) * ps_ref[...]
    y = _dot(py.astype(BF16), wa_ref[...]) + _dot(o_ref[0], wo_ref[...])
    out_ref[0] = x_ref[0] + g_ref[0] * _rms(y, nw_ref[...])


def _attn_out(x, gate, nw, a, o, pool_bd, pool_scale, w_a, w_o, tm):
    b, t, d = x.shape
    shared = gate.shape[0] == 1
    am, ap, an = _halo_specs(tm, POOL_DIM, t)
    return pl.pallas_call(
        functools.partial(_attn_out_kernel, tm=tm, t=t),
        grid=(b, t // tm),
        in_specs=[
            pl.BlockSpec((1, tm, d), lambda bb, i: (bb, i, 0)),
            _row_spec(d, shared), _const_spec((1, d)),
            am, ap, an,
            pl.BlockSpec((1, tm, ATTN_DIM), lambda bb, i: (bb, i, 0)),
            _const_spec((POOL_DIM, POOL_DIM)), _const_spec((1, POOL_DIM)),
            _const_spec((POOL_DIM, d)), _const_spec((ATTN_DIM, d)),
        ],
        out_specs=pl.BlockSpec((1, tm, d), lambda bb, i: (bb, i, 0)),
        out_shape=jax.ShapeDtypeStruct((b, t, d), F32),
        scratch_shapes=[pltpu.VMEM((tm + 2 * HALO, POOL_DIM), F32)],
        compiler_params=_params("parallel", "parallel"),
        name="attn_out",
    )(x, gate, nw, a, a, a, o, pool_bd, pool_scale, w_a, w_o)


def _ffn_kernel(x_ref, xp_ref, xn_ref, sh_ref, sc_ref, g_ref, nw2_ref, nw3_ref,
                wup_ref, cw_ref, cb_ref, wdn_ref, out_ref, xe_sc, h_sc, u0_sc, u1_sc, acc_sc, *, tm, fc, nch):
    i = pl.program_id(1)
    last = pl.num_programs(1) - 1
    xe_sc[0:HALO] = xp_ref[0]
    xe_sc[HALO:HALO + tm] = x_ref[0]
    xe_sc[HALO + tm:] = xn_ref[0]
    h = _rms(xe_sc[...], nw2_ref[...]) * (1.0 + sc_ref[0]) + sh_ref[0]
    h_sc[...] = h.astype(BF16)
    acc_sc[...] = jnp.zeros(acc_sc.shape, F32)
    keep_top = jnp.where(i == 0, 0.0, 1.0)
    keep_bot = jnp.where(i == last, 0.0, 1.0)
    slots = (u0_sc, u1_sc)

    def produce(j, slot):
        u = _dot(h_sc[...], wup_ref[j])
        u_sc = slots[slot]
        u_sc[0:HALO] = u[0:HALO] * keep_top
        u_sc[HALO:HALO + tm] = u[HALO:HALO + tm]
        u_sc[HALO + tm:] = u[HALO + tm:] * keep_bot

    def consume(j, slot):
        u_sc = slots[slot]
        cw = cw_ref[j]
        cv = cb_ref[j]
        for k in range(3):
            cv = cv + u_sc[HALO - 1 + k:HALO - 1 + k + tm, :] * cw[k:k + 1]
        g = _silu(cv[:, fc:]) * cv[:, :fc]
        acc_sc[...] += _dot(g.astype(BF16), wdn_ref[j])

    produce(0, 0)

    def body(jj, carry):
        produce(2 * jj + 1, 1)
        consume(2 * jj, 0)
        produce(2 * jj + 2, 0)
        consume(2 * jj + 1, 1)
        return carry

    lax.fori_loop(0, (nch - 1) // 2, body, 0)
    if nch % 2 == 0:
        produce(nch - 1, 1)
        consume(nch - 2, 0)
    consume(nch - 1, (nch - 1) % 2)
    out_ref[0] = x_ref[0] + g_ref[0] * _rms(acc_sc[...], nw3_ref[...])


def _ffn(x, shift, scale, gate, nw2, nw3, wup, cw, cb, wdn, tm):
    b, t, d = x.shape
    shared = shift.shape[0] == 1
    nch, _, fc2 = wup.shape
    fc = fc2 // 2
    xm, xp, xn = _halo_specs(tm, d, t)
    return pl.pallas_call(
        functools.partial(_ffn_kernel, tm=tm, fc=fc, nch=nch),
        grid=(b, t // tm),
        in_specs=[
            xm, xp, xn,
            _row_spec(d, shared), _row_spec(d, shared), _row_spec(d, shared),
            _const_spec((1, d)), _const_spec((1, d)),
            _const_spec(wup.shape), _const_spec(cw.shape), _const_spec(cb.shape), _const_spec(wdn.shape),
        ],
        out_specs=pl.BlockSpec((1, tm, d), lambda bb, i: (bb, i, 0)),
        out_shape=jax.ShapeDtypeStruct((b, t, d), F32),
        scratch_shapes=[
            pltpu.VMEM((tm + 2 * HALO, d), F32),
            pltpu.VMEM((tm + 2 * HALO, d), BF16),
            pltpu.VMEM((tm + 2 * HALO, fc2), F32),
            pltpu.VMEM((tm + 2 * HALO, fc2), F32),
            pltpu.VMEM((tm, d), F32),
        ],
        compiler_params=_params("parallel", "parallel"),
        name="conv_ffn",
    )(x, x, x, shift, scale, gate, nw2, nw3, wup, cw, cb, wdn)


def _ssm_in_kernel(x_ref, xp_ref, xn_ref, sh_ref, sc_ref, nw_ref, wz_ref, wx_ref, wdt_ref,
                   cw_ref, cb_ref, dtb_ref, z_ref, xs_ref, bm_ref, cm_ref, dt_ref, xe_sc, u_sc, *, tm, cc):
    i = pl.program_id(1)
    last = pl.num_programs(1) - 1
    xe_sc[0:HALO] = xp_ref[0]
    xe_sc[HALO:HALO + tm] = x_ref[0]
    xe_sc[HALO + tm:] = xn_ref[0]
    hf = _rms(xe_sc[...], nw_ref[...]) * (1.0 + sc_ref[0]) + sh_ref[0]
    he = hf.astype(BF16)
    hm = hf[HALO:HALO + tm].astype(BF16)

    for j in range(D_INNER // cc):
        z_ref[0, :, j * cc:(j + 1) * cc] = _dot(hm, wz_ref[:, j * cc:(j + 1) * cc])

    dt_raw = _dot(hm, wdt_ref[...]) + dtb_ref[...]
    dt_ref[0] = jnp.maximum(dt_raw, 0.0) + jnp.log1p(jnp.exp(-jnp.abs(dt_raw)))

    conv_dim = D_INNER + 2 * BC_DIM
    for j in range(conv_dim // cc):
        lo = j * cc
        u_sc[...] = _dot(he, wx_ref[:, lo:lo + cc])

        @pl.when(i == 0)
        def _():
            u_sc[0:HALO] = jnp.zeros((HALO, cc), F32)

        @pl.when(i == last)
        def _():
            u_sc[HALO + tm:] = jnp.zeros((HALO, cc), F32)

        cv = cb_ref[:, lo:lo + cc]
        for k in range(4):
            cv = cv + u_sc[HALO - 2 + k:HALO - 2 + k + tm, :] * cw_ref[k:k + 1, lo:lo + cc]
        act = _silu(cv)
        if lo < D_INNER:
            xs_ref[0, :, lo:lo + cc] = act
        elif lo < D_INNER + BC_DIM:
            bm_ref[0, :, lo - D_INNER:lo - D_INNER + cc] = act.astype(BF16)
        else:
            cm_ref[0, :, lo - D_INNER - BC_DIM:lo - D_INNER - BC_DIM + cc] = act.astype(BF16)


def _ssm_in(x, shift, scale, nw, wz, wx, wdt, cw, cb, dtb, tm):
    b, t, d = x.shape
    shared = shift.shape[0] == 1
    cc = 512
    xm, xp, xn = _halo_specs(tm, d, t)
    outs = [(D_INNER, F32), (D_INNER, F32), (BC_DIM, BF16), (BC_DIM, BF16), (LANES, F32)]
    return pl.pallas_call(
        functools.partial(_ssm_in_kernel, tm=tm, cc=cc),
        grid=(b, t // tm),
        in_specs=[
            xm, xp, xn, _row_spec(d, shared), _row_spec(d, shared), _const_spec((1, d)),
            _const_spec(wz.shape), _const_spec(wx.shape), _const_spec(wdt.shape),
            _const_spec(cw.shape), _const_spec(cb.shape), _const_spec(dtb.shape),
        ],
        out_specs=[pl.BlockSpec((1, tm, c), lambda bb, i: (bb, i, 0)) for c, _ in outs],
        out_shape=[jax.ShapeDtypeStruct((b, t, c), dt) for c, dt in outs],
        scratch_shapes=[pltpu.VMEM((tm + 2 * HALO, d), F32), pltpu.VMEM((tm + 2 * HALO, cc), F32)],
        compiler_params=_params("parallel", "parallel"),
        name="ssm_in",
    )(x, x, x, shift, scale, nw, wz, wx, wdt, cw, cb, dtb)


def _pair_expand(arr, c1, lane_lo):
    return jnp.where(lane_lo, arr[:, c1:c1 + 1], arr[:, c1 + 1:c1 + 2])


def _ssd_chunk(xs, dt, bm, cm, a_row, h_sc, doff, reverse, y_store):
    q = SSD_CHUNK
    row = lax.broadcasted_iota(jnp.int32, (q, q), 0)
    col = lax.broadcasted_iota(jnp.int32, (q, q), 1)
    mask = (col >= row) if reverse else (col <= row)
    tri = jnp.where(mask, 1.0, 0.0).astype(BF16)
    lane_lo = lax.broadcasted_iota(jnp.int32, (1, LANES), 1) < SSM_HEAD_DIM

    a = dt * a_row
    a1 = a.astype(BF16)
    r1 = a - a1.astype(F32)
    a2 = r1.astype(BF16)
    a3 = (r1 - a2.astype(F32)).astype(BF16)
    acum = _dot(tri, a1) + _dot(tri, a2) + _dot(tri, a3)
    tot = acum[0:1] if reverse else acum[q - 1:q]
    e_in = jnp.exp(acum)
    e_out = jnp.exp(tot - acum)
    e_tot = jnp.exp(tot)
    if y_store is not None:
        acum_t = acum.T

    for g in range(SSM_GROUPS):
        b_g = bm[:, g * D_STATE:(g + 1) * D_STATE]
        c_g = cm[:, g * D_STATE:(g + 1) * D_STATE]
        h_g = h_sc[g]
        if y_store is not None:
            cb = lax.dot_general(c_g, b_g, (((1,), (1,)), ((), ())), preferred_element_type=F32)
            y_inter = _dot(c_g, h_g.astype(BF16))
        xd_parts = []
        etot_parts = []
        for j in range(HEADS_PER_GROUP // 2):
            hd = g * HEADS_PER_GROUP + 2 * j
            c1 = doff + hd
            lanes = slice(hd * SSM_HEAD_DIM, (hd + 2) * SSM_HEAD_DIM)
            xdt = xs[:, lanes] * _pair_expand(dt, c1, lane_lo)
            xd_parts.append((xdt * _pair_expand(e_out, c1, lane_lo)).astype(BF16))
            etot_parts.append(_pair_expand(e_tot, c1, lane_lo))
            if y_store is not None:
                lm = []
                for c in (c1, c1 + 1):
                    seg = acum[:, c:c + 1] - acum_t[c:c + 1, :]
                    lm.append(cb * jnp.exp(jnp.where(mask, seg, -jnp.inf)))
                m2 = jnp.concatenate(lm, axis=1).astype(BF16)
                xb = xdt.astype(BF16)
                zero = jnp.zeros_like(xb)
                rhs = jnp.concatenate([jnp.where(lane_lo, xb, zero), jnp.where(lane_lo, zero, xb)], axis=0)
                y_intra = _dot(m2, rhs)
                y_i = y_inter[:, 2 * j * SSM_HEAD_DIM:(2 * j + 2) * SSM_HEAD_DIM]
                y_store(hd // 2, y_intra + y_i * _pair_expand(e_in, c1, lane_lo))
        xd = jnp.concatenate(xd_parts, axis=1)
        upd = lax.dot_general(b_g, xd, (((0,), (0,)), ((), ())), preferred_element_type=F32)
        h_sc[g] = jnp.concatenate(etot_parts, axis=1) * h_g + upd


def _ssd_state_kernel(xs_ref, dt_ref, bm_ref, cm_ref, a_ref, hf_ref, hb_ref, h_sc, *, nchunks):
    q = SSD_CHUNK
    for reverse, out_ref in ((False, hf_ref), (True, hb_ref)):
        h_sc[...] = jnp.zeros(h_sc.shape, F32)
        order = range(nchunks - 1, -1, -1) if reverse else range(nchunks)
        for ci in order:
            rows = slice(ci * q, (ci + 1) * q)
            _ssd_chunk(xs_ref[0, rows], dt_ref[0, rows], bm_ref[0, rows], cm_ref[0, rows], a_ref[...],
                       h_sc, SSM_HEADS if reverse else 0, reverse, None)
        out_ref[0] = h_sc[...]


def _ssd_states(xs, dt, bm, cm, a_row):
    b, t, _ = xs.shape
    st = jax.ShapeDtypeStruct((b, SSM_GROUPS, D_STATE, GROUP_W), F32)
    full = lambda c: pl.BlockSpec((1, t, c), lambda bb: (bb, 0, 0))
    st_spec = pl.BlockSpec((1, SSM_GROUPS, D_STATE, GROUP_W), lambda bb: (bb, 0, 0, 0))
    return pl.pallas_call(
        functools.partial(_ssd_state_kernel, nchunks=t // SSD_CHUNK),
        grid=(b,),
        in_specs=[full(D_INNER), full(LANES), full(BC_DIM), full(BC_DIM), _const_spec((1, LANES))],
        out_specs=[st_spec, st_spec],
        out_shape=[st, st],
        scratch_shapes=[pltpu.VMEM((SSM_GROUPS, D_STATE, GROUP_W), F32)],
        compiler_params=_params("parallel"),
        name="ssd_ctx_states",
    )(xs, dt, bm, cm, a_row)


def _ssd_fwd_kernel(xs_ref, dt_ref, bm_ref, cm_ref, a_ref, h0_ref, y_ref, h_sc, *, cps):
    q = SSD_CHUNK

    @pl.when(pl.program_id(1) == 0)
    def _():
        h_sc[...] = h0_ref[0]

    for ci in range(cps):
        rows = slice(ci * q, (ci + 1) * q)

        def store(pair, val, rows=rows):
            y_ref[0, rows, pair * LANES:(pair + 1) * LANES] = val

        _ssd_chunk(xs_ref[0, rows], dt_ref[0, rows], bm_ref[0, rows], cm_ref[0, rows], a_ref[...],
                   h_sc, 0, False, store)


def _ssd_fwd(xs, dt, bm, cm, a_row, h0, cps):
    b, t, _ = xs.shape
    tm = cps * SSD_CHUNK
    blk = lambda c: pl.BlockSpec((1, tm, c), lambda bb, i: (bb, i, 0))
    return pl.pallas_call(
        functools.partial(_ssd_fwd_kernel, cps=cps),
        grid=(b, t // tm),
        in_specs=[blk(D_INNER), blk(LANES), blk(BC_DIM), blk(BC_DIM), _const_spec((1, LANES)),
                  pl.BlockSpec((1, SSM_GROUPS, D_STATE, GROUP_W), lambda bb, i: (bb, 0, 0, 0))],
        out_specs=blk(D_INNER),
        out_shape=jax.ShapeDtypeStruct((b, t, D_INNER), F32),
        scratch_shapes=[pltpu.VMEM((SSM_GROUPS, D_STATE, GROUP_W), F32)],
        compiler_params=_params("parallel", "arbitrary"),
        name="ssd_fwd",
    )(xs, dt, bm, cm, a_row, h0)


def _ssd_bwd_out_kernel(xs_ref, dt_ref, bm_ref, cm_ref, a_ref, h0_ref, yf_ref, z_ref, x_ref, g_ref,
                        dsk_ref, snw_ref, wo_ref, nw_ref, out_ref, h_sc, yb_sc, *, cps):
    q = SSD_CHUNK

    @pl.when(pl.program_id(1) == 0)
    def _():
        h_sc[...] = h0_ref[0]

    for ci in range(cps - 1, -1, -1):
        rows = slice(ci * q, (ci + 1) * q)

        def store(pair, val, rows=rows):
            yb_sc[rows, pair * LANES:(pair + 1) * LANES] = val

        _ssd_chunk(xs_ref[0, rows], dt_ref[0, rows], bm_ref[0, rows], cm_ref[0, rows], a_ref[...],
                   h_sc, SSM_HEADS, True, store)

    y = yf_ref[0] + yb_sc[...] + dsk_ref[...] * xs_ref[0]
    yn = _rms(y * _silu(z_ref[0]), snw_ref[...])
    o = _dot(yn.astype(BF16), wo_ref[...])
    out_ref[0] = x_ref[0] + g_ref[0] * _rms(o, nw_ref[...])


def _ssd_bwd_out(xs, dt, bm, cm, a_row, h0, yf, z, x, gate, dskip, snw, w_out, nw, cps):
    b, t, d = x.shape
    tm = cps * SSD_CHUNK
    nblk = t // tm
    blk = lambda c: pl.BlockSpec((1, tm, c), lambda bb, i: (bb, nblk - 1 - i, 0))
    return pl.pallas_call(
        functools.partial(_ssd_bwd_out_kernel, cps=cps),
        grid=(b, nblk),
        in_specs=[blk(D_INNER), blk(LANES), blk(BC_DIM), blk(BC_DIM), _const_spec((1, LANES)),
                  pl.BlockSpec((1, SSM_GROUPS, D_STATE, GROUP_W), lambda bb, i: (bb, 0, 0, 0)),
                  blk(D_INNER), blk(D_INNER), blk(d),
                  pl.BlockSpec((1, 1, d), lambda bb, i: (bb, 0, 0)),
                  _const_spec((1, D_INNER)), _const_spec((1, D_INNER)), _const_spec((D_INNER, d)),
                  _const_spec((1, d))],
        out_specs=blk(d),
        out_shape=jax.ShapeDtypeStruct((b, t, d), F32),
        scratch_shapes=[pltpu.VMEM((SSM_GROUPS, D_STATE, GROUP_W), F32), pltpu.VMEM((tm, D_INNER), F32)],
        compiler_params=_params("parallel", "arbitrary"),
        name="ssd_bwd_out",
    )(xs, dt, bm, cm, a_row, h0, yf, z, x, gate, dskip, snw, w_out, nw)


def _rope_tables(t):
    rows = t // GRID_W
    row = jnp.repeat(jnp.arange(rows, dtype=F32), GRID_W)
    col = jnp.tile(jnp.arange(GRID_W, dtype=F32), rows)
    half = HEAD_DIM // 2
    inv_freq = ROPE_THETA ** (-jnp.arange(0, half, 2, dtype=F32) / half)
    ang = jnp.concatenate([row[:, None] * inv_freq, col[:, None] * inv_freq], axis=-1)
    cos, sin = jnp.cos(ang), jnp.sin(ang)
    return jnp.concatenate([cos, cos], axis=-1), jnp.concatenate([-sin, sin], axis=-1)


def _tile(t, pref):
    return pref if t % pref == 0 else t


def kernel(x, c, ctx, c_ctx, ada_w, ada_b, norm_w, attn_w_in, pool_w, pool_scale, q_gain, k_gain, attn_w_out,
           ssm_w_in, ssm_conv_w, ssm_conv_b, ssm_A_log, ssm_dt_bias, ssm_D, ssm_norm_w, ssm_w_out,
           ffn_w_up, ffn_conv_w, ffn_conv_b, ffn_w_down):
    bsz, t, d = x.shape
    lc = ctx.shape[1]
    depth = ada_w.shape[0]
    assert depth == 2 and attn_w_in.shape[0] == 1 and ssm_w_in.shape[0] == 1
    d_ff = ffn_w_down.shape[1]

    rows = -(-(bsz + 1) // HALO) * HALO
    cond = jnp.zeros((rows, d), F32).at[:bsz].set(c).at[bsz].set(c_ctx)
    ada = _adaln(cond, ada_w, ada_b)
    mod_l = [[ada[i, :bsz, k * d:(k + 1) * d].reshape(bsz, 1, d) for k in range(6)] for i in range(depth)]
    mod_c = [[ada[i, bsz:bsz + 1, k * d:(k + 1) * d].reshape(1, 1, d) for k in range(6)] for i in range(depth)]
    nw = lambda i, k: norm_w[i, k].reshape(1, d)

    fc = 256
    nch = d_ff // fc

    def ffn_weights(i):
        wu = ffn_w_up[i].astype(BF16)
        wup = jnp.concatenate([wu[:, :d_ff].reshape(d, nch, fc), wu[:, d_ff:].reshape(d, nch, fc)], axis=-1)
        wup = wup.transpose(1, 0, 2)
        cwt = ffn_conv_w[i].T
        cw = jnp.concatenate([cwt[:, :d_ff].reshape(3, nch, fc), cwt[:, d_ff:].reshape(3, nch, fc)], axis=-1)
        cw = cw.transpose(1, 0, 2)
        cb = jnp.concatenate([ffn_conv_b[i][:d_ff].reshape(nch, 1, fc), ffn_conv_b[i][d_ff:].reshape(nch, 1, fc)],
                             axis=-1)
        wdn = ffn_w_down[i].astype(BF16).reshape(nch, fc, d)
        return wup, cw, cb, wdn

    def ffn(xx, mods, i, tm):
        wup, cw, cb, wdn = ffn_weights(i)
        return _ffn(xx, mods[3], mods[4], mods[5], nw(i, 2), nw(i, 3), wup, cw, cb, wdn, tm)

    perm = jnp.concatenate([jnp.arange(0, HEAD_DIM, 2), jnp.arange(1, HEAD_DIM, 2)])
    cols = jnp.arange(attn_w_in.shape[2])
    qk_lo, qk_hi = POOL_DIM, POOL_DIM + ATTN_DIM + KV_DIM
    qk_cols = (qk_lo + (jnp.arange(qk_hi - qk_lo) // HEAD_DIM) * HEAD_DIM
               + perm[jnp.arange(qk_hi - qk_lo) % HEAD_DIM])
    cols = cols.at[qk_lo:qk_hi].set(qk_cols)
    w_in_a = attn_w_in[0][:, cols].astype(BF16)
    qg = q_gain[0][perm].reshape(1, HEAD_DIM)
    kg = k_gain[0][perm].reshape(1, HEAD_DIM)
    cos2, sin2 = _rope_tables(t)
    eye = jnp.eye(len(POOL_WINDOWS), dtype=F32)
    pool_bd = (eye[:, None, :, None] * pool_w[0][:, :, None, :]).reshape(POOL_DIM, POOL_DIM).astype(BF16)
    ps = pool_scale[0].reshape(1, POOL_DIM)
    w_out_a = attn_w_out[0][:POOL_DIM].astype(BF16)
    w_out_o = attn_w_out[0][POOL_DIM:].astype(BF16)

    tm_l = _tile(t, 512)
    tm_c = _tile(lc, 512)
    a_c, q_c, k_c, v_c = _attn_in(ctx, mod_c[0][0], mod_c[0][1], nw(0, 0), w_in_a, qg, kg, None, None, tm_c)
    a_l, q_l, k_l, v_l = _attn_in(x, mod_l[0][0], mod_l[0][1], nw(0, 0), w_in_a, qg, kg, cos2, sin2, tm_l)
    k_all = jnp.concatenate([k_c, k_l], axis=1)
    v_all = jnp.concatenate([v_c, v_l], axis=1)
    o_l = _attention(q_l, k_all, v_all, _tile(t, 256), 512)
    o_c = _attention(q_c, k_c, v_c, _tile(lc, 256), 512)
    x = _attn_out(x, mod_l[0][2], nw(0, 1), a_l, o_l, pool_bd, ps, w_out_a, w_out_o, tm_l)
    ctx = _attn_out(ctx, mod_c[0][2], nw(0, 1), a_c, o_c, pool_bd, ps, w_out_a, w_out_o, tm_c)
    x = ffn(x, mod_l[0], 0, tm_l)
    ctx = ffn(ctx, mod_c[0], 0, tm_c)

    w_in_c = ssm_w_in[0].astype(BF16)
    conv_dim = D_INNER + 2 * BC_DIM
    wz = w_in_c[:, :D_INNER]
    wx = w_in_c[:, D_INNER:D_INNER + conv_dim]
    ndt = 2 * SSM_HEADS
    wdt = jnp.zeros((d, LANES), BF16).at[:, :ndt].set(w_in_c[:, D_INNER + conv_dim:])
    dtb = jnp.zeros((1, LANES), F32).at[0, :ndt].set(ssm_dt_bias[0].reshape(ndt))
    a_row = jnp.zeros((1, LANES), F32).at[0, :ndt].set(-jnp.exp(ssm_A_log[0].astype(F32)).reshape(ndt))
    scw = ssm_conv_w[0].T
    scb = ssm_conv_b[0].reshape(1, conv_dim)
    dskip = jnp.repeat(ssm_D[0], SSM_HEAD_DIM).reshape(1, D_INNER)
    snw = ssm_norm_w[0].reshape(1, D_INNER)
    w_out_c = ssm_w_out[0].astype(BF16)

    tm_s = _tile(t, 256)
    tm_sc = _tile(lc, 256)
    _, xs_c, bm_c, cm_c, dt_c = _ssm_in(ctx, mod_c[1][0], mod_c[1][1], nw(1, 0), wz, wx, wdt, scw, scb, dtb, tm_sc)
    hf0, hb0 = _ssd_states(xs_c, dt_c, bm_c, cm_c, a_row)
    z_l, xs_l, bm_l, cm_l, dt_l = _ssm_in(x, mod_l[1][0], mod_l[1][1], nw(1, 0), wz, wx, wdt, scw, scb, dtb, tm_s)
    cps = 2 if t % (2 * SSD_CHUNK) == 0 else 1
    yf = _ssd_fwd(xs_l, dt_l, bm_l, cm_l, a_row, hf0, cps)
    x = _ssd_bwd_out(xs_l, dt_l, bm_l, cm_l, a_row, hb0, yf, z_l, x, mod_l[1][2], dskip, snw, w_out_c,
                     nw(1, 1), cps)
    x = ffn(x, mod_l[1], 1, tm_l)
    return x
```

```python
import functools
import math

import jax
import jax.numpy as jnp
from jax import lax
from jax.experimental import pallas as pl
from jax.experimental.pallas import tpu as pltpu

F32 = jnp.float32
BF16 = jnp.bfloat16

EPS = 1e-6
GRID_W = 64
ROPE_THETA = 10000.0

HEAD_DIM = 128
N_Q_HEADS = 6
N_KV_HEADS = 2
Q_PER_KV = N_Q_HEADS // N_KV_HEADS
POOL_WINDOWS = (2, 4, 8, 16)
POOL_GROUP_DIM = 64
POOL_DIM = 256
ATTN_DIM = N_Q_HEADS * HEAD_DIM
KV_DIM = N_KV_HEADS * HEAD_DIM

SSM_HEAD_DIM = 64
SSM_HEADS = 32
SSM_GROUPS = 4
HEADS_PER_GROUP = 8
D_STATE = 128
D_INNER = SSM_HEADS * SSM_HEAD_DIM
GROUP_W = HEADS_PER_GROUP * SSM_HEAD_DIM
BC_DIM = SSM_GROUPS * D_STATE
SSD_CHUNK = 128

HALO = 8
LANES = 128
VMEM_LIMIT = 56 * 1024 * 1024


def _params(*sem):
    return pltpu.CompilerParams(dimension_semantics=sem, vmem_limit_bytes=VMEM_LIMIT)


def _const_spec(shape):
    nd = len(shape)
    return pl.BlockSpec(shape, lambda *_: (0,) * nd, pipeline_mode=pl.Buffered(1))


def _row_spec(c, shared):
    if shared:
        return pl.BlockSpec((1, 1, c), lambda b, i: (0, 0, 0))
    return pl.BlockSpec((1, 1, c), lambda b, i: (b, 0, 0))


def _halo_specs(tm, c, t):
    nb = tm // HALO
    last = t // HALO - 1
    main = pl.BlockSpec((1, tm, c), lambda b, i: (b, i, 0))
    prev = pl.BlockSpec((1, HALO, c), lambda b, i: (b, jnp.maximum(i * nb - 1, 0), 0))
    nxt = pl.BlockSpec((1, HALO, c), lambda b, i: (b, jnp.minimum((i + 1) * nb, last), 0))
    return main, prev, nxt


def _rms(xf, w):
    ms = jnp.mean(xf * xf, axis=-1, keepdims=True)
    return xf * lax.rsqrt(ms + EPS) * w


def _silu(x):
    return x * jax.nn.sigmoid(x)


def _dot(a, b):
    return jnp.dot(a, b, preferred_element_type=F32)


def _ada_kernel(c_ref, w_ref, b_ref, o_ref):
    s = _silu(c_ref[...])
    o_ref[0] = _dot(s.astype(BF16), w_ref[0].astype(BF16)) + b_ref[0]


def _adaln(cond, ada_w, ada_b):
    depth, d, n = ada_w.shape
    rows = cond.shape[0]
    tn = 1536
    return pl.pallas_call(
        _ada_kernel,
        grid=(depth, n // tn),
        in_specs=[
            pl.BlockSpec((rows, d), lambda l, j: (0, 0)),
            pl.BlockSpec((1, d, tn), lambda l, j: (l, 0, j)),
            pl.BlockSpec((1, 1, tn), lambda l, j: (l, 0, j)),
        ],
        out_specs=pl.BlockSpec((1, rows, tn), lambda l, j: (l, 0, j)),
        out_shape=jax.ShapeDtypeStruct((depth, rows, n), F32),
        compiler_params=_params("parallel", "parallel"),
        name="adaln",
    )(cond, ada_w, ada_b.reshape(depth, 1, n))


def _attn_in_kernel(*refs, rope):
    if rope:
        (x_ref, sh_ref, sc_ref, nw_ref, w_ref, qg_ref, kg_ref, cos_ref, sin_ref,
         a_ref, q_ref, k_ref, v_ref) = refs
        cos2, sin2 = cos_ref[...], sin_ref[...]
    else:
        (x_ref, sh_ref, sc_ref, nw_ref, w_ref, qg_ref, kg_ref,
         a_ref, q_ref, k_ref, v_ref) = refs
    h = _rms(x_ref[0], nw_ref[...]) * (1.0 + sc_ref[0]) + sh_ref[0]
    u = _dot(h.astype(BF16), w_ref[...])
    a_ref[0] = u[:, :POOL_DIM]

    def norm_rope(t, gain):
        y = _rms(t, gain)
        if rope:
            y = y * cos2 + pltpu.roll(y, HEAD_DIM // 2, 1) * sin2
        return y.astype(BF16)

    off = POOL_DIM
    for hh in range(N_Q_HEADS):
        q_ref[0, :, hh * HEAD_DIM:(hh + 1) * HEAD_DIM] = norm_rope(
            u[:, off + hh * HEAD_DIM: off + (hh + 1) * HEAD_DIM], qg_ref[...])
    off += ATTN_DIM
    for hh in range(N_KV_HEADS):
        k_ref[0, :, hh * HEAD_DIM:(hh + 1) * HEAD_DIM] = norm_rope(
            u[:, off + hh * HEAD_DIM: off + (hh + 1) * HEAD_DIM], kg_ref[...])
    off += KV_DIM
    v_ref[0] = u[:, off:].astype(BF16)


def _attn_in(x, shift, scale, nw, w_in, qg, kg, cos2, sin2, tm):
    b, t, d = x.shape
    shared = shift.shape[0] == 1
    rope = cos2 is not None
    width = w_in.shape[1]
    in_specs = [
        pl.BlockSpec((1, tm, d), lambda bb, i: (bb, i, 0)),
        _row_spec(d, shared), _row_spec(d, shared),
        _const_spec((1, d)), _const_spec((d, width)),
        _const_spec((1, HEAD_DIM)), _const_spec((1, HEAD_DIM)),
    ]
    args = [x, shift, scale, nw, w_in, qg, kg]
    if rope:
        in_specs += [pl.BlockSpec((tm, HEAD_DIM), lambda bb, i: (i, 0))] * 2
        args += [cos2, sin2]
    outs = [(POOL_DIM, F32), (ATTN_DIM, BF16), (KV_DIM, BF16), (KV_DIM, BF16)]
    return pl.pallas_call(
        functools.partial(_attn_in_kernel, rope=rope),
        grid=(b, t // tm),
        in_specs=in_specs,
        out_specs=[pl.BlockSpec((1, tm, c), lambda bb, i: (bb, i, 0)) for c, _ in outs],
        out_shape=[jax.ShapeDtypeStruct((b, t, c), dt) for c, dt in outs],
        compiler_params=_params("parallel", "parallel"),
        name="attn_in",
    )(*args)


SUM_ROWS = 16


def _attn_kernel(q_ref, k_ref, vt_ref, o_ref, m_sc, acc_sc, s0_sc, mc0_sc, s1_sc, mc1_sc, *, n, tq):
    slots = ((s0_sc, mc0_sc), (s1_sc, mc1_sc))
    c = (HEAD_DIM ** -0.5) * math.log2(math.e)

    q = q_ref[0]
    qs = jnp.concatenate([q[:, g * HEAD_DIM:(g + 1) * HEAD_DIM] for g in range(Q_PER_KV)], axis=0)
    m_sc[...] = jnp.full(m_sc.shape, -jnp.inf, F32)
    acc_sc[...] = jnp.zeros(acc_sc.shape, F32)

    def produce(j, slot, g):
        s_sc, mc_sc = slots[slot]
        cols = slice(g * tq, (g + 1) * tq)
        s = lax.dot_general(k_ref[0, 0, j], qs[cols], (((1,), (1,)), ((), ())),
                            preferred_element_type=F32)
        s_sc[:, cols] = s
        mc_sc[:, cols] = jnp.max(s, axis=0, keepdims=True)

    def consume(j, slot, g):
        s_sc, mc_sc = slots[slot]
        cols = slice(g * tq, (g + 1) * tq)
        m_prev = m_sc[:, cols]
        m_new = jnp.maximum(m_prev, mc_sc[:, cols])
        alpha = jnp.exp2((m_prev - m_new) * c)
        p = jnp.exp2((s_sc[:, cols] - m_new) * c).astype(BF16)
        acc_sc[:, cols] = alpha * acc_sc[:, cols] + _dot(vt_ref[0, 0, j], p)
        m_sc[:, cols] = m_new

    def step(jp, sp, jc, sc):
        for g in range(Q_PER_KV):
            if jp is not None:
                produce(jp, sp, g)
            if jc is not None:
                consume(jc, sc, g)

    step(0, 0, None, None)

    def body(jj, carry):
        step(2 * jj + 1, 1, 2 * jj, 0)
        step(2 * jj + 2, 0, 2 * jj + 1, 1)
        return carry

    npairs = (n - 1) // 2
    lax.fori_loop(0, npairs, body, 0)
    if n % 2 == 0:
        step(n - 1, 1, n - 2, 0)
    step(None, None, n - 1, (n - 1) % 2)

    acc = acc_sc[...]
    o_t = acc[:HEAD_DIM] / acc[HEAD_DIM:HEAD_DIM + 1]
    for g in range(Q_PER_KV):
        o_ref[0, :, g * HEAD_DIM:(g + 1) * HEAD_DIM] = o_t[:, g * tq:(g + 1) * tq].T.astype(BF16)


def _attention(q, k, v, tq, tk):
    b, t, _ = q.shape
    l = k.shape[1]
    gw = Q_PER_KV * HEAD_DIM
    tk = min(tk, l)
    pad = (-l) % tk
    assert pad <= l
    n = (l + pad) // tk
    k = jnp.concatenate([k, k[:, :pad]], axis=1)
    v = jnp.concatenate([v, jnp.zeros((b, pad, KV_DIM), BF16)], axis=1)
    live = (jnp.arange(l + pad) < l).astype(BF16).reshape(n, 1, tk)
    kc = k.reshape(b, n, tk, N_KV_HEADS, HEAD_DIM).transpose(0, 3, 1, 2, 4)
    vt = v.reshape(b, n, tk, N_KV_HEADS, HEAD_DIM).transpose(0, 3, 1, 4, 2)
    vt = jnp.concatenate([vt, jnp.broadcast_to(live, (b, N_KV_HEADS, n, SUM_ROWS, tk))], axis=3)
    m = Q_PER_KV * tq
    return pl.pallas_call(
        functools.partial(_attn_kernel, n=n, tq=tq),
        grid=(b, N_KV_HEADS, t // tq),
        in_specs=[
            pl.BlockSpec((1, tq, gw), lambda bb, g, i: (bb, i, g)),
            pl.BlockSpec((1, 1) + kc.shape[2:], lambda bb, g, i: (bb, g, 0, 0, 0)),
            pl.BlockSpec((1, 1) + vt.shape[2:], lambda bb, g, i: (bb, g, 0, 0, 0)),
        ],
        out_specs=pl.BlockSpec((1, tq, gw), lambda bb, g, i: (bb, i, g)),
        out_shape=jax.ShapeDtypeStruct((b, t, ATTN_DIM), BF16),
        scratch_shapes=[pltpu.VMEM((1, m), F32), pltpu.VMEM((HEAD_DIM + SUM_ROWS, m), F32)]
        + [pltpu.VMEM((tk, m), F32), pltpu.VMEM((1, m), F32)] * 2,
        compiler_params=_params("parallel", "parallel", "arbitrary"),
        name="attention",
    )(q, kc, vt)


def _attn_out_kernel(x_ref, g_ref, nw_ref, a_ref, ap_ref, an_ref, o_ref, pw_ref, ps_ref, wa_ref, wo_ref,
                     out_ref, abuf, *, tm, t):
    i = pl.program_id(1)
    last = pl.num_programs(1) - 1
    abuf[0:HALO] = jnp.where(i == 0, 0.0, ap_ref[0])
    abuf[HALO:HALO + tm] = a_ref[0]
    abuf[HALO + tm:] = jnp.where(i == last, 0.0, an_ref[0])

    def sh(d):
        return abuf[HALO + d:HALO + d + tm, :]

    x0 = sh(0)
    s2 = sh(-1) + x0
    s4 = s2 + sh(-2) + sh(1)
    s8 = s4 + sh(-4) + sh(-3) + sh(2) + sh(3)
    s16 = s8 + sh(-8) + sh(-7) + sh(-6) + sh(-5) + sh(4) + sh(5) + sh(6) + sh(7)
    pos = i * tm + lax.broadcasted_iota(jnp.int32, (tm, 1), 0)
    lane = lax.broadcasted_iota(jnp.int32, (1, POOL_DIM), 1)

    def mean(s, w):
        left = w // 2
        right = w - 1 - left
        cnt = jnp.minimum(pos + right, t - 1) - jnp.maximum(pos - left, 0) + 1
        return s / cnt.astype(F32)

    pm = jnp.where(lane < 64, mean(s2, 2),
                   jnp.where(lane < 128, mean(s4, 4),
                             jnp.where(lane < 192, mean(s8, 8), mean(s16, 16))))
    p = pm - x0
    py = _dot(p.astype(BF16), pw_ref[...]) * ps_ref[...]
    y = _dot(py.astype(BF16), wa_ref[...]) + _dot(o_ref[0], wo_ref[...])
    out_ref[0] = x_ref[0] + g_ref[0] * _rms(y, nw_ref[...])


def _attn_out(x, gate, nw, a, o, pool_bd, pool_scale, w_a, w_o, tm):
    b, t, d = x.shape
    shared = gate.shape[0] == 1
    am, ap, an = _halo_specs(tm, POOL_DIM, t)
    return pl.pallas_call(
        functools.partial(_attn_out_kernel, tm=tm, t=t),
        grid=(b, t // tm),
        in_specs=[
            pl.BlockSpec((1, tm, d), lambda bb, i: (bb, i, 0)),
            _row_spec(d, shared), _const_spec((1, d)),
            am, ap, an,
            pl.BlockSpec((1, tm, ATTN_DIM), lambda bb, i: (bb, i, 0)),
            _const_spec((POOL_DIM, POOL_DIM)), _const_spec((1, POOL_DIM)),
            _const_spec((POOL_DIM, d)), _const_spec((ATTN_DIM, d)),
        ],
        out_specs=pl.BlockSpec((1, tm, d), lambda bb, i: (bb, i, 0)),
        out_shape=jax.ShapeDtypeStruct((b, t, d), F32),
        scratch_shapes=[pltpu.VMEM((tm + 2 * HALO, POOL_DIM), F32)],
        compiler_params=_params("parallel", "parallel"),
        name="attn_out",
    )(x, gate, nw, a, a, a, o, pool_bd, pool_scale, w_a, w_o)


def _ffn_kernel(x_ref, xp_ref, xn_ref, sh_ref, sc_ref, g_ref, nw2_ref, nw3_ref,
                wup_ref, cw_ref, cb_ref, wdn_ref, out_ref, xe_sc, h_sc, u0_sc, u1_sc, acc_sc, *, tm, fc, nch):
    i = pl.program_id(1)
    last = pl.num_programs(1) - 1
    xe_sc[0:HALO] = xp_ref[0]
    xe_sc[HALO:HALO + tm] = x_ref[0]
    xe_sc[HALO + tm:] = xn_ref[0]
    h = _rms(xe_sc[...], nw2_ref[...]) * (1.0 + sc_ref[0]) + sh_ref[0]
    h_sc[...] = h.astype(BF16)
    acc_sc[...] = jnp.zeros(acc_sc.shape, F32)
    keep_top = jnp.where(i == 0, 0.0, 1.0)
    keep_bot = jnp.where(i == last, 0.0, 1.0)
    slots = (u0_sc, u1_sc)

    def produce(j, slot):
        u = _dot(h_sc[...], wup_ref[j])
        u_sc = slots[slot]
        u_sc[0:HALO] = u[0:HALO] * keep_top
        u_sc[HALO:HALO + tm] = u[HALO:HALO + tm]
        u_sc[HALO + tm:] = u[HALO + tm:] * keep_bot

    def consume(j, slot):
        u_sc = slots[slot]
        cw = cw_ref[j]
        cv = cb_ref[j]
        for k in range(3):
            cv = cv + u_sc[HALO - 1 + k:HALO - 1 + k + tm, :] * cw[k:k + 1]
        g = _silu(cv[:, fc:]) * cv[:, :fc]
        acc_sc[...] += _dot(g.astype(BF16), wdn_ref[j])

    produce(0, 0)

    def body(jj, carry):
        produce(2 * jj + 1, 1)
        consume(2 * jj, 0)
        produce(2 * jj + 2, 0)
        consume(2 * jj + 1, 1)
        return carry

    lax.fori_loop(0, (nch - 1) // 2, body, 0)
    if nch % 2 == 0:
        produce(nch - 1, 1)
        consume(nch - 2, 0)
    consume(nch - 1, (nch - 1) % 2)
    out_ref[0] = x_ref[0] + g_ref[0] * _rms(acc_sc[...], nw3_ref[...])


def _ffn(x, shift, scale, gate, nw2, nw3, wup, cw, cb, wdn, tm):
    b, t, d = x.shape
    shared = shift.shape[0] == 1
    nch, _, fc2 = wup.shape
    fc = fc2 // 2
    xm, xp, xn = _halo_specs(tm, d, t)
    return pl.pallas_call(
        functools.partial(_ffn_kernel, tm=tm, fc=fc, nch=nch),
        grid=(b, t // tm),
        in_specs=[
            xm, xp, xn,
            _row_spec(d, shared), _row_spec(d, shared), _row_spec(d, shared),
            _const_spec((1, d)), _const_spec((1, d)),
            _const_spec(wup.shape), _const_spec(cw.shape), _const_spec(cb.shape), _const_spec(wdn.shape),
        ],
        out_specs=pl.BlockSpec((1, tm, d), lambda bb, i: (bb, i, 0)),
        out_shape=jax.ShapeDtypeStruct((b, t, d), F32),
        scratch_shapes=[
            pltpu.VMEM((tm + 2 * HALO, d), F32),
            pltpu.VMEM((tm + 2 * HALO, d), BF16),
            pltpu.VMEM((tm + 2 * HALO, fc2), F32),
            pltpu.VMEM((tm + 2 * HALO, fc2), F32),
            pltpu.VMEM((tm, d), F32),
        ],
        compiler_params=_params("parallel", "parallel"),
        name="conv_ffn",
    )(x, x, x, shift, scale, gate, nw2, nw3, wup, cw, cb, wdn)


def _ssm_in_kernel(x_ref, xp_ref, xn_ref, sh_ref, sc_ref, nw_ref, wz_ref, wx_ref, wdt_ref,
                   cw_ref, cb_ref, dtb_ref, z_ref, xs_ref, bm_ref, cm_ref, dt_ref, xe_sc, u0_sc, u1_sc, *, tm, cc):
    u_scs = (u0_sc, u1_sc)
    i = pl.program_id(1)
    last = pl.num_programs(1) - 1
    xe_sc[0:HALO] = xp_ref[0]
    xe_sc[HALO:HALO + tm] = x_ref[0]
    xe_sc[HALO + tm:] = xn_ref[0]
    hf = _rms(xe_sc[...], nw_ref[...]) * (1.0 + sc_ref[0]) + sh_ref[0]
    he = hf.astype(BF16)
    hm = hf[HALO:HALO + tm].astype(BF16)
    keep_top = jnp.where(i == 0, 0.0, 1.0)
    keep_bot = jnp.where(i == last, 0.0, 1.0)

    for j in range(D_INNER // cc):
        z_ref[0, :, j * cc:(j + 1) * cc] = _dot(hm, wz_ref[:, j * cc:(j + 1) * cc])

    dt_raw = _dot(hm, wdt_ref[...]) + dtb_ref[...]
    dt_ref[0] = jnp.maximum(dt_raw, 0.0) + jnp.log1p(jnp.exp(-jnp.abs(dt_raw)))

    conv_dim = D_INNER + 2 * BC_DIM
    for j in range(conv_dim // cc):
        lo = j * cc
        u_sc = u_scs[j % 2]
        u = _dot(he, wx_ref[:, lo:lo + cc])
        u_sc[0:HALO] = u[0:HALO] * keep_top
        u_sc[HALO:HALO + tm] = u[HALO:HALO + tm]
        u_sc[HALO + tm:] = u[HALO + tm:] * keep_bot

        cv = cb_ref[:, lo:lo + cc]
        for k in range(4):
            cv = cv + u_sc[HALO - 2 + k:HALO - 2 + k + tm, :] * cw_ref[k:k + 1, lo:lo + cc]
        act = _silu(cv)
        if lo < D_INNER:
            xs_ref[0, :, lo:lo + cc] = act
        elif lo < D_INNER + BC_DIM:
            bm_ref[0, :, lo - D_INNER:lo - D_INNER + cc] = act.astype(BF16)
        else:
            cm_ref[0, :, lo - D_INNER - BC_DIM:lo - D_INNER - BC_DIM + cc] = act.astype(BF16)


def _ssm_in(x, shift, scale, nw, wz, wx, wdt, cw, cb, dtb, tm):
    b, t, d = x.shape
    shared = shift.shape[0] == 1
    cc = 512
    xm, xp, xn = _halo_specs(tm, d, t)
    outs = [(D_INNER, F32), (D_INNER, F32), (BC_DIM, BF16), (BC_DIM, BF16), (2 * LANES, F32)]
    return pl.pallas_call(
        functools.partial(_ssm_in_kernel, tm=tm, cc=cc),
        grid=(b, t // tm),
        in_specs=[
            xm, xp, xn, _row_spec(d, shared), _row_spec(d, shared), _const_spec((1, d)),
            _const_spec(wz.shape), _const_spec(wx.shape), _const_spec(wdt.shape),
            _const_spec(cw.shape), _const_spec(cb.shape), _const_spec(dtb.shape),
        ],
        out_specs=[pl.BlockSpec((1, tm, c), lambda bb, i: (bb, i, 0)) for c, _ in outs],
        out_shape=[jax.ShapeDtypeStruct((b, t, c), dt) for c, dt in outs],
        scratch_shapes=[pltpu.VMEM((tm + 2 * HALO, d), F32)] + [pltpu.VMEM((tm + 2 * HALO, cc), F32)] * 2,
        compiler_params=_params("parallel", "parallel"),
        name="ssm_in",
    )(x, x, x, shift, scale, nw, wz, wx, wdt, cw, cb, dtb)


DT_REP = 3


def _split3(v, lane):
    p1 = v.astype(BF16).astype(F32)
    r1 = v - p1
    p2 = r1.astype(BF16).astype(F32)
    r2 = r1 - p2
    packed = jnp.where(lane < 32, p1, jnp.where(lane < 64, p2, jnp.where(lane < 96, r2, 0.0)))
    return packed.astype(BF16)


def _sel_matrix(width):
    r = jnp.arange(LANES)[:, None]
    c = jnp.arange(SSM_HEADS * width)[None, :]
    return ((c // width == r % SSM_HEADS) & (r < DT_REP * SSM_HEADS)).astype(BF16)


def _ssd_chunk(xs_ref, dt_ref, bm_ref, cm_ref, rows, a_row, selx_ref, sell_ref, h_sc, reverse, y_store):
    q = SSD_CHUNK
    row = lax.broadcasted_iota(jnp.int32, (q, q), 0)
    col = lax.broadcasted_iota(jnp.int32, (q, q), 1)
    mask = (col >= row) if reverse else (col <= row)
    tri = jnp.where(mask, 1.0, 0.0).astype(BF16)
    lane = lax.broadcasted_iota(jnp.int32, (1, LANES), 1)
    lane_lo = lane < SSM_HEAD_DIM
    end = 0 if reverse else q - 1

    dt = dt_ref[0, rows]
    a = dt * a_row
    a1 = a.astype(BF16)
    r1 = a - a1.astype(F32)
    a2 = r1.astype(BF16)
    a3 = (r1 - a2.astype(F32)).astype(BF16)
    acum = _dot(tri, a1) + _dot(tri, a2) + _dot(tri, a3)
    dt_p = _split3(dt, lane)
    acum_p = _split3(acum, lane)
    if y_store is not None:
        acum_t = acum.T

    for g in range(SSM_GROUPS):
        gs = slice(g * D_STATE, (g + 1) * D_STATE)
        b_g = bm_ref[0, rows, gs]
        c_g = cm_ref[0, rows, gs]
        h_g = h_sc[g]
        dt_x = _dot(dt_p, selx_ref[:, g * GROUP_W:(g + 1) * GROUP_W])
        if y_store is not None:
            cb = lax.dot_general(c_g, b_g, (((1,), (1,)), ((), ())), preferred_element_type=F32)
            y_inter = _dot(c_g, h_g.astype(BF16))
        xd_parts = []
        etot_parts = []
        for j in range(HEADS_PER_GROUP // 2):
            hd = g * HEADS_PER_GROUP + 2 * j
            pair = hd // 2
            pl_ = slice(j * LANES, (j + 1) * LANES)
            d2 = _dot(acum_p, sell_ref[:, pair * 2 * q:(pair + 1) * 2 * q])
            dmix = jnp.where(lane_lo, d2[:, :q], d2[:, q:])
            tot = dmix[end:end + 1]
            xdt = xs_ref[0, rows, pair * LANES:(pair + 1) * LANES] * dt_x[:, pl_]
            xd_parts.append((xdt * jnp.exp(tot - dmix)).astype(BF16))
            etot_parts.append(jnp.exp(tot))
            if y_store is not None:
                lm = []
                for k in range(2):
                    seg = d2[:, k * q:(k + 1) * q] - acum_t[hd + k:hd + k + 1, :]
                    lm.append(cb * jnp.exp(jnp.where(mask, seg, -jnp.inf)))
                m2 = jnp.concatenate(lm, axis=1).astype(BF16)
                xb = xdt.astype(BF16)
                zero = jnp.zeros_like(xb)
                rhs = jnp.concatenate([jnp.where(lane_lo, xb, zero), jnp.where(lane_lo, zero, xb)], axis=0)
                y_intra = _dot(m2, rhs)
                y_store(pair, y_intra + y_inter[:, pl_] * jnp.exp(dmix))
        xd = jnp.concatenate(xd_parts, axis=1)
        upd = lax.dot_general(b_g, xd, (((0,), (0,)), ((), ())), preferred_element_type=F32)
        h_sc[g] = jnp.concatenate(etot_parts, axis=1) * h_g + upd


def _ssd_state_kernel(xs_ref, dtf_ref, dtb_ref, bm_ref, cm_ref, af_ref, ab_ref, selx_ref, sell_ref,
                      hf_ref, hb_ref, h_sc, *, nchunks):
    q = SSD_CHUNK
    for reverse, dt_ref, a_ref, out_ref in ((False, dtf_ref, af_ref, hf_ref), (True, dtb_ref, ab_ref, hb_ref)):
        h_sc[...] = jnp.zeros(h_sc.shape, F32)
        order = range(nchunks - 1, -1, -1) if reverse else range(nchunks)
        for ci in order:
            rows = slice(ci * q, (ci + 1) * q)
            _ssd_chunk(xs_ref, dt_ref, bm_ref, cm_ref, rows, a_ref[...], selx_ref, sell_ref, h_sc, reverse, None)
        out_ref[0] = h_sc[...]


def _dir_spec(rev):
    return pl.BlockSpec((1, LANES), lambda *_: (0, rev))


def _ssd_states(xs, dt, bm, cm, a_row, selx, sell):
    b, t, _ = xs.shape
    st = jax.ShapeDtypeStruct((b, SSM_GROUPS, D_STATE, GROUP_W), F32)
    full = lambda c, j=0: pl.BlockSpec((1, t, c), lambda bb: (bb, 0, j))
    st_spec = pl.BlockSpec((1, SSM_GROUPS, D_STATE, GROUP_W), lambda bb: (bb, 0, 0, 0))
    return pl.pallas_call(
        functools.partial(_ssd_state_kernel, nchunks=t // SSD_CHUNK),
        grid=(b,),
        in_specs=[full(D_INNER), full(LANES, 0), full(LANES, 1), full(BC_DIM), full(BC_DIM),
                  _dir_spec(0), _dir_spec(1), _const_spec(selx.shape), _const_spec(sell.shape)],
        out_specs=[st_spec, st_spec],
        out_shape=[st, st],
        scratch_shapes=[pltpu.VMEM((SSM_GROUPS, D_STATE, GROUP_W), F32)],
        compiler_params=_params("parallel"),
        name="ssd_ctx_states",
    )(xs, dt, dt, bm, cm, a_row, a_row, selx, sell)


def _ssd_fwd_kernel(xs_ref, dt_ref, bm_ref, cm_ref, a_ref, selx_ref, sell_ref, h0_ref, y_ref, h_sc, *, cps):
    q = SSD_CHUNK

    @pl.when(pl.program_id(1) == 0)
    def _():
        h_sc[...] = h0_ref[0]

    for ci in range(cps):
        rows = slice(ci * q, (ci + 1) * q)

        def store(pair, val, rows=rows):
            y_ref[0, rows, pair * LANES:(pair + 1) * LANES] = val

        _ssd_chunk(xs_ref, dt_ref, bm_ref, cm_ref, rows, a_ref[...], selx_ref, sell_ref, h_sc, False, store)


def _ssd_fwd(xs, dt, bm, cm, a_row, selx, sell, h0, cps):
    b, t, _ = xs.shape
    tm = cps * SSD_CHUNK
    blk = lambda c: pl.BlockSpec((1, tm, c), lambda bb, i: (bb, i, 0))
    return pl.pallas_call(
        functools.partial(_ssd_fwd_kernel, cps=cps),
        grid=(b, t // tm),
        in_specs=[blk(D_INNER), blk(LANES), blk(BC_DIM), blk(BC_DIM), _dir_spec(0),
                  _const_spec(selx.shape), _const_spec(sell.shape),
                  pl.BlockSpec((1, SSM_GROUPS, D_STATE, GROUP_W), lambda bb, i: (bb, 0, 0, 0))],
        out_specs=blk(D_INNER),
        out_shape=jax.ShapeDtypeStruct((b, t, D_INNER), F32),
        scratch_shapes=[pltpu.VMEM((SSM_GROUPS, D_STATE, GROUP_W), F32)],
        compiler_params=_params("parallel", "arbitrary"),
        name="ssd_fwd",
    )(xs, dt, bm, cm, a_row, selx, sell, h0)


def _ssd_bwd_out_kernel(xs_ref, dt_ref, bm_ref, cm_ref, a_ref, selx_ref, sell_ref, h0_ref, yf_ref, z_ref, x_ref,
                        g_ref, dsk_ref, snw_ref, wo_ref, nw_ref, out_ref, h_sc, yb_sc, *, cps):
    q = SSD_CHUNK

    @pl.when(pl.program_id(1) == 0)
    def _():
        h_sc[...] = h0_ref[0]

    for ci in range(cps - 1, -1, -1):
        rows = slice(ci * q, (ci + 1) * q)

        def store(pair, val, rows=rows):
            yb_sc[rows, pair * LANES:(pair + 1) * LANES] = val

        _ssd_chunk(xs_ref, dt_ref, bm_ref, cm_ref, rows, a_ref[...], selx_ref, sell_ref, h_sc, True, store)

    y = yf_ref[0] + yb_sc[...] + dsk_ref[...] * xs_ref[0]
    yn = _rms(y * _silu(z_ref[0]), snw_ref[...])
    o = _dot(yn.astype(BF16), wo_ref[...])
    out_ref[0] = x_ref[0] + g_ref[0] * _rms(o, nw_ref[...])


def _ssd_bwd_out(xs, dt, bm, cm, a_row, selx, sell, h0, yf, z, x, gate, dskip, snw, w_out, nw, cps):
    b, t, d = x.shape
    tm = cps * SSD_CHUNK
    nblk = t // tm
    blk = lambda c, j=0: pl.BlockSpec((1, tm, c), lambda bb, i: (bb, nblk - 1 - i, j))
    return pl.pallas_call(
        functools.partial(_ssd_bwd_out_kernel, cps=cps),
        grid=(b, nblk),
        in_specs=[blk(D_INNER), blk(LANES, 1), blk(BC_DIM), blk(BC_DIM), _dir_spec(1),
                  _const_spec(selx.shape), _const_spec(sell.shape),
                  pl.BlockSpec((1, SSM_GROUPS, D_STATE, GROUP_W), lambda bb, i: (bb, 0, 0, 0)),
                  blk(D_INNER), blk(D_INNER), blk(d),
                  pl.BlockSpec((1, 1, d), lambda bb, i: (bb, 0, 0)),
                  _const_spec((1, D_INNER)), _const_spec((1, D_INNER)), _const_spec((D_INNER, d)),
                  _const_spec((1, d))],
        out_specs=blk(d),
        out_shape=jax.ShapeDtypeStruct((b, t, d), F32),
        scratch_shapes=[pltpu.VMEM((SSM_GROUPS, D_STATE, GROUP_W), F32), pltpu.VMEM((tm, D_INNER), F32)],
        compiler_params=_params("parallel", "arbitrary"),
        name="ssd_bwd_out",
    )(xs, dt, bm, cm, a_row, selx, sell, h0, yf, z, x, gate, dskip, snw, w_out, nw)


def _rope_tables(t):
    rows = t // GRID_W
    row = jnp.repeat(jnp.arange(rows, dtype=F32), GRID_W)
    col = jnp.tile(jnp.arange(GRID_W, dtype=F32), rows)
    half = HEAD_DIM // 2
    inv_freq = ROPE_THETA ** (-jnp.arange(0, half, 2, dtype=F32) / half)
    ang = jnp.concatenate([row[:, None] * inv_freq, col[:, None] * inv_freq], axis=-1)
    cos, sin = jnp.cos(ang), jnp.sin(ang)
    return jnp.concatenate([cos, cos], axis=-1), jnp.concatenate([-sin, sin], axis=-1)


def _tile(t, pref):
    return pref if t % pref == 0 else t


def kernel(x, c, ctx, c_ctx, ada_w, ada_b, norm_w, attn_w_in, pool_w, pool_scale, q_gain, k_gain, attn_w_out,
           ssm_w_in, ssm_conv_w, ssm_conv_b, ssm_A_log, ssm_dt_bias, ssm_D, ssm_norm_w, ssm_w_out,
           ffn_w_up, ffn_conv_w, ffn_conv_b, ffn_w_down):
    bsz, t, d = x.shape
    lc = ctx.shape[1]
    depth = ada_w.shape[0]
    assert depth == 2 and attn_w_in.shape[0] == 1 and ssm_w_in.shape[0] == 1
    d_ff = ffn_w_down.shape[1]

    rows = -(-(bsz + 1) // HALO) * HALO
    cond = jnp.zeros((rows, d), F32).at[:bsz].set(c).at[bsz].set(c_ctx)
    ada = _adaln(cond, ada_w, ada_b)
    mod_l = [[ada[i, :bsz, k * d:(k + 1) * d].reshape(bsz, 1, d) for k in range(6)] for i in range(depth)]
    mod_c = [[ada[i, bsz:bsz + 1, k * d:(k + 1) * d].reshape(1, 1, d) for k in range(6)] for i in range(depth)]
    nw = lambda i, k: norm_w[i, k].reshape(1, d)

    fc = 256
    nch = d_ff // fc

    def ffn_weights(i):
        wu = ffn_w_up[i].astype(BF16)
        wup = jnp.concatenate([wu[:, :d_ff].reshape(d, nch, fc), wu[:, d_ff:].reshape(d, nch, fc)], axis=-1)
        wup = wup.transpose(1, 0, 2)
        cwt = ffn_conv_w[i].T
        cw = jnp.concatenate([cwt[:, :d_ff].reshape(3, nch, fc), cwt[:, d_ff:].reshape(3, nch, fc)], axis=-1)
        cw = cw.transpose(1, 0, 2)
        cb = jnp.concatenate([ffn_conv_b[i][:d_ff].reshape(nch, 1, fc), ffn_conv_b[i][d_ff:].reshape(nch, 1, fc)],
                             axis=-1)
        wdn = ffn_w_down[i].astype(BF16).reshape(nch, fc, d)
        return wup, cw, cb, wdn

    def ffn(xx, mods, i, tm):
        wup, cw, cb, wdn = ffn_weights(i)
        return _ffn(xx, mods[3], mods[4], mods[5], nw(i, 2), nw(i, 3), wup, cw, cb, wdn, tm)

    perm = jnp.concatenate([jnp.arange(0, HEAD_DIM, 2), jnp.arange(1, HEAD_DIM, 2)])
    cols = jnp.arange(attn_w_in.shape[2])
    qk_lo, qk_hi = POOL_DIM, POOL_DIM + ATTN_DIM + KV_DIM
    qk_cols = (qk_lo + (jnp.arange(qk_hi - qk_lo) // HEAD_DIM) * HEAD_DIM
               + perm[jnp.arange(qk_hi - qk_lo) % HEAD_DIM])
    cols = cols.at[qk_lo:qk_hi].set(qk_cols)
    w_in_a = attn_w_in[0][:, cols].astype(BF16)
    qg = q_gain[0][perm].reshape(1, HEAD_DIM)
    kg = k_gain[0][perm].reshape(1, HEAD_DIM)
    cos2, sin2 = _rope_tables(t)
    eye = jnp.eye(len(POOL_WINDOWS), dtype=F32)
    pool_bd = (eye[:, None, :, None] * pool_w[0][:, :, None, :]).reshape(POOL_DIM, POOL_DIM).astype(BF16)
    ps = pool_scale[0].reshape(1, POOL_DIM)
    w_out_a = attn_w_out[0][:POOL_DIM].astype(BF16)
    w_out_o = attn_w_out[0][POOL_DIM:].astype(BF16)

    tm_l = _tile(t, 512)
    tm_c = _tile(lc, 512)
    a_c, q_c, k_c, v_c = _attn_in(ctx, mod_c[0][0], mod_c[0][1], nw(0, 0), w_in_a, qg, kg, None, None, tm_c)
    a_l, q_l, k_l, v_l = _attn_in(x, mod_l[0][0], mod_l[0][1], nw(0, 0), w_in_a, qg, kg, cos2, sin2, tm_l)
    k_all = jnp.concatenate([k_c, k_l], axis=1)
    v_all = jnp.concatenate([v_c, v_l], axis=1)
    o_l = _attention(q_l, k_all, v_all, _tile(t, 256), 512)
    o_c = _attention(q_c, k_c, v_c, _tile(lc, 256), 512)
    x = _attn_out(x, mod_l[0][2], nw(0, 1), a_l, o_l, pool_bd, ps, w_out_a, w_out_o, tm_l)
    ctx = _attn_out(ctx, mod_c[0][2], nw(0, 1), a_c, o_c, pool_bd, ps, w_out_a, w_out_o, tm_c)
    x = ffn(x, mod_l[0], 0, tm_l)
    ctx = ffn(ctx, mod_c[0], 0, tm_c)

    w_in_c = ssm_w_in[0].astype(BF16)
    conv_dim = D_INNER + 2 * BC_DIM
    wz = w_in_c[:, :D_INNER]
    wx = w_in_c[:, D_INNER:D_INNER + conv_dim]

    def dir_lanes(v):
        pad = jnp.zeros(v.shape[:-2] + (LANES - DT_REP * SSM_HEADS,), v.dtype)
        return jnp.concatenate([p for r in range(2) for p in [v[..., r, :]] * DT_REP + [pad]], axis=-1)

    wdt = dir_lanes(w_in_c[:, D_INNER + conv_dim:].reshape(d, 2, SSM_HEADS))
    dtb = dir_lanes(ssm_dt_bias[0]).reshape(1, 2 * LANES)
    a_row = dir_lanes(-jnp.exp(ssm_A_log[0].astype(F32))).reshape(1, 2 * LANES)
    selx = _sel_matrix(SSM_HEAD_DIM)
    sell = _sel_matrix(SSD_CHUNK)
    scw = ssm_conv_w[0].T
    scb = ssm_conv_b[0].reshape(1, conv_dim)
    dskip = jnp.repeat(ssm_D[0], SSM_HEAD_DIM).reshape(1, D_INNER)
    snw = ssm_norm_w[0].reshape(1, D_INNER)
    w_out_c = ssm_w_out[0].astype(BF16)

    tm_s = _tile(t, 256)
    tm_sc = _tile(lc, 256)
    _, xs_c, bm_c, cm_c, dt_c = _ssm_in(ctx, mod_c[1][0], mod_c[1][1], nw(1, 0), wz, wx, wdt, scw, scb, dtb, tm_sc)
    hf0, hb0 = _ssd_states(xs_c, dt_c, bm_c, cm_c, a_row, selx, sell)
    z_l, xs_l, bm_l, cm_l, dt_l = _ssm_in(x, mod_l[1][0], mod_l[1][1], nw(1, 0), wz, wx, wdt, scw, scb, dtb, tm_s)
    cps = 2 if t % (2 * SSD_CHUNK) == 0 else 1
    yf = _ssd_fwd(xs_l, dt_l, bm_l, cm_l, a_row, selx, sell, hf0, cps)
    x = _ssd_bwd_out(xs_l, dt_l, bm_l, cm_l, a_row, selx, sell, hb0, yf, z_l, x, mod_l[1][2], dskip, snw, w_out_c,
                     nw(1, 1), cps)
    x = ffn(x, mod_l[1], 1, tm_l)
    return x
```

```python
import functools
import math

import jax
import jax.numpy as jnp
from jax import lax
from jax.experimental import pallas as pl
from jax.experimental.pallas import tpu as pltpu

F32 = jnp.float32
BF16 = jnp.bfloat16

EPS = 1e-6
GRID_W = 64
ROPE_THETA = 10000.0

HEAD_DIM = 128
N_Q_HEADS = 6
N_KV_HEADS = 2
Q_PER_KV = N_Q_HEADS // N_KV_HEADS
POOL_WINDOWS = (2, 4, 8, 16)
POOL_GROUP_DIM = 64
POOL_DIM = 256
ATTN_DIM = N_Q_HEADS * HEAD_DIM
KV_DIM = N_KV_HEADS * HEAD_DIM

SSM_HEAD_DIM = 64
SSM_HEADS = 32
SSM_GROUPS = 4
HEADS_PER_GROUP = 8
D_STATE = 128
D_INNER = SSM_HEADS * SSM_HEAD_DIM
GROUP_W = HEADS_PER_GROUP * SSM_HEAD_DIM
BC_DIM = SSM_GROUPS * D_STATE
SSD_CHUNK = 128

HALO = 8
LANES = 128
VMEM_LIMIT = 56 * 1024 * 1024


def _params(*sem):
    return pltpu.CompilerParams(dimension_semantics=sem, vmem_limit_bytes=VMEM_LIMIT)


def _const_spec(shape):
    nd = len(shape)
    return pl.BlockSpec(shape, lambda *_: (0,) * nd, pipeline_mode=pl.Buffered(1))


def _row_spec(c, shared):
    if shared:
        return pl.BlockSpec((1, 1, c), lambda b, i: (0, 0, 0))
    return pl.BlockSpec((1, 1, c), lambda b, i: (b, 0, 0))


def _halo_specs(tm, c, t):
    nb = tm // HALO
    last = t // HALO - 1
    main = pl.BlockSpec((1, tm, c), lambda b, i: (b, i, 0))
    prev = pl.BlockSpec((1, HALO, c), lambda b, i: (b, jnp.maximum(i * nb - 1, 0), 0))
    nxt = pl.BlockSpec((1, HALO, c), lambda b, i: (b, jnp.minimum((i + 1) * nb, last), 0))
    return main, prev, nxt


def _rms(xf, w):
    ms = jnp.mean(xf * xf, axis=-1, keepdims=True)
    return xf * lax.rsqrt(ms + EPS) * w


def _silu(x):
    return x * jax.nn.sigmoid(x)


def _dot(a, b):
    return jnp.dot(a, b, preferred_element_type=F32)


def _ada_kernel(c_ref, w_ref, b_ref, o_ref):
    s = _silu(c_ref[...])
    o_ref[0] = _dot(s.astype(BF16), w_ref[0].astype(BF16)) + b_ref[0]


def _adaln(cond, ada_w, ada_b):
    depth, d, n = ada_w.shape
    rows = cond.shape[0]
    tn = 1536
    return pl.pallas_call(
        _ada_kernel,
        grid=(depth, n // tn),
        in_specs=[
            pl.BlockSpec((rows, d), lambda l, j: (0, 0)),
            pl.BlockSpec((1, d, tn), lambda l, j: (l, 0, j)),
            pl.BlockSpec((1, 1, tn), lambda l, j: (l, 0, j)),
        ],
        out_specs=pl.BlockSpec((1, rows, tn), lambda l, j: (l, 0, j)),
        out_shape=jax.ShapeDtypeStruct((depth, rows, n), F32),
        compiler_params=_params("parallel", "parallel"),
        name="adaln",
    )(cond, ada_w, ada_b.reshape(depth, 1, n))


def _attn_in_kernel(*refs, rope):
    if rope:
        (x_ref, sh_ref, sc_ref, nw_ref, w_ref, qg_ref, kg_ref, cos_ref, sin_ref,
         a_ref, q_ref, k_ref, v_ref) = refs
        cos2, sin2 = cos_ref[...], sin_ref[...]
    else:
        (x_ref, sh_ref, sc_ref, nw_ref, w_ref, qg_ref, kg_ref,
         a_ref, q_ref, k_ref, v_ref) = refs
    h = _rms(x_ref[0], nw_ref[...]) * (1.0 + sc_ref[0]) + sh_ref[0]
    u = _dot(h.astype(BF16), w_ref[...])
    a_ref[0] = u[:, :POOL_DIM]

    def norm_rope(t, gain):
        y = _rms(t, gain)
        if rope:
            y = y * cos2 + pltpu.roll(y, HEAD_DIM // 2, 1) * sin2
        return y.astype(BF16)

    off = POOL_DIM
    for hh in range(N_Q_HEADS):
        q_ref[0, :, hh * HEAD_DIM:(hh + 1) * HEAD_DIM] = norm_rope(
            u[:, off + hh * HEAD_DIM: off + (hh + 1) * HEAD_DIM], qg_ref[...])
    off += ATTN_DIM
    for hh in range(N_KV_HEADS):
        k_ref[0, :, hh * HEAD_DIM:(hh + 1) * HEAD_DIM] = norm_rope(
            u[:, off + hh * HEAD_DIM: off + (hh + 1) * HEAD_DIM], kg_ref[...])
    off += KV_DIM
    v_ref[0] = u[:, off:].astype(BF16)


def _attn_in(x, shift, scale, nw, w_in, qg, kg, cos2, sin2, tm):
    b, t, d = x.shape
    shared = shift.shape[0] == 1
    rope = cos2 is not None
    width = w_in.shape[1]
    in_specs = [
        pl.BlockSpec((1, tm, d), lambda bb, i: (bb, i, 0)),
        _row_spec(d, shared), _row_spec(d, shared),
        _const_spec((1, d)), _const_spec((d, width)),
        _const_spec((1, HEAD_DIM)), _const_spec((1, HEAD_DIM)),
    ]
    args = [x, shift, scale, nw, w_in, qg, kg]
    if rope:
        in_specs += [pl.BlockSpec((tm, HEAD_DIM), lambda bb, i: (i, 0))] * 2
        args += [cos2, sin2]
    outs = [(POOL_DIM, F32), (ATTN_DIM, BF16), (KV_DIM, BF16), (KV_DIM, BF16)]
    return pl.pallas_call(
        functools.partial(_attn_in_kernel, rope=rope),
        grid=(b, t // tm),
        in_specs=in_specs,
        out_specs=[pl.BlockSpec((1, tm, c), lambda bb, i: (bb, i, 0)) for c, _ in outs],
        out_shape=[jax.ShapeDtypeStruct((b, t, c), dt) for c, dt in outs],
        compiler_params=_params("parallel", "parallel"),
        name="attn_in",
    )(*args)


SUM_ROWS = 16
SOFTMAX_C = (HEAD_DIM ** -0.5) * math.log2(math.e)
ATTN_UNROLL = 8


def _attn_kernel(q_ref, k_ref, vt_ref, o_ref, m_sc, acc_sc, s0_sc, mc0_sc, s1_sc, mc1_sc, *, n, tq):
    slots = ((s0_sc, mc0_sc), (s1_sc, mc1_sc))

    q = q_ref[0]
    qs = jnp.concatenate([q[:, g * HEAD_DIM:(g + 1) * HEAD_DIM] for g in range(Q_PER_KV)], axis=0)
    m_sc[...] = jnp.full(m_sc.shape, -jnp.inf, F32)
    acc_sc[...] = jnp.zeros(acc_sc.shape, F32)

    def produce(j, slot, g):
        s_sc, mc_sc = slots[slot]
        cols = slice(g * tq, (g + 1) * tq)
        s = lax.dot_general(k_ref[0, 0, j], qs[cols], (((1,), (1,)), ((), ())),
                            preferred_element_type=F32)
        s_sc[:, cols] = s
        mc_sc[:, cols] = jnp.max(s, axis=0, keepdims=True)

    def consume(j, slot, g):
        s_sc, mc_sc = slots[slot]
        cols = slice(g * tq, (g + 1) * tq)
        m_prev = m_sc[:, cols]
        m_new = jnp.maximum(m_prev, mc_sc[:, cols])
        alpha = jnp.exp2(m_prev - m_new)
        p = jnp.exp2(s_sc[:, cols] - m_new).astype(BF16)
        acc_sc[:, cols] = alpha * acc_sc[:, cols] + _dot(vt_ref[0, 0, j], p)
        m_sc[:, cols] = m_new

    def step(jp, sp, jc, sc):
        for g in range(Q_PER_KV):
            if jp is not None:
                produce(jp, sp, g)
            if jc is not None:
                consume(jc, sc, g)

    step(0, 0, None, None)

    def body(jj, carry):
        for u in range(ATTN_UNROLL):
            step(ATTN_UNROLL * jj + u + 1, (u + 1) % 2, ATTN_UNROLL * jj + u, u % 2)
        return carry

    trips = (n - 1) // ATTN_UNROLL
    lax.fori_loop(0, trips, body, 0)
    for j in range(trips * ATTN_UNROLL, n - 1):
        step(j + 1, (j + 1) % 2, j, j % 2)
    step(None, None, n - 1, (n - 1) % 2)

    acc = acc_sc[...]
    o_t = acc[:HEAD_DIM] / acc[HEAD_DIM:HEAD_DIM + 1]
    for g in range(Q_PER_KV):
        o_ref[0, :, g * HEAD_DIM:(g + 1) * HEAD_DIM] = o_t[:, g * tq:(g + 1) * tq].T.astype(BF16)


def _attention(q, k, v, tq, tk):
    b, t, _ = q.shape
    l = k.shape[1]
    gw = Q_PER_KV * HEAD_DIM
    tk = min(tk, l)
    pad = (-l) % tk
    assert pad <= l
    n = (l + pad) // tk
    k = jnp.concatenate([k, k[:, :pad]], axis=1)
    v = jnp.concatenate([v, jnp.zeros((b, pad, KV_DIM), BF16)], axis=1)
    live = (jnp.arange(l + pad) < l).astype(BF16).reshape(n, 1, tk)
    kc = k.reshape(b, n, tk, N_KV_HEADS, HEAD_DIM).transpose(0, 3, 1, 2, 4)
    vt = v.reshape(b, n, tk, N_KV_HEADS, HEAD_DIM).transpose(0, 3, 1, 4, 2)
    vt = jnp.concatenate([vt, jnp.broadcast_to(live, (b, N_KV_HEADS, n, SUM_ROWS, tk))], axis=3)
    m = Q_PER_KV * tq
    return pl.pallas_call(
        functools.partial(_attn_kernel, n=n, tq=tq),
        grid=(b, N_KV_HEADS, t // tq),
        in_specs=[
            pl.BlockSpec((1, tq, gw), lambda bb, g, i: (bb, i, g)),
            pl.BlockSpec((1, 1) + kc.shape[2:], lambda bb, g, i: (bb, g, 0, 0, 0)),
            pl.BlockSpec((1, 1) + vt.shape[2:], lambda bb, g, i: (bb, g, 0, 0, 0)),
        ],
        out_specs=pl.BlockSpec((1, tq, gw), lambda bb, g, i: (bb, i, g)),
        out_shape=jax.ShapeDtypeStruct((b, t, ATTN_DIM), BF16),
        scratch_shapes=[pltpu.VMEM((1, m), F32), pltpu.VMEM((HEAD_DIM + SUM_ROWS, m), F32)]
        + [pltpu.VMEM((tk, m), F32), pltpu.VMEM((1, m), F32)] * 2,
        compiler_params=_params("parallel", "parallel", "arbitrary"),
        name="attention",
    )(q, kc, vt)


def _attn_out_kernel(x_ref, g_ref, nw_ref, a_ref, ap_ref, an_ref, o_ref, pw_ref, ps_ref, wa_ref, wo_ref,
                     out_ref, abuf, *, tm, t):
    i = pl.program_id(1)
    last = pl.num_programs(1) - 1
    abuf[0:HALO] = jnp.where(i == 0, 0.0, ap_ref[0])
    abuf[HALO:HALO + tm] = a_ref[0]
    abuf[HALO + tm:] = jnp.where(i == last, 0.0, an_ref[0])

    def sh(d):
        return abuf[HALO + d:HALO + d + tm, :]

    x0 = sh(0)
    s2 = sh(-1) + x0
    s4 = s2 + sh(-2) + sh(1)
    s8 = s4 + sh(-4) + sh(-3) + sh(2) + sh(3)
    s16 = s8 + sh(-8) + sh(-7) + sh(-6) + sh(-5) + sh(4) + sh(5) + sh(6) + sh(7)
    pos = i * tm + lax.broadcasted_iota(jnp.int32, (tm, 1), 0)
    lane = lax.broadcasted_iota(jnp.int32, (1, POOL_DIM), 1)

    def mean(s, w):
        left = w // 2
        right = w - 1 - left
        cnt = jnp.minimum(pos + right, t - 1) - jnp.maximum(pos - left, 0) + 1
        return s / cnt.astype(F32)

    pm = jnp.where(lane < 64, mean(s2, 2),
                   jnp.where(lane < 128, mean(s4, 4),
                             jnp.where(lane < 192, mean(s8, 8), mean(s16, 16))))
    p = pm - x0
    py = _dot(p.astype(BF16), pw_ref[...]) * ps_ref[...]
    y = _dot(py.astype(BF16), wa_ref[...]) + _dot(o_ref[0], wo_ref[...])
    out_ref[0] = x_ref[0] + g_ref[0] * _rms(y, nw_ref[...])


def _attn_out(x, gate, nw, a, o, pool_bd, pool_scale, w_a, w_o, tm):
    b, t, d = x.shape
    shared = gate.shape[0] == 1
    am, ap, an = _halo_specs(tm, POOL_DIM, t)
    return pl.pallas_call(
        functools.partial(_attn_out_kernel, tm=tm, t=t),
        grid=(b, t // tm),
        in_specs=[
            pl.BlockSpec((1, tm, d), lambda bb, i: (bb, i, 0)),
            _row_spec(d, shared), _const_spec((1, d)),
            am, ap, an,
            pl.BlockSpec((1, tm, ATTN_DIM), lambda bb, i: (bb, i, 0)),
            _const_spec((POOL_DIM, POOL_DIM)), _const_spec((1, POOL_DIM)),
            _const_spec((POOL_DIM, d)), _const_spec((ATTN_DIM, d)),
        ],
        out_specs=pl.BlockSpec((1, tm, d), lambda bb, i: (bb, i, 0)),
        out_shape=jax.ShapeDtypeStruct((b, t, d), F32),
        scratch_shapes=[pltpu.VMEM((tm + 2 * HALO, POOL_DIM), F32)],
        compiler_params=_params("parallel", "parallel"),
        name="attn_out",
    )(x, gate, nw, a, a, a, o, pool_bd, pool_scale, w_a, w_o)


FFN_UNROLL = 2


def _ffn_kernel(x_ref, xp_ref, xn_ref, sh_ref, sc_ref, g_ref, nw2_ref, nw3_ref,
                wup_ref, cw_ref, cb_ref, wdn_ref, out_ref, xe_sc, h_sc, u0_sc, u1_sc, acc_sc, *, tm, fc, nch):
    i = pl.program_id(1)
    last = pl.num_programs(1) - 1
    xe_sc[0:HALO] = xp_ref[0]
    xe_sc[HALO:HALO + tm] = x_ref[0]
    xe_sc[HALO + tm:] = xn_ref[0]
    h = _rms(xe_sc[...], nw2_ref[...]) * (1.0 + sc_ref[0]) + sh_ref[0]
    h_sc[...] = h.astype(BF16)
    acc_sc[...] = jnp.zeros(acc_sc.shape, F32)
    keep_top = jnp.where(i == 0, 0.0, 1.0)
    keep_bot = jnp.where(i == last, 0.0, 1.0)
    slots = (u0_sc, u1_sc)

    def produce(j, slot):
        u = _dot(h_sc[...], wup_ref[j])
        u_sc = slots[slot]
        u_sc[0:HALO] = u[0:HALO] * keep_top
        u_sc[HALO:HALO + tm] = u[HALO:HALO + tm]
        u_sc[HALO + tm:] = u[HALO + tm:] * keep_bot

    def consume(j, slot):
        u_sc = slots[slot]
        cw = cw_ref[j]
        cv = cb_ref[j]
        for k in range(3):
            cv = cv + u_sc[HALO - 1 + k:HALO - 1 + k + tm, :] * cw[k:k + 1]
        g = _silu(cv[:, fc:]) * cv[:, :fc]
        acc_sc[...] += _dot(g.astype(BF16), wdn_ref[j])

    produce(0, 0)

    def body(jj, carry):
        for u in range(FFN_UNROLL):
            produce(FFN_UNROLL * jj + u + 1, (u + 1) % 2)
            consume(FFN_UNROLL * jj + u, u % 2)
        return carry

    trips = (nch - 1) // FFN_UNROLL
    if trips > 1:
        lax.fori_loop(0, trips, body, 0)
    elif trips == 1:
        body(0, 0)
    for j in range(trips * FFN_UNROLL, nch - 1):
        produce(j + 1, (j + 1) % 2)
        consume(j, j % 2)
    consume(nch - 1, (nch - 1) % 2)
    out_ref[0] = x_ref[0] + g_ref[0] * _rms(acc_sc[...], nw3_ref[...])


def _ffn(x, shift, scale, gate, nw2, nw3, wup, cw, cb, wdn, tm):
    b, t, d = x.shape
    shared = shift.shape[0] == 1
    nch, _, fc2 = wup.shape
    fc = fc2 // 2
    xm, xp, xn = _halo_specs(tm, d, t)
    return pl.pallas_call(
        functools.partial(_ffn_kernel, tm=tm, fc=fc, nch=nch),
        grid=(b, t // tm),
        in_specs=[
            xm, xp, xn,
            _row_spec(d, shared), _row_spec(d, shared), _row_spec(d, shared),
            _const_spec((1, d)), _const_spec((1, d)),
            _const_spec(wup.shape), _const_spec(cw.shape), _const_spec(cb.shape), _const_spec(wdn.shape),
        ],
        out_specs=pl.BlockSpec((1, tm, d), lambda bb, i: (bb, i, 0)),
        out_shape=jax.ShapeDtypeStruct((b, t, d), F32),
        scratch_shapes=[
            pltpu.VMEM((tm + 2 * HALO, d), F32),
            pltpu.VMEM((tm + 2 * HALO, d), BF16),
            pltpu.VMEM((tm + 2 * HALO, fc2), F32),
            pltpu.VMEM((tm + 2 * HALO, fc2), F32),
            pltpu.VMEM((tm, d), F32),
        ],
        compiler_params=_params("parallel", "parallel"),
        name="conv_ffn",
    )(x, x, x, shift, scale, gate, nw2, nw3, wup, cw, cb, wdn)


def _ssm_in_kernel(x_ref, xp_ref, xn_ref, sh_ref, sc_ref, nw_ref, wz_ref, wx_ref, wdt_ref,
                   cw_ref, cb_ref, dtb_ref, z_ref, xs_ref, bm_ref, cm_ref, dt_ref, xe_sc, u0_sc, u1_sc, *, tm, cc):
    u_scs = (u0_sc, u1_sc)
    i = pl.program_id(1)
    last = pl.num_programs(1) - 1
    xe_sc[0:HALO] = xp_ref[0]
    xe_sc[HALO:HALO + tm] = x_ref[0]
    xe_sc[HALO + tm:] = xn_ref[0]
    hf = _rms(xe_sc[...], nw_ref[...]) * (1.0 + sc_ref[0]) + sh_ref[0]
    he = hf.astype(BF16)
    hm = hf[HALO:HALO + tm].astype(BF16)
    keep_top = jnp.where(i == 0, 0.0, 1.0)
    keep_bot = jnp.where(i == last, 0.0, 1.0)

    for j in range(D_INNER // cc):
        z_ref[0, :, j * cc:(j + 1) * cc] = _dot(hm, wz_ref[:, j * cc:(j + 1) * cc])

    dt_raw = _dot(hm, wdt_ref[...]) + dtb_ref[...]
    dt_ref[0] = jnp.maximum(dt_raw, 0.0) + jnp.log1p(jnp.exp(-jnp.abs(dt_raw)))

    conv_dim = D_INNER + 2 * BC_DIM
    for j in range(conv_dim // cc):
        lo = j * cc
        u_sc = u_scs[j % 2]
        u = _dot(he, wx_ref[:, lo:lo + cc])
        u_sc[0:HALO] = u[0:HALO] * keep_top
        u_sc[HALO:HALO + tm] = u[HALO:HALO + tm]
        u_sc[HALO + tm:] = u[HALO + tm:] * keep_bot

        cv = cb_ref[:, lo:lo + cc]
        for k in range(4):
            cv = cv + u_sc[HALO - 2 + k:HALO - 2 + k + tm, :] * cw_ref[k:k + 1, lo:lo + cc]
        act = _silu(cv)
        if lo < D_INNER:
            xs_ref[0, :, lo:lo + cc] = act
        elif lo < D_INNER + BC_DIM:
            bm_ref[0, :, lo - D_INNER:lo - D_INNER + cc] = act.astype(BF16)
        else:
            cm_ref[0, :, lo - D_INNER - BC_DIM:lo - D_INNER - BC_DIM + cc] = act.astype(BF16)


def _ssm_in(x, shift, scale, nw, wz, wx, wdt, cw, cb, dtb, tm):
    b, t, d = x.shape
    shared = shift.shape[0] == 1
    cc = 512
    xm, xp, xn = _halo_specs(tm, d, t)
    outs = [(D_INNER, F32), (D_INNER, F32), (BC_DIM, BF16), (BC_DIM, BF16), (2 * LANES, F32)]
    return pl.pallas_call(
        functools.partial(_ssm_in_kernel, tm=tm, cc=cc),
        grid=(b, t // tm),
        in_specs=[
            xm, xp, xn, _row_spec(d, shared), _row_spec(d, shared), _const_spec((1, d)),
            _const_spec(wz.shape), _const_spec(wx.shape), _const_spec(wdt.shape),
            _const_spec(cw.shape), _const_spec(cb.shape), _const_spec(dtb.shape),
        ],
        out_specs=[pl.BlockSpec((1, tm, c), lambda bb, i: (bb, i, 0)) for c, _ in outs],
        out_shape=[jax.ShapeDtypeStruct((b, t, c), dt) for c, dt in outs],
        scratch_shapes=[pltpu.VMEM((tm + 2 * HALO, d), F32)] + [pltpu.VMEM((tm + 2 * HALO, cc), F32)] * 2,
        compiler_params=_params("parallel", "parallel"),
        name="ssm_in",
    )(x, x, x, shift, scale, nw, wz, wx, wdt, cw, cb, dtb)


DT_REP = 3


def _split3(v, lane):
    p1 = v.astype(BF16).astype(F32)
    r1 = v - p1
    p2 = r1.astype(BF16).astype(F32)
    r2 = r1 - p2
    packed = jnp.where(lane < 32, p1, jnp.where(lane < 64, p2, jnp.where(lane < 96, r2, 0.0)))
    return packed.astype(BF16)


def _sel_matrix(width):
    r = jnp.arange(LANES)[:, None]
    c = jnp.arange(SSM_HEADS * width)[None, :]
    return ((c // width == r % SSM_HEADS) & (r < DT_REP * SSM_HEADS)).astype(BF16)


def _ssd_chunk(xs_ref, dt_ref, bm_ref, cm_ref, rows, a_row, selx_ref, sell_ref, h_sc, reverse, y_store):
    q = SSD_CHUNK
    row = lax.broadcasted_iota(jnp.int32, (q, q), 0)
    col = lax.broadcasted_iota(jnp.int32, (q, q), 1)
    mask = (col >= row) if reverse else (col <= row)
    tri = jnp.where(mask, 1.0, 0.0).astype(BF16)
    lane = lax.broadcasted_iota(jnp.int32, (1, LANES), 1)
    lane_lo = lane < SSM_HEAD_DIM
    end = 0 if reverse else q - 1

    dt = dt_ref[0, rows]
    a = dt * a_row
    a1 = a.astype(BF16)
    r1 = a - a1.astype(F32)
    a2 = r1.astype(BF16)
    a3 = (r1 - a2.astype(F32)).astype(BF16)
    acum = _dot(tri, a1) + _dot(tri, a2) + _dot(tri, a3)
    dt_p = _split3(dt, lane)
    acum_p = _split3(acum, lane)
    if y_store is not None:
        acum_t = acum.T

    for g in range(SSM_GROUPS):
        gs = slice(g * D_STATE, (g + 1) * D_STATE)
        b_g = bm_ref[0, rows, gs]
        c_g = cm_ref[0, rows, gs]
        h_g = h_sc[g]
        dt_x = _dot(dt_p, selx_ref[:, g * GROUP_W:(g + 1) * GROUP_W])
        if y_store is not None:
            cb = lax.dot_general(c_g, b_g, (((1,), (1,)), ((), ())), preferred_element_type=F32)
            y_inter = _dot(c_g, h_g.astype(BF16))
        xd_parts = []
        etot_parts = []
        for j in range(HEADS_PER_GROUP // 2):
            hd = g * HEADS_PER_GROUP + 2 * j
            pair = hd // 2
            pl_ = slice(j * LANES, (j + 1) * LANES)
            d2 = _dot(acum_p, sell_ref[:, pair * 2 * q:(pair + 1) * 2 * q])
            dmix = jnp.where(lane_lo, d2[:, :q], d2[:, q:])
            tot = dmix[end:end + 1]
            xdt = xs_ref[0, rows, pair * LANES:(pair + 1) * LANES] * dt_x[:, pl_]
            xd_parts.append((xdt * jnp.exp2(tot - dmix)).astype(BF16))
            etot_parts.append(jnp.exp2(tot))
            if y_store is not None:
                lm = []
                for k in range(2):
                    seg = d2[:, k * q:(k + 1) * q] - acum_t[hd + k:hd + k + 1, :]
                    lm.append(cb * jnp.exp2(jnp.where(mask, seg, -jnp.inf)))
                m2 = jnp.concatenate(lm, axis=1).astype(BF16)
                xb = xdt.astype(BF16)
                zero = jnp.zeros_like(xb)
                rhs = jnp.concatenate([jnp.where(lane_lo, xb, zero), jnp.where(lane_lo, zero, xb)], axis=0)
                y_intra = _dot(m2, rhs)
                y_store(pair, y_intra + y_inter[:, pl_] * jnp.exp2(dmix))
        xd = jnp.concatenate(xd_parts, axis=1)
        upd = lax.dot_general(b_g, xd, (((0,), (0,)), ((), ())), preferred_element_type=F32)
        h_sc[g] = jnp.concatenate(etot_parts, axis=1) * h_g + upd


def _ssd_state_kernel(xs_ref, dtf_ref, dtb_ref, bm_ref, cm_ref, af_ref, ab_ref, selx_ref, sell_ref,
                      hf_ref, hb_ref, h_sc, *, nchunks):
    q = SSD_CHUNK
    for reverse, dt_ref, a_ref, out_ref in ((False, dtf_ref, af_ref, hf_ref), (True, dtb_ref, ab_ref, hb_ref)):
        h_sc[...] = jnp.zeros(h_sc.shape, F32)
        order = range(nchunks - 1, -1, -1) if reverse else range(nchunks)
        for ci in order:
            rows = slice(ci * q, (ci + 1) * q)
            _ssd_chunk(xs_ref, dt_ref, bm_ref, cm_ref, rows, a_ref[...], selx_ref, sell_ref, h_sc, reverse, None)
        out_ref[0] = h_sc[...]


def _dir_spec(rev):
    return pl.BlockSpec((1, LANES), lambda *_: (0, rev))


def _ssd_states(xs, dt, bm, cm, a_row, selx, sell):
    b, t, _ = xs.shape
    st = jax.ShapeDtypeStruct((b, SSM_GROUPS, D_STATE, GROUP_W), F32)
    full = lambda c, j=0: pl.BlockSpec((1, t, c), lambda bb: (bb, 0, j))
    st_spec = pl.BlockSpec((1, SSM_GROUPS, D_STATE, GROUP_W), lambda bb: (bb, 0, 0, 0))
    return pl.pallas_call(
        functools.partial(_ssd_state_kernel, nchunks=t // SSD_CHUNK),
        grid=(b,),
        in_specs=[full(D_INNER), full(LANES, 0), full(LANES, 1), full(BC_DIM), full(BC_DIM),
                  _dir_spec(0), _dir_spec(1), _const_spec(selx.shape), _const_spec(sell.shape)],
        out_specs=[st_spec, st_spec],
        out_shape=[st, st],
        scratch_shapes=[pltpu.VMEM((SSM_GROUPS, D_STATE, GROUP_W), F32)],
        compiler_params=_params("parallel"),
        name="ssd_ctx_states",
    )(xs, dt, dt, bm, cm, a_row, a_row, selx, sell)


def _ssd_fwd_kernel(xs_ref, dt_ref, bm_ref, cm_ref, a_ref, selx_ref, sell_ref, h0_ref, y_ref, h_sc, *, cps):
    q = SSD_CHUNK

    @pl.when(pl.program_id(1) == 0)
    def _():
        h_sc[...] = h0_ref[0]

    for ci in range(cps):
        rows = slice(ci * q, (ci + 1) * q)

        def store(pair, val, rows=rows):
            y_ref[0, rows, pair * LANES:(pair + 1) * LANES] = val

        _ssd_chunk(xs_ref, dt_ref, bm_ref, cm_ref, rows, a_ref[...], selx_ref, sell_ref, h_sc, False, store)


def _ssd_fwd(xs, dt, bm, cm, a_row, selx, sell, h0, cps):
    b, t, _ = xs.shape
    tm = cps * SSD_CHUNK
    blk = lambda c: pl.BlockSpec((1, tm, c), lambda bb, i: (bb, i, 0))
    return pl.pallas_call(
        functools.partial(_ssd_fwd_kernel, cps=cps),
        grid=(b, t // tm),
        in_specs=[blk(D_INNER), blk(LANES), blk(BC_DIM), blk(BC_DIM), _dir_spec(0),
                  _const_spec(selx.shape), _const_spec(sell.shape),
                  pl.BlockSpec((1, SSM_GROUPS, D_STATE, GROUP_W), lambda bb, i: (bb, 0, 0, 0))],
        out_specs=blk(D_INNER),
        out_shape=jax.ShapeDtypeStruct((b, t, D_INNER), F32),
        scratch_shapes=[pltpu.VMEM((SSM_GROUPS, D_STATE, GROUP_W), F32)],
        compiler_params=_params("parallel", "arbitrary"),
        name="ssd_fwd",
    )(xs, dt, bm, cm, a_row, selx, sell, h0)


def _ssd_bwd_out_kernel(xs_ref, dt_ref, bm_ref, cm_ref, a_ref, selx_ref, sell_ref, h0_ref, yf_ref, z_ref, x_ref,
                        g_ref, dsk_ref, snw_ref, wo_ref, nw_ref, out_ref, h_sc, yb_sc, *, cps):
    q = SSD_CHUNK

    @pl.when(pl.program_id(1) == 0)
    def _():
        h_sc[...] = h0_ref[0]

    for ci in range(cps - 1, -1, -1):
        rows = slice(ci * q, (ci + 1) * q)

        def store(pair, val, rows=rows):
            yb_sc[rows, pair * LANES:(pair + 1) * LANES] = val

        _ssd_chunk(xs_ref, dt_ref, bm_ref, cm_ref, rows, a_ref[...], selx_ref, sell_ref, h_sc, True, store)

    y = yf_ref[0] + yb_sc[...] + dsk_ref[...] * xs_ref[0]
    yn = _rms(y * _silu(z_ref[0]), snw_ref[...])
    o = _dot(yn.astype(BF16), wo_ref[...])
    out_ref[0] = x_ref[0] + g_ref[0] * _rms(o, nw_ref[...])


def _ssd_bwd_out(xs, dt, bm, cm, a_row, selx, sell, h0, yf, z, x, gate, dskip, snw, w_out, nw, cps):
    b, t, d = x.shape
    tm = cps * SSD_CHUNK
    nblk = t // tm
    blk = lambda c, j=0: pl.BlockSpec((1, tm, c), lambda bb, i: (bb, nblk - 1 - i, j))
    return pl.pallas_call(
        functools.partial(_ssd_bwd_out_kernel, cps=cps),
        grid=(b, nblk),
        in_specs=[blk(D_INNER), blk(LANES, 1), blk(BC_DIM), blk(BC_DIM), _dir_spec(1),
                  _const_spec(selx.shape), _const_spec(sell.shape),
                  pl.BlockSpec((1, SSM_GROUPS, D_STATE, GROUP_W), lambda bb, i: (bb, 0, 0, 0)),
                  blk(D_INNER), blk(D_INNER), blk(d),
                  pl.BlockSpec((1, 1, d), lambda bb, i: (bb, 0, 0)),
                  _const_spec((1, D_INNER)), _const_spec((1, D_INNER)), _const_spec((D_INNER, d)),
                  _const_spec((1, d))],
        out_specs=blk(d),
        out_shape=jax.ShapeDtypeStruct((b, t, d), F32),
        scratch_shapes=[pltpu.VMEM((SSM_GROUPS, D_STATE, GROUP_W), F32), pltpu.VMEM((tm, D_INNER), F32)],
        compiler_params=_params("parallel", "arbitrary"),
        name="ssd_bwd_out",
    )(xs, dt, bm, cm, a_row, selx, sell, h0, yf, z, x, gate, dskip, snw, w_out, nw)


def _rope_tables(t):
    rows = t // GRID_W
    row = jnp.repeat(jnp.arange(rows, dtype=F32), GRID_W)
    col = jnp.tile(jnp.arange(GRID_W, dtype=F32), rows)
    half = HEAD_DIM // 2
    inv_freq = ROPE_THETA ** (-jnp.arange(0, half, 2, dtype=F32) / half)
    ang = jnp.concatenate([row[:, None] * inv_freq, col[:, None] * inv_freq], axis=-1)
    cos, sin = jnp.cos(ang), jnp.sin(ang)
    return jnp.concatenate([cos, cos], axis=-1), jnp.concatenate([-sin, sin], axis=-1)


def _tile(t, pref):
    return pref if t % pref == 0 else t


def kernel(x, c, ctx, c_ctx, ada_w, ada_b, norm_w, attn_w_in, pool_w, pool_scale, q_gain, k_gain, attn_w_out,
           ssm_w_in, ssm_conv_w, ssm_conv_b, ssm_A_log, ssm_dt_bias, ssm_D, ssm_norm_w, ssm_w_out,
           ffn_w_up, ffn_conv_w, ffn_conv_b, ffn_w_down):
    bsz, t, d = x.shape
    lc = ctx.shape[1]
    depth = ada_w.shape[0]
    assert depth == 2 and attn_w_in.shape[0] == 1 and ssm_w_in.shape[0] == 1
    d_ff = ffn_w_down.shape[1]

    rows = -(-(bsz + 1) // HALO) * HALO
    cond = jnp.zeros((rows, d), F32).at[:bsz].set(c).at[bsz].set(c_ctx)
    ada = _adaln(cond, ada_w, ada_b)
    mod_l = [[ada[i, :bsz, k * d:(k + 1) * d].reshape(bsz, 1, d) for k in range(6)] for i in range(depth)]
    mod_c = [[ada[i, bsz:bsz + 1, k * d:(k + 1) * d].reshape(1, 1, d) for k in range(6)] for i in range(depth)]
    nw = lambda i, k: norm_w[i, k].reshape(1, d)

    fc = 256
    nch = d_ff // fc

    def ffn_weights(i):
        wu = ffn_w_up[i].astype(BF16)
        wup = jnp.concatenate([wu[:, :d_ff].reshape(d, nch, fc), wu[:, d_ff:].reshape(d, nch, fc)], axis=-1)
        wup = wup.transpose(1, 0, 2)
        cwt = ffn_conv_w[i].T
        cw = jnp.concatenate([cwt[:, :d_ff].reshape(3, nch, fc), cwt[:, d_ff:].reshape(3, nch, fc)], axis=-1)
        cw = cw.transpose(1, 0, 2)
        cb = jnp.concatenate([ffn_conv_b[i][:d_ff].reshape(nch, 1, fc), ffn_conv_b[i][d_ff:].reshape(nch, 1, fc)],
                             axis=-1)
        wdn = ffn_w_down[i].astype(BF16).reshape(nch, fc, d)
        return wup, cw, cb, wdn

    def ffn(xx, mods, i, tm):
        wup, cw, cb, wdn = ffn_weights(i)
        return _ffn(xx, mods[3], mods[4], mods[5], nw(i, 2), nw(i, 3), wup, cw, cb, wdn, tm)

    perm = jnp.concatenate([jnp.arange(0, HEAD_DIM, 2), jnp.arange(1, HEAD_DIM, 2)])
    cols = jnp.arange(attn_w_in.shape[2])
    qk_lo, qk_hi = POOL_DIM, POOL_DIM + ATTN_DIM + KV_DIM
    qk_cols = (qk_lo + (jnp.arange(qk_hi - qk_lo) // HEAD_DIM) * HEAD_DIM
               + perm[jnp.arange(qk_hi - qk_lo) % HEAD_DIM])
    cols = cols.at[qk_lo:qk_hi].set(qk_cols)
    w_in_a = attn_w_in[0][:, cols].astype(BF16)
    qg = (q_gain[0][perm] * SOFTMAX_C).reshape(1, HEAD_DIM)
    kg = k_gain[0][perm].reshape(1, HEAD_DIM)
    cos2, sin2 = _rope_tables(t)
    eye = jnp.eye(len(POOL_WINDOWS), dtype=F32)
    pool_bd = (eye[:, None, :, None] * pool_w[0][:, :, None, :]).reshape(POOL_DIM, POOL_DIM).astype(BF16)
    ps = pool_scale[0].reshape(1, POOL_DIM)
    w_out_a = attn_w_out[0][:POOL_DIM].astype(BF16)
    w_out_o = attn_w_out[0][POOL_DIM:].astype(BF16)

    tm_l = _tile(t, 512)
    tm_c = _tile(lc, 512)
    a_c, q_c, k_c, v_c = _attn_in(ctx, mod_c[0][0], mod_c[0][1], nw(0, 0), w_in_a, qg, kg, None, None, tm_c)
    a_l, q_l, k_l, v_l = _attn_in(x, mod_l[0][0], mod_l[0][1], nw(0, 0), w_in_a, qg, kg, cos2, sin2, tm_l)
    k_all = jnp.concatenate([k_c, k_l], axis=1)
    v_all = jnp.concatenate([v_c, v_l], axis=1)
    o_l = _attention(q_l, k_all, v_all, _tile(t, 256), 512)
    o_c = _attention(q_c, k_c, v_c, _tile(lc, 256), 512)
    x = _attn_out(x, mod_l[0][2], nw(0, 1), a_l, o_l, pool_bd, ps, w_out_a, w_out_o, tm_l)
    ctx = _attn_out(ctx, mod_c[0][2], nw(0, 1), a_c, o_c, pool_bd, ps, w_out_a, w_out_o, tm_c)
    x = ffn(x, mod_l[0], 0, tm_l)
    ctx = ffn(ctx, mod_c[0], 0, tm_c)

    w_in_c = ssm_w_in[0].astype(BF16)
    conv_dim = D_INNER + 2 * BC_DIM
    wz = w_in_c[:, :D_INNER]
    wx = w_in_c[:, D_INNER:D_INNER + conv_dim]

    def dir_lanes(v):
        pad = jnp.zeros(v.shape[:-2] + (LANES - DT_REP * SSM_HEADS,), v.dtype)
        return jnp.concatenate([p for r in range(2) for p in [v[..., r, :]] * DT_REP + [pad]], axis=-1)

    wdt = dir_lanes(w_in_c[:, D_INNER + conv_dim:].reshape(d, 2, SSM_HEADS))
    dtb = dir_lanes(ssm_dt_bias[0]).reshape(1, 2 * LANES)
    a_row = dir_lanes(-jnp.exp(ssm_A_log[0].astype(F32)) * math.log2(math.e)).reshape(1, 2 * LANES)
    selx = _sel_matrix(SSM_HEAD_DIM)
    sell = _sel_matrix(SSD_CHUNK)
    scw = ssm_conv_w[0].T
    scb = ssm_conv_b[0].reshape(1, conv_dim)
    dskip = jnp.repeat(ssm_D[0], SSM_HEAD_DIM).reshape(1, D_INNER)
    snw = ssm_norm_w[0].reshape(1, D_INNER)
    w_out_c = ssm_w_out[0].astype(BF16)

    tm_s = _tile(t, 256)
    tm_sc = _tile(lc, 256)
    _, xs_c, bm_c, cm_c, dt_c = _ssm_in(ctx, mod_c[1][0], mod_c[1][1], nw(1, 0), wz, wx, wdt, scw, scb, dtb, tm_sc)
    hf0, hb0 = _ssd_states(xs_c, dt_c, bm_c, cm_c, a_row, selx, sell)
    z_l, xs_l, bm_l, cm_l, dt_l = _ssm_in(x, mod_l[1][0], mod_l[1][1], nw(1, 0), wz, wx, wdt, scw, scb, dtb, tm_s)
    cps = 2 if t % (2 * SSD_CHUNK) == 0 else 1
    yf = _ssd_fwd(xs_l, dt_l, bm_l, cm_l, a_row, selx, sell, hf0, cps)
    x = _ssd_bwd_out(xs_l, dt_l, bm_l, cm_l, a_row, selx, sell, hb0, yf, z_l, x, mod_l[1][2], dskip, snw, w_out_c,
                     nw(1, 1), cps)
    x = ffn(x, mod_l[1], 1, tm_l)
    return x
```

```python
import functools
import math

import jax
import jax.numpy as jnp
from jax import lax
from jax.experimental import pallas as pl
from jax.experimental.pallas import tpu as pltpu

F32 = jnp.float32
BF16 = jnp.bfloat16

EPS = 1e-6
GRID_W = 64
ROPE_THETA = 10000.0

HEAD_DIM = 128
N_Q_HEADS = 6
N_KV_HEADS = 2
Q_PER_KV = N_Q_HEADS // N_KV_HEADS
POOL_WINDOWS = (2, 4, 8, 16)
POOL_GROUP_DIM = 64
POOL_DIM = 256
ATTN_DIM = N_Q_HEADS * HEAD_DIM
KV_DIM = N_KV_HEADS * HEAD_DIM

SSM_HEAD_DIM = 64
SSM_HEADS = 32
SSM_GROUPS = 4
HEADS_PER_GROUP = 8
D_STATE = 128
D_INNER = SSM_HEADS * SSM_HEAD_DIM
GROUP_W = HEADS_PER_GROUP * SSM_HEAD_DIM
BC_DIM = SSM_GROUPS * D_STATE
SSD_CHUNK = 128

HALO = 8
LANES = 128
VMEM_LIMIT = 56 * 1024 * 1024


def _params(*sem):
    return pltpu.CompilerParams(dimension_semantics=sem, vmem_limit_bytes=VMEM_LIMIT)


def _const_spec(shape):
    nd = len(shape)
    return pl.BlockSpec(shape, lambda *_: (0,) * nd, pipeline_mode=pl.Buffered(1))


def _row_spec(c, shared):
    if shared:
        return pl.BlockSpec((1, 1, c), lambda b, i: (0, 0, 0))
    return pl.BlockSpec((1, 1, c), lambda b, i: (b, 0, 0))


def _halo_specs(tm, c, t):
    nb = tm // HALO
    last = t // HALO - 1
    main = pl.BlockSpec((1, tm, c), lambda b, i: (b, i, 0))
    prev = pl.BlockSpec((1, HALO, c), lambda b, i: (b, jnp.maximum(i * nb - 1, 0), 0))
    nxt = pl.BlockSpec((1, HALO, c), lambda b, i: (b, jnp.minimum((i + 1) * nb, last), 0))
    return main, prev, nxt


def _rms(xf, w):
    ms = jnp.mean(xf * xf, axis=-1, keepdims=True)
    return xf * lax.rsqrt(ms + EPS) * w


def _silu(x):
    return x * jax.nn.sigmoid(x)


def _dot(a, b):
    return jnp.dot(a, b, preferred_element_type=F32)


def _ada_kernel(c_ref, w_ref, b_ref, o_ref):
    s = _silu(c_ref[...])
    o_ref[0] = _dot(s.astype(BF16), w_ref[0].astype(BF16)) + b_ref[0]


def _adaln(cond, ada_w, ada_b):
    depth, d, n = ada_w.shape
    rows = cond.shape[0]
    tn = 1536
    return pl.pallas_call(
        _ada_kernel,
        grid=(depth, n // tn),
        in_specs=[
            pl.BlockSpec((rows, d), lambda l, j: (0, 0)),
            pl.BlockSpec((1, d, tn), lambda l, j: (l, 0, j)),
            pl.BlockSpec((1, 1, tn), lambda l, j: (l, 0, j)),
        ],
        out_specs=pl.BlockSpec((1, rows, tn), lambda l, j: (l, 0, j)),
        out_shape=jax.ShapeDtypeStruct((depth, rows, n), F32),
        compiler_params=_params("parallel", "parallel"),
        name="adaln",
    )(cond, ada_w, ada_b.reshape(depth, 1, n))


def _attn_in_kernel(*refs, rope):
    if rope:
        (x_ref, sh_ref, sc_ref, nw_ref, w_ref, qg_ref, kg_ref, cos_ref, sin_ref,
         a_ref, q_ref, k_ref, v_ref) = refs
        cos2, sin2 = cos_ref[...], sin_ref[...]
    else:
        (x_ref, sh_ref, sc_ref, nw_ref, w_ref, qg_ref, kg_ref,
         a_ref, q_ref, k_ref, v_ref) = refs
    h = _rms(x_ref[0], nw_ref[...]) * (1.0 + sc_ref[0]) + sh_ref[0]
    u = _dot(h.astype(BF16), w_ref[...])
    a_ref[0] = u[:, :POOL_DIM]

    def norm_rope(t, gain):
        y = _rms(t, gain)
        if rope:
            y = y * cos2 + pltpu.roll(y, HEAD_DIM // 2, 1) * sin2
        return y.astype(BF16)

    off = POOL_DIM
    for hh in range(N_Q_HEADS):
        q_ref[0, :, hh * HEAD_DIM:(hh + 1) * HEAD_DIM] = norm_rope(
            u[:, off + hh * HEAD_DIM: off + (hh + 1) * HEAD_DIM], qg_ref[...])
    off += ATTN_DIM
    for hh in range(N_KV_HEADS):
        k_ref[0, :, hh * HEAD_DIM:(hh + 1) * HEAD_DIM] = norm_rope(
            u[:, off + hh * HEAD_DIM: off + (hh + 1) * HEAD_DIM], kg_ref[...])
    off += KV_DIM
    v_ref[0] = u[:, off:].astype(BF16)


def _attn_in(x, shift, scale, nw, w_in, qg, kg, cos2, sin2, tm):
    b, t, d = x.shape
    shared = shift.shape[0] == 1
    rope = cos2 is not None
    width = w_in.shape[1]
    in_specs = [
        pl.BlockSpec((1, tm, d), lambda bb, i: (bb, i, 0)),
        _row_spec(d, shared), _row_spec(d, shared),
        _const_spec((1, d)), _const_spec((d, width)),
        _const_spec((1, HEAD_DIM)), _const_spec((1, HEAD_DIM)),
    ]
    args = [x, shift, scale, nw, w_in, qg, kg]
    if rope:
        in_specs += [pl.BlockSpec((tm, HEAD_DIM), lambda bb, i: (i, 0))] * 2
        args += [cos2, sin2]
    outs = [(POOL_DIM, F32), (ATTN_DIM, BF16), (KV_DIM, BF16), (KV_DIM, BF16)]
    return pl.pallas_call(
        functools.partial(_attn_in_kernel, rope=rope),
        grid=(b, t // tm),
        in_specs=in_specs,
        out_specs=[pl.BlockSpec((1, tm, c), lambda bb, i: (bb, i, 0)) for c, _ in outs],
        out_shape=[jax.ShapeDtypeStruct((b, t, c), dt) for c, dt in outs],
        compiler_params=_params("parallel", "parallel"),
        name="attn_in",
    )(*args)


SUM_ROWS = 16
SOFTMAX_C = (HEAD_DIM ** -0.5) * math.log2(math.e)
ATTN_UNROLL = 8


def _attn_kernel(q_ref, k_ref, vt_ref, o_ref, m_sc, acc_sc, s0_sc, mc0_sc, s1_sc, mc1_sc, *, n, tq):
    slots = ((s0_sc, mc0_sc), (s1_sc, mc1_sc))

    q = q_ref[0]
    qs = jnp.concatenate([q[:, g * HEAD_DIM:(g + 1) * HEAD_DIM] for g in range(Q_PER_KV)], axis=0)
    m_sc[...] = jnp.full(m_sc.shape, -jnp.inf, F32)
    acc_sc[...] = jnp.zeros(acc_sc.shape, F32)

    def produce(j, slot, g):
        s_sc, mc_sc = slots[slot]
        cols = slice(g * tq, (g + 1) * tq)
        s = lax.dot_general(k_ref[0, 0, j], qs[cols], (((1,), (1,)), ((), ())),
                            preferred_element_type=F32)
        s_sc[:, cols] = s
        mc_sc[:, cols] = jnp.max(s, axis=0, keepdims=True)

    def consume(j, slot, g):
        s_sc, mc_sc = slots[slot]
        cols = slice(g * tq, (g + 1) * tq)
        m_prev = m_sc[:, cols]
        m_new = jnp.maximum(m_prev, mc_sc[:, cols])
        alpha = jnp.exp2(m_prev - m_new)
        p = jnp.exp2(s_sc[:, cols] - m_new).astype(BF16)
        acc_sc[:, cols] = alpha * acc_sc[:, cols] + _dot(vt_ref[0, 0, j], p)
        m_sc[:, cols] = m_new

    def step(jp, sp, jc, sc):
        for g in range(Q_PER_KV):
            if jp is not None:
                produce(jp, sp, g)
            if jc is not None:
                consume(jc, sc, g)

    step(0, 0, None, None)

    def body(jj, carry):
        for u in range(ATTN_UNROLL):
            step(ATTN_UNROLL * jj + u + 1, (u + 1) % 2, ATTN_UNROLL * jj + u, u % 2)
        return carry

    trips = (n - 1) // ATTN_UNROLL
    lax.fori_loop(0, trips, body, 0)
    for j in range(trips * ATTN_UNROLL, n - 1):
        step(j + 1, (j + 1) % 2, j, j % 2)
    step(None, None, n - 1, (n - 1) % 2)

    acc = acc_sc[...]
    o_t = acc[:HEAD_DIM] / acc[HEAD_DIM:HEAD_DIM + 1]
    for g in range(Q_PER_KV):
        o_ref[0, :, g * HEAD_DIM:(g + 1) * HEAD_DIM] = o_t[:, g * tq:(g + 1) * tq].T.astype(BF16)


def _attention(q, k, v, tq, tk):
    b, t, _ = q.shape
    l = k.shape[1]
    gw = Q_PER_KV * HEAD_DIM
    tk = min(tk, l)
    pad = (-l) % tk
    assert pad <= l
    n = (l + pad) // tk
    k = jnp.concatenate([k, k[:, :pad]], axis=1)
    v = jnp.concatenate([v, jnp.zeros((b, pad, KV_DIM), BF16)], axis=1)
    live = (jnp.arange(l + pad) < l).astype(BF16).reshape(n, 1, tk)
    kc = k.reshape(b, n, tk, N_KV_HEADS, HEAD_DIM).transpose(0, 3, 1, 2, 4)
    vt = v.reshape(b, n, tk, N_KV_HEADS, HEAD_DIM).transpose(0, 3, 1, 4, 2)
    vt = jnp.concatenate([vt, jnp.broadcast_to(live, (b, N_KV_HEADS, n, SUM_ROWS, tk))], axis=3)
    m = Q_PER_KV * tq
    return pl.pallas_call(
        functools.partial(_attn_kernel, n=n, tq=tq),
        grid=(b, N_KV_HEADS, t // tq),
        in_specs=[
            pl.BlockSpec((1, tq, gw), lambda bb, g, i: (bb, i, g)),
            pl.BlockSpec((1, 1) + kc.shape[2:], lambda bb, g, i: (bb, g, 0, 0, 0)),
            pl.BlockSpec((1, 1) + vt.shape[2:], lambda bb, g, i: (bb, g, 0, 0, 0)),
        ],
        out_specs=pl.BlockSpec((1, tq, gw), lambda bb, g, i: (bb, i, g)),
        out_shape=jax.ShapeDtypeStruct((b, t, ATTN_DIM), BF16),
        scratch_shapes=[pltpu.VMEM((1, m), F32), pltpu.VMEM((HEAD_DIM + SUM_ROWS, m), F32)]
        + [pltpu.VMEM((tk, m), F32), pltpu.VMEM((1, m), F32)] * 2,
        compiler_params=_params("parallel", "parallel", "arbitrary"),
        name="attention",
    )(q, kc, vt)


def _attn_out_kernel(x_ref, g_ref, nw_ref, a_ref, ap_ref, an_ref, o_ref, pw_ref, ps_ref, wa_ref, wo_ref,
                     out_ref, abuf, *, tm, t):
    i = pl.program_id(1)
    last = pl.num_programs(1) - 1
    abuf[0:HALO] = jnp.where(i == 0, 0.0, ap_ref[0])
    abuf[HALO:HALO + tm] = a_ref[0]
    abuf[HALO + tm:] = jnp.where(i == last, 0.0, an_ref[0])

    def sh(d):
        return abuf[HALO + d:HALO + d + tm, :]

    x0 = sh(0)
    s2 = sh(-1) + x0
    s4 = s2 + sh(-2) + sh(1)
    s8 = s4 + sh(-4) + sh(-3) + sh(2) + sh(3)
    s16 = s8 + sh(-8) + sh(-7) + sh(-6) + sh(-5) + sh(4) + sh(5) + sh(6) + sh(7)
    pos = i * tm + lax.broadcasted_iota(jnp.int32, (tm, 1), 0)
    lane = lax.broadcasted_iota(jnp.int32, (1, POOL_DIM), 1)

    def mean(s, w):
        left = w // 2
        right = w - 1 - left
        cnt = jnp.minimum(pos + right, t - 1) - jnp.maximum(pos - left, 0) + 1
        return s / cnt.astype(F32)

    pm = jnp.where(lane < 64, mean(s2, 2),
                   jnp.where(lane < 128, mean(s4, 4),
                             jnp.where(lane < 192, mean(s8, 8), mean(s16, 16))))
    p = pm - x0
    py = _dot(p.astype(BF16), pw_ref[...]) * ps_ref[...]
    y = _dot(py.astype(BF16), wa_ref[...]) + _dot(o_ref[0], wo_ref[...])
    out_ref[0] = x_ref[0] + g_ref[0] * _rms(y, nw_ref[...])


def _attn_out(x, gate, nw, a, o, pool_bd, pool_scale, w_a, w_o, tm):
    b, t, d = x.shape
    shared = gate.shape[0] == 1
    am, ap, an = _halo_specs(tm, POOL_DIM, t)
    return pl.pallas_call(
        functools.partial(_attn_out_kernel, tm=tm, t=t),
        grid=(b, t // tm),
        in_specs=[
            pl.BlockSpec((1, tm, d), lambda bb, i: (bb, i, 0)),
            _row_spec(d, shared), _const_spec((1, d)),
            am, ap, an,
            pl.BlockSpec((1, tm, ATTN_DIM), lambda bb, i: (bb, i, 0)),
            _const_spec((POOL_DIM, POOL_DIM)), _const_spec((1, POOL_DIM)),
            _const_spec((POOL_DIM, d)), _const_spec((ATTN_DIM, d)),
        ],
        out_specs=pl.BlockSpec((1, tm, d), lambda bb, i: (bb, i, 0)),
        out_shape=jax.ShapeDtypeStruct((b, t, d), F32),
        scratch_shapes=[pltpu.VMEM((tm + 2 * HALO, POOL_DIM), F32)],
        compiler_params=_params("parallel", "parallel"),
        name="attn_out",
    )(x, gate, nw, a, a, a, o, pool_bd, pool_scale, w_a, w_o)


FFN_UNROLL = 2


def _ffn_kernel(x_ref, xp_ref, xn_ref, sh_ref, sc_ref, g_ref, nw2_ref, nw3_ref,
                wup_ref, cw_ref, cb_ref, wdn_ref, out_ref, xe_sc, h_sc, u0_sc, u1_sc, acc_sc, *, tm, fc, nch):
    i = pl.program_id(1)
    last = pl.num_programs(1) - 1
    xe_sc[0:HALO] = xp_ref[0]
    xe_sc[HALO:HALO + tm] = x_ref[0]
    xe_sc[HALO + tm:] = xn_ref[0]
    h = _rms(xe_sc[...], nw2_ref[...]) * (1.0 + sc_ref[0]) + sh_ref[0]
    h_sc[...] = h.astype(BF16)
    acc_sc[...] = jnp.zeros(acc_sc.shape, F32)
    keep_top = jnp.where(i == 0, 0.0, 1.0)
    keep_bot = jnp.where(i == last, 0.0, 1.0)
    slots = (u0_sc, u1_sc)

    def produce(j, slot):
        u = _dot(h_sc[...], wup_ref[j])
        u_sc = slots[slot]
        u_sc[0:HALO] = u[0:HALO] * keep_top
        u_sc[HALO:HALO + tm] = u[HALO:HALO + tm]
        u_sc[HALO + tm:] = u[HALO + tm:] * keep_bot

    def consume(j, slot):
        u_sc = slots[slot]
        cw = cw_ref[j]
        cv = cb_ref[j]
        for k in range(3):
            cv = cv + u_sc[HALO - 1 + k:HALO - 1 + k + tm, :] * cw[k:k + 1]
        g = _silu(cv[:, fc:]) * cv[:, :fc]
        acc_sc[...] += _dot(g.astype(BF16), wdn_ref[j])

    produce(0, 0)

    def body(jj, carry):
        for u in range(FFN_UNROLL):
            produce(FFN_UNROLL * jj + u + 1, (u + 1) % 2)
            consume(FFN_UNROLL * jj + u, u % 2)
        return carry

    trips = (nch - 1) // FFN_UNROLL
    if trips > 1:
        lax.fori_loop(0, trips, body, 0)
    elif trips == 1:
        body(0, 0)
    for j in range(trips * FFN_UNROLL, nch - 1):
        produce(j + 1, (j + 1) % 2)
        consume(j, j % 2)
    consume(nch - 1, (nch - 1) % 2)
    out_ref[0] = x_ref[0] + g_ref[0] * _rms(acc_sc[...], nw3_ref[...])


def _ffn(x, shift, scale, gate, nw2, nw3, wup, cw, cb, wdn, tm):
    b, t, d = x.shape
    shared = shift.shape[0] == 1
    nch, _, fc2 = wup.shape
    fc = fc2 // 2
    xm, xp, xn = _halo_specs(tm, d, t)
    return pl.pallas_call(
        functools.partial(_ffn_kernel, tm=tm, fc=fc, nch=nch),
        grid=(b, t // tm),
        in_specs=[
            xm, xp, xn,
            _row_spec(d, shared), _row_spec(d, shared), _row_spec(d, shared),
            _const_spec((1, d)), _const_spec((1, d)),
            _const_spec(wup.shape), _const_spec(cw.shape), _const_spec(cb.shape), _const_spec(wdn.shape),
        ],
        out_specs=pl.BlockSpec((1, tm, d), lambda bb, i: (bb, i, 0)),
        out_shape=jax.ShapeDtypeStruct((b, t, d), F32),
        scratch_shapes=[
            pltpu.VMEM((tm + 2 * HALO, d), F32),
            pltpu.VMEM((tm + 2 * HALO, d), BF16),
            pltpu.VMEM((tm + 2 * HALO, fc2), F32),
            pltpu.VMEM((tm + 2 * HALO, fc2), F32),
            pltpu.VMEM((tm, d), F32),
        ],
        compiler_params=_params("parallel", "parallel"),
        name="conv_ffn",
    )(x, x, x, shift, scale, gate, nw2, nw3, wup, cw, cb, wdn)


def _ssm_in_kernel(x_ref, xp_ref, xn_ref, sh_ref, sc_ref, nw_ref, wz_ref, wx_ref, wdt_ref,
                   cw_ref, cb_ref, dtb_ref, z_ref, xs_ref, bm_ref, cm_ref, dt_ref, xe_sc, u0_sc, u1_sc, *, tm, cc):
    u_scs = (u0_sc, u1_sc)
    i = pl.program_id(1)
    last = pl.num_programs(1) - 1
    xe_sc[0:HALO] = xp_ref[0]
    xe_sc[HALO:HALO + tm] = x_ref[0]
    xe_sc[HALO + tm:] = xn_ref[0]
    hf = _rms(xe_sc[...], nw_ref[...]) * (1.0 + sc_ref[0]) + sh_ref[0]
    he = hf.astype(BF16)
    hm = hf[HALO:HALO + tm].astype(BF16)
    keep_top = jnp.where(i == 0, 0.0, 1.0)
    keep_bot = jnp.where(i == last, 0.0, 1.0)

    for j in range(D_INNER // cc):
        z_ref[0, :, j * cc:(j + 1) * cc] = _dot(hm, wz_ref[:, j * cc:(j + 1) * cc])

    dt_raw = _dot(hm, wdt_ref[...]) + dtb_ref[...]
    dt_ref[0] = jnp.maximum(dt_raw, 0.0) + jnp.log1p(jnp.exp(-jnp.abs(dt_raw)))

    conv_dim = D_INNER + 2 * BC_DIM
    for j in range(conv_dim // cc):
        lo = j * cc
        u_sc = u_scs[j % 2]
        u = _dot(he, wx_ref[:, lo:lo + cc])
        u_sc[0:HALO] = u[0:HALO] * keep_top
        u_sc[HALO:HALO + tm] = u[HALO:HALO + tm]
        u_sc[HALO + tm:] = u[HALO + tm:] * keep_bot

        cv = cb_ref[:, lo:lo + cc]
        for k in range(4):
            cv = cv + u_sc[HALO - 2 + k:HALO - 2 + k + tm, :] * cw_ref[k:k + 1, lo:lo + cc]
        act = _silu(cv)
        if lo < D_INNER:
            xs_ref[0, :, lo:lo + cc] = act
        elif lo < D_INNER + BC_DIM:
            bm_ref[0, :, lo - D_INNER:lo - D_INNER + cc] = act.astype(BF16)
        else:
            cm_ref[0, :, lo - D_INNER - BC_DIM:lo - D_INNER - BC_DIM + cc] = act.astype(BF16)


def _ssm_in(x, shift, scale, nw, wz, wx, wdt, cw, cb, dtb, tm):
    b, t, d = x.shape
    shared = shift.shape[0] == 1
    cc = 512
    xm, xp, xn = _halo_specs(tm, d, t)
    outs = [(D_INNER, F32), (D_INNER, F32), (BC_DIM, BF16), (BC_DIM, BF16), (2 * LANES, F32)]
    return pl.pallas_call(
        functools.partial(_ssm_in_kernel, tm=tm, cc=cc),
        grid=(b, t // tm),
        in_specs=[
            xm, xp, xn, _row_spec(d, shared), _row_spec(d, shared), _const_spec((1, d)),
            _const_spec(wz.shape), _const_spec(wx.shape), _const_spec(wdt.shape),
            _const_spec(cw.shape), _const_spec(cb.shape), _const_spec(dtb.shape),
        ],
        out_specs=[pl.BlockSpec((1, tm, c), lambda bb, i: (bb, i, 0)) for c, _ in outs],
        out_shape=[jax.ShapeDtypeStruct((b, t, c), dt) for c, dt in outs],
        scratch_shapes=[pltpu.VMEM((tm + 2 * HALO, d), F32)] + [pltpu.VMEM((tm + 2 * HALO, cc), F32)] * 2,
        compiler_params=_params("parallel", "parallel"),
        name="ssm_in",
    )(x, x, x, shift, scale, nw, wz, wx, wdt, cw, cb, dtb)


DT_REP = 3


def _split3(v, lane):
    p1 = v.astype(BF16).astype(F32)
    r1 = v - p1
    p2 = r1.astype(BF16).astype(F32)
    r2 = r1 - p2
    packed = jnp.where(lane < 32, p1, jnp.where(lane < 64, p2, jnp.where(lane < 96, r2, 0.0)))
    return packed.astype(BF16)


def _sel_matrix(width):
    r = jnp.arange(LANES)[:, None]
    c = jnp.arange(SSM_HEADS * width)[None, :]
    return ((c // width == r % SSM_HEADS) & (r < DT_REP * SSM_HEADS)).astype(BF16)


def _ssd_chunk(xs_ref, dt_ref, bm_ref, cm_ref, rows, a_row, selx_ref, h_sc, reverse, y_store):
    q = SSD_CHUNK
    row = lax.broadcasted_iota(jnp.int32, (q, q), 0)
    col = lax.broadcasted_iota(jnp.int32, (q, q), 1)
    mask = (col >= row) if reverse else (col <= row)
    tri = jnp.where(mask, 1.0, 0.0).astype(BF16)
    lane = lax.broadcasted_iota(jnp.int32, (1, LANES), 1)
    lane_lo = lane < SSM_HEAD_DIM
    end = 0 if reverse else q - 1

    dt = dt_ref[0, rows]
    a = dt * a_row
    a1 = a.astype(BF16)
    r1 = a - a1.astype(F32)
    a2 = r1.astype(BF16)
    a3 = (r1 - a2.astype(F32)).astype(BF16)
    acum = _dot(tri, a1) + _dot(tri, a2) + _dot(tri, a3)
    dt_p = _split3(dt, lane)
    if y_store is not None:
        acum_t = acum.T

    for g in range(SSM_GROUPS):
        gs = slice(g * D_STATE, (g + 1) * D_STATE)
        b_g = bm_ref[0, rows, gs]
        c_g = cm_ref[0, rows, gs]
        h_g = h_sc[g]
        dt_x = _dot(dt_p, selx_ref[:, g * GROUP_W:(g + 1) * GROUP_W])
        if y_store is not None:
            cb = lax.dot_general(c_g, b_g, (((1,), (1,)), ((), ())), preferred_element_type=F32)
            y_inter = _dot(c_g, h_g.astype(BF16))
        xd_parts = []
        etot_parts = []
        for j in range(HEADS_PER_GROUP // 2):
            hd = g * HEADS_PER_GROUP + 2 * j
            pair = hd // 2
            pl_ = slice(j * LANES, (j + 1) * LANES)
            d2 = [jnp.broadcast_to(acum[:, hd + k:hd + k + 1], (q, q)) for k in range(2)]
            dmix = jnp.where(lane_lo, d2[0], d2[1])
            tot = dmix[end:end + 1]
            xdt = xs_ref[0, rows, pair * LANES:(pair + 1) * LANES] * dt_x[:, pl_]
            xd_parts.append((xdt * jnp.exp2(tot - dmix)).astype(BF16))
            etot_parts.append(jnp.exp2(tot))
            if y_store is not None:
                lm = []
                for k in range(2):
                    seg = d2[k] - acum_t[hd + k:hd + k + 1, :]
                    lm.append(cb * jnp.exp2(jnp.where(mask, seg, -jnp.inf)))
                m2 = jnp.concatenate(lm, axis=1).astype(BF16)
                xb = xdt.astype(BF16)
                zero = jnp.zeros_like(xb)
                rhs = jnp.concatenate([jnp.where(lane_lo, xb, zero), jnp.where(lane_lo, zero, xb)], axis=0)
                y_intra = _dot(m2, rhs)
                y_store(pair, y_intra + y_inter[:, pl_] * jnp.exp2(dmix))
        xd = jnp.concatenate(xd_parts, axis=1)
        upd = lax.dot_general(b_g, xd, (((0,), (0,)), ((), ())), preferred_element_type=F32)
        h_sc[g] = jnp.concatenate(etot_parts, axis=1) * h_g + upd


def _ssd_state_kernel(xs_ref, dtf_ref, dtb_ref, bm_ref, cm_ref, af_ref, ab_ref, selx_ref,
                      hf_ref, hb_ref, h_sc, *, nchunks):
    q = SSD_CHUNK
    for reverse, dt_ref, a_ref, out_ref in ((False, dtf_ref, af_ref, hf_ref), (True, dtb_ref, ab_ref, hb_ref)):
        h_sc[...] = jnp.zeros(h_sc.shape, F32)
        order = range(nchunks - 1, -1, -1) if reverse else range(nchunks)
        for ci in order:
            rows = slice(ci * q, (ci + 1) * q)
            _ssd_chunk(xs_ref, dt_ref, bm_ref, cm_ref, rows, a_ref[...], selx_ref, h_sc, reverse, None)
        out_ref[0] = h_sc[...]


def _dir_spec(rev):
    return pl.BlockSpec((1, LANES), lambda *_: (0, rev))


def _ssd_states(xs, dt, bm, cm, a_row, selx):
    b, t, _ = xs.shape
    st = jax.ShapeDtypeStruct((b, SSM_GROUPS, D_STATE, GROUP_W), F32)
    full = lambda c, j=0: pl.BlockSpec((1, t, c), lambda bb: (bb, 0, j))
    st_spec = pl.BlockSpec((1, SSM_GROUPS, D_STATE, GROUP_W), lambda bb: (bb, 0, 0, 0))
    return pl.pallas_call(
        functools.partial(_ssd_state_kernel, nchunks=t // SSD_CHUNK),
        grid=(b,),
        in_specs=[full(D_INNER), full(LANES, 0), full(LANES, 1), full(BC_DIM), full(BC_DIM),
                  _dir_spec(0), _dir_spec(1), _const_spec(selx.shape)],
        out_specs=[st_spec, st_spec],
        out_shape=[st, st],
        scratch_shapes=[pltpu.VMEM((SSM_GROUPS, D_STATE, GROUP_W), F32)],
        compiler_params=_params("parallel"),
        name="ssd_ctx_states",
    )(xs, dt, dt, bm, cm, a_row, a_row, selx)


def _ssd_fwd_kernel(xs_ref, dt_ref, bm_ref, cm_ref, a_ref, selx_ref, h0_ref, y_ref, h_sc, *, cps):
    q = SSD_CHUNK

    @pl.when(pl.program_id(1) == 0)
    def _():
        h_sc[...] = h0_ref[0]

    for ci in range(cps):
        rows = slice(ci * q, (ci + 1) * q)

        def store(pair, val, rows=rows):
            y_ref[0, rows, pair * LANES:(pair + 1) * LANES] = val

        _ssd_chunk(xs_ref, dt_ref, bm_ref, cm_ref, rows, a_ref[...], selx_ref, h_sc, False, store)


def _ssd_fwd(xs, dt, bm, cm, a_row, selx, h0, cps):
    b, t, _ = xs.shape
    tm = cps * SSD_CHUNK
    blk = lambda c: pl.BlockSpec((1, tm, c), lambda bb, i: (bb, i, 0))
    return pl.pallas_call(
        functools.partial(_ssd_fwd_kernel, cps=cps),
        grid=(b, t // tm),
        in_specs=[blk(D_INNER), blk(LANES), blk(BC_DIM), blk(BC_DIM), _dir_spec(0),
                  _const_spec(selx.shape),
                  pl.BlockSpec((1, SSM_GROUPS, D_STATE, GROUP_W), lambda bb, i: (bb, 0, 0, 0))],
        out_specs=blk(D_INNER),
        out_shape=jax.ShapeDtypeStruct((b, t, D_INNER), F32),
        scratch_shapes=[pltpu.VMEM((SSM_GROUPS, D_STATE, GROUP_W), F32)],
        compiler_params=_params("parallel", "arbitrary"),
        name="ssd_fwd",
    )(xs, dt, bm, cm, a_row, selx, h0)


def _ssd_bwd_out_kernel(xs_ref, dt_ref, bm_ref, cm_ref, a_ref, selx_ref, h0_ref, yf_ref, z_ref, x_ref,
                        g_ref, dsk_ref, snw_ref, wo_ref, nw_ref, out_ref, h_sc, yb_sc, *, cps):
    q = SSD_CHUNK

    @pl.when(pl.program_id(1) == 0)
    def _():
        h_sc[...] = h0_ref[0]

    for ci in range(cps - 1, -1, -1):
        rows = slice(ci * q, (ci + 1) * q)

        def store(pair, val, rows=rows):
            yb_sc[rows, pair * LANES:(pair + 1) * LANES] = val

        _ssd_chunk(xs_ref, dt_ref, bm_ref, cm_ref, rows, a_ref[...], selx_ref, h_sc, True, store)

    y = yf_ref[0] + yb_sc[...] + dsk_ref[...] * xs_ref[0]
    yn = _rms(y * _silu(z_ref[0]), snw_ref[...])
    o = _dot(yn.astype(BF16), wo_ref[...])
    out_ref[0] = x_ref[0] + g_ref[0] * _rms(o, nw_ref[...])


def _ssd_bwd_out(xs, dt, bm, cm, a_row, selx, h0, yf, z, x, gate, dskip, snw, w_out, nw, cps):
    b, t, d = x.shape
    tm = cps * SSD_CHUNK
    nblk = t // tm
    blk = lambda c, j=0: pl.BlockSpec((1, tm, c), lambda bb, i: (bb, nblk - 1 - i, j))
    return pl.pallas_call(
        functools.partial(_ssd_bwd_out_kernel, cps=cps),
        grid=(b, nblk),
        in_specs=[blk(D_INNER), blk(LANES, 1), blk(BC_DIM), blk(BC_DIM), _dir_spec(1),
                  _const_spec(selx.shape),
                  pl.BlockSpec((1, SSM_GROUPS, D_STATE, GROUP_W), lambda bb, i: (bb, 0, 0, 0)),
                  blk(D_INNER), blk(D_INNER), blk(d),
                  pl.BlockSpec((1, 1, d), lambda bb, i: (bb, 0, 0)),
                  _const_spec((1, D_INNER)), _const_spec((1, D_INNER)), _const_spec((D_INNER, d)),
                  _const_spec((1, d))],
        out_specs=blk(d),
        out_shape=jax.ShapeDtypeStruct((b, t, d), F32),
        scratch_shapes=[pltpu.VMEM((SSM_GROUPS, D_STATE, GROUP_W), F32), pltpu.VMEM((tm, D_INNER), F32)],
        compiler_params=_params("parallel", "arbitrary"),
        name="ssd_bwd_out",
    )(xs, dt, bm, cm, a_row, selx, h0, yf, z, x, gate, dskip, snw, w_out, nw)


def _rope_tables(t):
    rows = t // GRID_W
    row = jnp.repeat(jnp.arange(rows, dtype=F32), GRID_W)
    col = jnp.tile(jnp.arange(GRID_W, dtype=F32), rows)
    half = HEAD_DIM // 2
    inv_freq = ROPE_THETA ** (-jnp.arange(0, half, 2, dtype=F32) / half)
    ang = jnp.concatenate([row[:, None] * inv_freq, col[:, None] * inv_freq], axis=-1)
    cos, sin = jnp.cos(ang), jnp.sin(ang)
    return jnp.concatenate([cos, cos], axis=-1), jnp.concatenate([-sin, sin], axis=-1)


def _tile(t, pref):
    return pref if t % pref == 0 else t


def kernel(x, c, ctx, c_ctx, ada_w, ada_b, norm_w, attn_w_in, pool_w, pool_scale, q_gain, k_gain, attn_w_out,
           ssm_w_in, ssm_conv_w, ssm_conv_b, ssm_A_log, ssm_dt_bias, ssm_D, ssm_norm_w, ssm_w_out,
           ffn_w_up, ffn_conv_w, ffn_conv_b, ffn_w_down):
    bsz, t, d = x.shape
    lc = ctx.shape[1]
    depth = ada_w.shape[0]
    assert depth == 2 and attn_w_in.shape[0] == 1 and ssm_w_in.shape[0] == 1
    d_ff = ffn_w_down.shape[1]

    rows = -(-(bsz + 1) // HALO) * HALO
    cond = jnp.zeros((rows, d), F32).at[:bsz].set(c).at[bsz].set(c_ctx)
    ada = _adaln(cond, ada_w, ada_b)
    mod_l = [[ada[i, :bsz, k * d:(k + 1) * d].reshape(bsz, 1, d) for k in range(6)] for i in range(depth)]
    mod_c = [[ada[i, bsz:bsz + 1, k * d:(k + 1) * d].reshape(1, 1, d) for k in range(6)] for i in range(depth)]
    nw = lambda i, k: norm_w[i, k].reshape(1, d)

    fc = 256
    nch = d_ff // fc

    def ffn_weights(i):
        wu = ffn_w_up[i].astype(BF16)
        wup = jnp.concatenate([wu[:, :d_ff].reshape(d, nch, fc), wu[:, d_ff:].reshape(d, nch, fc)], axis=-1)
        wup = wup.transpose(1, 0, 2)
        cwt = ffn_conv_w[i].T
        cw = jnp.concatenate([cwt[:, :d_ff].reshape(3, nch, fc), cwt[:, d_ff:].reshape(3, nch, fc)], axis=-1)
        cw = cw.transpose(1, 0, 2)
        cb = jnp.concatenate([ffn_conv_b[i][:d_ff].reshape(nch, 1, fc), ffn_conv_b[i][d_ff:].reshape(nch, 1, fc)],
                             axis=-1)
        wdn = ffn_w_down[i].astype(BF16).reshape(nch, fc, d)
        return wup, cw, cb, wdn

    def ffn(xx, mods, i, tm):
        wup, cw, cb, wdn = ffn_weights(i)
        return _ffn(xx, mods[3], mods[4], mods[5], nw(i, 2), nw(i, 3), wup, cw, cb, wdn, tm)

    perm = jnp.concatenate([jnp.arange(0, HEAD_DIM, 2), jnp.arange(1, HEAD_DIM, 2)])
    cols = jnp.arange(attn_w_in.shape[2])
    qk_lo, qk_hi = POOL_DIM, POOL_DIM + ATTN_DIM + KV_DIM
    qk_cols = (qk_lo + (jnp.arange(qk_hi - qk_lo) // HEAD_DIM) * HEAD_DIM
               + perm[jnp.arange(qk_hi - qk_lo) % HEAD_DIM])
    cols = cols.at[qk_lo:qk_hi].set(qk_cols)
    w_in_a = attn_w_in[0][:, cols].astype(BF16)
    qg = (q_gain[0][perm] * SOFTMAX_C).reshape(1, HEAD_DIM)
    kg = k_gain[0][perm].reshape(1, HEAD_DIM)
    cos2, sin2 = _rope_tables(t)
    eye = jnp.eye(len(POOL_WINDOWS), dtype=F32)
    pool_bd = (eye[:, None, :, None] * pool_w[0][:, :, None, :]).reshape(POOL_DIM, POOL_DIM).astype(BF16)
    ps = pool_scale[0].reshape(1, POOL_DIM)
    w_out_a = attn_w_out[0][:POOL_DIM].astype(BF16)
    w_out_o = attn_w_out[0][POOL_DIM:].astype(BF16)

    tm_l = _tile(t, 512)
    tm_c = _tile(lc, 512)
    a_c, q_c, k_c, v_c = _attn_in(ctx, mod_c[0][0], mod_c[0][1], nw(0, 0), w_in_a, qg, kg, None, None, tm_c)
    a_l, q_l, k_l, v_l = _attn_in(x, mod_l[0][0], mod_l[0][1], nw(0, 0), w_in_a, qg, kg, cos2, sin2, tm_l)
    k_all = jnp.concatenate([k_c, k_l], axis=1)
    v_all = jnp.concatenate([v_c, v_l], axis=1)
    o_l = _attention(q_l, k_all, v_all, _tile(t, 256), 512)
    o_c = _attention(q_c, k_c, v_c, _tile(lc, 256), 512)
    x = _attn_out(x, mod_l[0][2], nw(0, 1), a_l, o_l, pool_bd, ps, w_out_a, w_out_o, tm_l)
    ctx = _attn_out(ctx, mod_c[0][2], nw(0, 1), a_c, o_c, pool_bd, ps, w_out_a, w_out_o, tm_c)
    x = ffn(x, mod_l[0], 0, tm_l)
    ctx = ffn(ctx, mod_c[0], 0, tm_c)

    w_in_c = ssm_w_in[0].astype(BF16)
    conv_dim = D_INNER + 2 * BC_DIM
    wz = w_in_c[:, :D_INNER]
    wx = w_in_c[:, D_INNER:D_INNER + conv_dim]

    def dir_lanes(v):
        pad = jnp.zeros(v.shape[:-2] + (LANES - DT_REP * SSM_HEADS,), v.dtype)
        return jnp.concatenate([p for r in range(2) for p in [v[..., r, :]] * DT_REP + [pad]], axis=-1)

    wdt = dir_lanes(w_in_c[:, D_INNER + conv_dim:].reshape(d, 2, SSM_HEADS))
    dtb = dir_lanes(ssm_dt_bias[0]).reshape(1, 2 * LANES)
    a_row = dir_lanes(-jnp.exp(ssm_A_log[0].astype(F32)) * math.log2(math.e)).reshape(1, 2 * LANES)
    selx = _sel_matrix(SSM_HEAD_DIM)
    scw = ssm_conv_w[0].T
    scb = ssm_conv_b[0].reshape(1, conv_dim)
    dskip = jnp.repeat(ssm_D[0], SSM_HEAD_DIM).reshape(1, D_INNER)
    snw = ssm_norm_w[0].reshape(1, D_INNER)
    w_out_c = ssm_w_out[0].astype(BF16)

    tm_s = _tile(t, 256)
    tm_sc = _tile(lc, 256)
    _, xs_c, bm_c, cm_c, dt_c = _ssm_in(ctx, mod_c[1][0], mod_c[1][1], nw(1, 0), wz, wx, wdt, scw, scb, dtb, tm_sc)
    hf0, hb0 = _ssd_states(xs_c, dt_c, bm_c, cm_c, a_row, selx)
    z_l, xs_l, bm_l, cm_l, dt_l = _ssm_in(x, mod_l[1][0], mod_l[1][1], nw(1, 0), wz, wx, wdt, scw, scb, dtb, tm_s)
    cps = lambda want: max(c for c in (1, 2, 4) if c <= want and t % (c * SSD_CHUNK) == 0)
    yf = _ssd_fwd(xs_l, dt_l, bm_l, cm_l, a_row, selx, hf0, cps(2))
    x = _ssd_bwd_out(xs_l, dt_l, bm_l, cm_l, a_row, selx, hb0, yf, z_l, x, mod_l[1][2], dskip, snw, w_out_c,
                     nw(1, 1), cps(4))
    x = ffn(x, mod_l[1], 1, tm_l)
    return x
```

```python
import functools
import math

import jax
import jax.numpy as jnp
from jax import lax
from jax.experimental import pallas as pl
from jax.experimental.pallas import tpu as pltpu

F32 = jnp.float32
BF16 = jnp.bfloat16

EPS = 1e-6
GRID_W = 64
ROPE_THETA = 10000.0

HEAD_DIM = 128
N_Q_HEADS = 6
N_KV_HEADS = 2
Q_PER_KV = N_Q_HEADS // N_KV_HEADS
POOL_WINDOWS = (2, 4, 8, 16)
POOL_GROUP_DIM = 64
POOL_DIM = 256
ATTN_DIM = N_Q_HEADS * HEAD_DIM
KV_DIM = N_KV_HEADS * HEAD_DIM

SSM_HEAD_DIM = 64
SSM_HEADS = 32
SSM_GROUPS = 4
HEADS_PER_GROUP = 8
D_STATE = 128
D_INNER = SSM_HEADS * SSM_HEAD_DIM
GROUP_W = HEADS_PER_GROUP * SSM_HEAD_DIM
BC_DIM = SSM_GROUPS * D_STATE
SSD_CHUNK = 128

HALO = 8
LANES = 128
VMEM_LIMIT = 56 * 1024 * 1024


def _params(*sem):
    return pltpu.CompilerParams(dimension_semantics=sem, vmem_limit_bytes=VMEM_LIMIT)


def _const_spec(shape):
    nd = len(shape)
    return pl.BlockSpec(shape, lambda *_: (0,) * nd, pipeline_mode=pl.Buffered(1))


def _row_spec(c, shared):
    if shared:
        return pl.BlockSpec((1, 1, c), lambda b, i: (0, 0, 0))
    return pl.BlockSpec((1, 1, c), lambda b, i: (b, 0, 0))


def _halo_specs(tm, c, t):
    nb = tm // HALO
    last = t // HALO - 1
    main = pl.BlockSpec((1, tm, c), lambda b, i: (b, i, 0))
    prev = pl.BlockSpec((1, HALO, c), lambda b, i: (b, jnp.maximum(i * nb - 1, 0), 0))
    nxt = pl.BlockSpec((1, HALO, c), lambda b, i: (b, jnp.minimum((i + 1) * nb, last), 0))
    return main, prev, nxt


def _rms(xf, w):
    ms = jnp.mean(xf * xf, axis=-1, keepdims=True)
    return xf * lax.rsqrt(ms + EPS) * w


def _silu(x):
    return x * jax.nn.sigmoid(x)


def _dot(a, b):
    return jnp.dot(a, b, preferred_element_type=F32)


def _ada_kernel(c_ref, w_ref, b_ref, o_ref):
    s = _silu(c_ref[...])
    o_ref[0] = _dot(s.astype(BF16), w_ref[0].astype(BF16)) + b_ref[0]


def _adaln(cond, ada_w, ada_b):
    depth, d, n = ada_w.shape
    rows = cond.shape[0]
    tn = 1536
    return pl.pallas_call(
        _ada_kernel,
        grid=(depth, n // tn),
        in_specs=[
            pl.BlockSpec((rows, d), lambda l, j: (0, 0)),
            pl.BlockSpec((1, d, tn), lambda l, j: (l, 0, j)),
            pl.BlockSpec((1, 1, tn), lambda l, j: (l, 0, j)),
        ],
        out_specs=pl.BlockSpec((1, rows, tn), lambda l, j: (l, 0, j)),
        out_shape=jax.ShapeDtypeStruct((depth, rows, n), F32),
        compiler_params=_params("parallel", "parallel"),
        name="adaln",
    )(cond, ada_w, ada_b.reshape(depth, 1, n))


def _attn_in_kernel(*refs, rope):
    if rope:
        (x_ref, sh_ref, sc_ref, nw_ref, w_ref, qg_ref, kg_ref, cos_ref, sin_ref,
         a_ref, q_ref, k_ref, v_ref) = refs
        cos2, sin2 = cos_ref[...], sin_ref[...]
    else:
        (x_ref, sh_ref, sc_ref, nw_ref, w_ref, qg_ref, kg_ref,
         a_ref, q_ref, k_ref, v_ref) = refs
    h = _rms(x_ref[0], nw_ref[...]) * (1.0 + sc_ref[0]) + sh_ref[0]
    u = _dot(h.astype(BF16), w_ref[...])
    a_ref[0] = u[:, :POOL_DIM]

    def norm_rope(t, gain):
        y = _rms(t, gain)
        if rope:
            y = y * cos2 + pltpu.roll(y, HEAD_DIM // 2, 1) * sin2
        return y.astype(BF16)

    off = POOL_DIM
    for hh in range(N_Q_HEADS):
        q_ref[0, :, hh * HEAD_DIM:(hh + 1) * HEAD_DIM] = norm_rope(
            u[:, off + hh * HEAD_DIM: off + (hh + 1) * HEAD_DIM], qg_ref[...])
    off += ATTN_DIM
    for hh in range(N_KV_HEADS):
        k_ref[0, :, hh * HEAD_DIM:(hh + 1) * HEAD_DIM] = norm_rope(
            u[:, off + hh * HEAD_DIM: off + (hh + 1) * HEAD_DIM], kg_ref[...])
    off += KV_DIM
    v_ref[0] = u[:, off:].astype(BF16)


def _attn_in(x, shift, scale, nw, w_in, qg, kg, cos2, sin2, tm):
    b, t, d = x.shape
    shared = shift.shape[0] == 1
    rope = cos2 is not None
    width = w_in.shape[1]
    in_specs = [
        pl.BlockSpec((1, tm, d), lambda bb, i: (bb, i, 0)),
        _row_spec(d, shared), _row_spec(d, shared),
        _const_spec((1, d)), _const_spec((d, width)),
        _const_spec((1, HEAD_DIM)), _const_spec((1, HEAD_DIM)),
    ]
    args = [x, shift, scale, nw, w_in, qg, kg]
    if rope:
        in_specs += [pl.BlockSpec((tm, HEAD_DIM), lambda bb, i: (i, 0))] * 2
        args += [cos2, sin2]
    outs = [(POOL_DIM, F32), (ATTN_DIM, BF16), (KV_DIM, BF16), (KV_DIM, BF16)]
    return pl.pallas_call(
        functools.partial(_attn_in_kernel, rope=rope),
        grid=(b, t // tm),
        in_specs=in_specs,
        out_specs=[pl.BlockSpec((1, tm, c), lambda bb, i: (bb, i, 0)) for c, _ in outs],
        out_shape=[jax.ShapeDtypeStruct((b, t, c), dt) for c, dt in outs],
        compiler_params=_params("parallel", "parallel"),
        name="attn_in",
    )(*args)


SUM_ROWS = 16
SOFTMAX_C = (HEAD_DIM ** -0.5) * math.log2(math.e)
ATTN_UNROLL = 8


def _attn_kernel(q_ref, k_ref, vt_ref, o_ref, m_sc, acc_sc, s0_sc, mc0_sc, s1_sc, mc1_sc, *, n, tq):
    slots = ((s0_sc, mc0_sc), (s1_sc, mc1_sc))

    q = q_ref[0]
    qs = jnp.concatenate([q[:, g * HEAD_DIM:(g + 1) * HEAD_DIM] for g in range(Q_PER_KV)], axis=0)
    m_sc[...] = jnp.full(m_sc.shape, -jnp.inf, F32)
    acc_sc[...] = jnp.zeros(acc_sc.shape, F32)

    def produce(j, slot, g):
        s_sc, mc_sc = slots[slot]
        cols = slice(g * tq, (g + 1) * tq)
        s = lax.dot_general(k_ref[0, 0, j], qs[cols], (((1,), (1,)), ((), ())),
                            preferred_element_type=F32)
        s_sc[:, cols] = s
        mc_sc[:, cols] = jnp.max(s, axis=0, keepdims=True)

    def consume(j, slot, g):
        s_sc, mc_sc = slots[slot]
        cols = slice(g * tq, (g + 1) * tq)
        m_prev = m_sc[:, cols]
        m_new = jnp.maximum(m_prev, mc_sc[:, cols])
        alpha = jnp.exp2(m_prev - m_new)
        p = jnp.exp2(s_sc[:, cols] - m_new).astype(BF16)
        acc_sc[:, cols] = alpha * acc_sc[:, cols] + _dot(vt_ref[0, 0, j], p)
        m_sc[:, cols] = m_new

    def step(jp, sp, jc, sc):
        for g in range(Q_PER_KV):
            if jp is not None:
                produce(jp, sp, g)
            if jc is not None:
                consume(jc, sc, g)

    step(0, 0, None, None)

    def body(jj, carry):
        for u in range(ATTN_UNROLL):
            step(ATTN_UNROLL * jj + u + 1, (u + 1) % 2, ATTN_UNROLL * jj + u, u % 2)
        return carry

    trips = (n - 1) // ATTN_UNROLL
    lax.fori_loop(0, trips, body, 0)
    for j in range(trips * ATTN_UNROLL, n - 1):
        step(j + 1, (j + 1) % 2, j, j % 2)
    step(None, None, n - 1, (n - 1) % 2)

    acc = acc_sc[...]
    o_t = acc[:HEAD_DIM] / acc[HEAD_DIM:HEAD_DIM + 1]
    for g in range(Q_PER_KV):
        o_ref[0, :, g * HEAD_DIM:(g + 1) * HEAD_DIM] = o_t[:, g * tq:(g + 1) * tq].T.astype(BF16)


def _attention(q, k, v, tq, tk):
    b, t, _ = q.shape
    l = k.shape[1]
    gw = Q_PER_KV * HEAD_DIM
    tk = min(tk, l)
    pad = (-l) % tk
    assert pad <= l
    n = (l + pad) // tk
    k = jnp.concatenate([k, k[:, :pad]], axis=1)
    v = jnp.concatenate([v, jnp.zeros((b, pad, KV_DIM), BF16)], axis=1)
    live = (jnp.arange(l + pad) < l).astype(BF16).reshape(n, 1, tk)
    kc = k.reshape(b, n, tk, N_KV_HEADS, HEAD_DIM).transpose(0, 3, 1, 2, 4)
    vt = v.reshape(b, n, tk, N_KV_HEADS, HEAD_DIM).transpose(0, 3, 1, 4, 2)
    vt = jnp.concatenate([vt, jnp.broadcast_to(live, (b, N_KV_HEADS, n, SUM_ROWS, tk))], axis=3)
    m = Q_PER_KV * tq
    return pl.pallas_call(
        functools.partial(_attn_kernel, n=n, tq=tq),
        grid=(b, N_KV_HEADS, t // tq),
        in_specs=[
            pl.BlockSpec((1, tq, gw), lambda bb, g, i: (bb, i, g)),
            pl.BlockSpec((1, 1) + kc.shape[2:], lambda bb, g, i: (bb, g, 0, 0, 0)),
            pl.BlockSpec((1, 1) + vt.shape[2:], lambda bb, g, i: (bb, g, 0, 0, 0)),
        ],
        out_specs=pl.BlockSpec((1, tq, gw), lambda bb, g, i: (bb, i, g)),
        out_shape=jax.ShapeDtypeStruct((b, t, ATTN_DIM), BF16),
        scratch_shapes=[pltpu.VMEM((1, m), F32), pltpu.VMEM((HEAD_DIM + SUM_ROWS, m), F32)]
        + [pltpu.VMEM((tk, m), F32), pltpu.VMEM((1, m), F32)] * 2,
        compiler_params=_params("parallel", "parallel", "arbitrary"),
        name="attention",
    )(q, kc, vt)


def _attn_out_kernel(x_ref, g_ref, nw_ref, a_ref, ap_ref, an_ref, o_ref, pw_ref, ps_ref, wa_ref, wo_ref,
                     out_ref, abuf, *, tm, t):
    i = pl.program_id(1)
    last = pl.num_programs(1) - 1
    abuf[0:HALO] = jnp.where(i == 0, 0.0, ap_ref[0])
    abuf[HALO:HALO + tm] = a_ref[0]
    abuf[HALO + tm:] = jnp.where(i == last, 0.0, an_ref[0])

    def sh(d):
        return abuf[HALO + d:HALO + d + tm, :]

    x0 = sh(0)
    s2 = sh(-1) + x0
    s4 = s2 + sh(-2) + sh(1)
    s8 = s4 + sh(-4) + sh(-3) + sh(2) + sh(3)
    s16 = s8 + sh(-8) + sh(-7) + sh(-6) + sh(-5) + sh(4) + sh(5) + sh(6) + sh(7)
    pos = i * tm + lax.broadcasted_iota(jnp.int32, (tm, 1), 0)
    lane = lax.broadcasted_iota(jnp.int32, (1, POOL_DIM), 1)

    def mean(s, w):
        left = w // 2
        right = w - 1 - left
        cnt = jnp.minimum(pos + right, t - 1) - jnp.maximum(pos - left, 0) + 1
        return s / cnt.astype(F32)

    pm = jnp.where(lane < 64, mean(s2, 2),
                   jnp.where(lane < 128, mean(s4, 4),
                             jnp.where(lane < 192, mean(s8, 8), mean(s16, 16))))
    p = pm - x0
    py = _dot(p.astype(BF16), pw_ref[...]) * ps_ref[...]
    y = _dot(py.astype(BF16), wa_ref[...]) + _dot(o_ref[0], wo_ref[...])
    out_ref[0] = x_ref[0] + g_ref[0] * _rms(y, nw_ref[...])


def _attn_out(x, gate, nw, a, o, pool_bd, pool_scale, w_a, w_o, tm):
    b, t, d = x.shape
    shared = gate.shape[0] == 1
    am, ap, an = _halo_specs(tm, POOL_DIM, t)
    return pl.pallas_call(
        functools.partial(_attn_out_kernel, tm=tm, t=t),
        grid=(b, t // tm),
        in_specs=[
            pl.BlockSpec((1, tm, d), lambda bb, i: (bb, i, 0)),
            _row_spec(d, shared), _const_spec((1, d)),
            am, ap, an,
            pl.BlockSpec((1, tm, ATTN_DIM), lambda bb, i: (bb, i, 0)),
            _const_spec((POOL_DIM, POOL_DIM)), _const_spec((1, POOL_DIM)),
            _const_spec((POOL_DIM, d)), _const_spec((ATTN_DIM, d)),
        ],
        out_specs=pl.BlockSpec((1, tm, d), lambda bb, i: (bb, i, 0)),
        out_shape=jax.ShapeDtypeStruct((b, t, d), F32),
        scratch_shapes=[pltpu.VMEM((tm + 2 * HALO, POOL_DIM), F32)],
        compiler_params=_params("parallel", "parallel"),
        name="attn_out",
    )(x, gate, nw, a, a, a, o, pool_bd, pool_scale, w_a, w_o)


def _stage_tile(xe_sc, xp_ref, x_ref, xn_ref, tm):
    for c in range(xe_sc.shape[0]):
        cols = slice(c * LANES, (c + 1) * LANES)
        xe_sc[c, 0:HALO] = xp_ref[0, :, cols]
        xe_sc[c, HALO:HALO + tm] = x_ref[0, :, cols]
        xe_sc[c, HALO + tm:] = xn_ref[0, :, cols]


def _permuted_rows(xe_sc):
    nslab, r, _ = xe_sc.shape
    nv = r // 8
    return jnp.concatenate(
        [jnp.concatenate([xe_sc[c, pl.ds(v, 8, stride=nv), :] for c in range(nslab)], axis=1) for v in range(nv)],
        axis=0)


def _unpermute_into(ye_sc, val):
    nslab, r, _ = ye_sc.shape
    nv = r // 8
    for c in range(nslab):
        for v in range(nv):
            ye_sc[c, pl.ds(v, 8, stride=nv), :] = val[8 * v:8 * v + 8, c * LANES:(c + 1) * LANES]


def _edge_masks(r, at_start, at_end):
    sub = lax.broadcasted_iota(jnp.int32, (8 * HALO, 1), 0) % 8
    top = jnp.where(jnp.logical_and(sub == 0, at_start), 0.0, 1.0)
    bot = jnp.where(jnp.logical_and(sub == 7, at_end), 0.0, 1.0)
    return top, bot


def _store_masked(u_sc, u, masks):
    r = u.shape[0]
    edge = 8 * HALO
    u_sc[0:edge] = u[0:edge] * masks[0]
    u_sc[edge:r - edge] = u[edge:r - edge]
    u_sc[r - edge:] = u[r - edge:] * masks[1]


FFN_UNROLL = 2


def _ffn_kernel(x_ref, xp_ref, xn_ref, sh_ref, sc_ref, g_ref, nw2_ref, nw3_ref,
                wup_ref, cw_ref, cb_ref, wdn_ref, out_ref, xe_sc, h_sc, u0_sc, u1_sc, acc_sc, *, tm, fc, nch):
    i = pl.program_id(1)
    last = pl.num_programs(1) - 1
    xe_sc[0:HALO] = xp_ref[0]
    xe_sc[HALO:HALO + tm] = x_ref[0]
    xe_sc[HALO + tm:] = xn_ref[0]
    h = _rms(xe_sc[...], nw2_ref[...]) * (1.0 + sc_ref[0]) + sh_ref[0]
    h_sc[...] = h.astype(BF16)
    acc_sc[...] = jnp.zeros(acc_sc.shape, F32)
    keep_top = jnp.where(i == 0, 0.0, 1.0)
    keep_bot = jnp.where(i == last, 0.0, 1.0)
    slots = (u0_sc, u1_sc)

    def produce(j, slot):
        u = _dot(h_sc[...], wup_ref[j])
        u_sc = slots[slot]
        u_sc[0:HALO] = u[0:HALO] * keep_top
        u_sc[HALO:HALO + tm] = u[HALO:HALO + tm]
        u_sc[HALO + tm:] = u[HALO + tm:] * keep_bot

    def consume(j, slot):
        u_sc = slots[slot]
        cw = cw_ref[j]
        cv = cb_ref[j]
        for k in range(3):
            cv = cv + u_sc[HALO - 1 + k:HALO - 1 + k + tm, :] * cw[k:k + 1]
        g = _silu(cv[:, fc:]) * cv[:, :fc]
        acc_sc[...] += _dot(g.astype(BF16), wdn_ref[j])

    produce(0, 0)

    def body(jj, carry):
        for u in range(FFN_UNROLL):
            produce(FFN_UNROLL * jj + u + 1, (u + 1) % 2)
            consume(FFN_UNROLL * jj + u, u % 2)
        return carry

    trips = (nch - 1) // FFN_UNROLL
    if trips > 1:
        lax.fori_loop(0, trips, body, 0)
    elif trips == 1:
        body(0, 0)
    for j in range(trips * FFN_UNROLL, nch - 1):
        produce(j + 1, (j + 1) % 2)
        consume(j, j % 2)
    consume(nch - 1, (nch - 1) % 2)
    out_ref[0] = x_ref[0] + g_ref[0] * _rms(acc_sc[...], nw3_ref[...])


def _ffn(x, shift, scale, gate, nw2, nw3, wup, cw, cb, wdn, tm):
    b, t, d = x.shape
    shared = shift.shape[0] == 1
    nch, _, fc2 = wup.shape
    fc = fc2 // 2
    xm, xp, xn = _halo_specs(tm, d, t)
    return pl.pallas_call(
        functools.partial(_ffn_kernel, tm=tm, fc=fc, nch=nch),
        grid=(b, t // tm),
        in_specs=[
            xm, xp, xn,
            _row_spec(d, shared), _row_spec(d, shared), _row_spec(d, shared),
            _const_spec((1, d)), _const_spec((1, d)),
            _const_spec(wup.shape), _const_spec(cw.shape), _const_spec(cb.shape), _const_spec(wdn.shape),
        ],
        out_specs=pl.BlockSpec((1, tm, d), lambda bb, i: (bb, i, 0)),
        out_shape=jax.ShapeDtypeStruct((b, t, d), F32),
        scratch_shapes=[
            pltpu.VMEM((tm + 2 * HALO, d), F32),
            pltpu.VMEM((tm + 2 * HALO, d), BF16),
            pltpu.VMEM((tm + 2 * HALO, fc2), F32),
            pltpu.VMEM((tm + 2 * HALO, fc2), F32),
            pltpu.VMEM((tm, d), F32),
        ],
        compiler_params=_params("parallel", "parallel"),
        name="conv_ffn",
    )(x, x, x, shift, scale, gate, nw2, nw3, wup, cw, cb, wdn)


def _ssm_in_kernel(x_ref, xp_ref, xn_ref, sh_ref, sc_ref, nw_ref, wz_ref, wx_ref, wdt_ref,
                   cw_ref, cb_ref, dtb_ref, z_ref, xs_ref, bm_ref, cm_ref, dt_ref,
                   xe_sc, u0_sc, u1_sc, y0_sc, y1_sc, *, tm, cc):
    u_scs = (u0_sc, u1_sc)
    y_scs = (y0_sc, y1_sc)
    i = pl.program_id(1)
    last = pl.num_programs(1) - 1
    r = tm + 2 * HALO
    _stage_tile(xe_sc, xp_ref, x_ref, xn_ref, tm)

    def mod(v):
        return (_rms(v, nw_ref[...]) * (1.0 + sc_ref[0]) + sh_ref[0]).astype(BF16)

    he = mod(_permuted_rows(xe_sc))
    hm = mod(x_ref[0])
    masks = _edge_masks(r, i == 0, i == last)

    for j in range(D_INNER // cc):
        z_ref[0, :, j * cc:(j + 1) * cc] = _dot(hm, wz_ref[:, j * cc:(j + 1) * cc])

    dt_raw = _dot(hm, wdt_ref[...]) + dtb_ref[...]
    dt_ref[0] = jnp.maximum(dt_raw, 0.0) + jnp.log1p(jnp.exp(-jnp.abs(dt_raw)))

    def down1(a):
        return pltpu.roll(a, 1, 0)

    conv_dim = D_INNER + 2 * BC_DIM
    for j in range(conv_dim // cc):
        lo = j * cc
        u_sc, y_sc = u_scs[j % 2], y_scs[j % 2]
        _store_masked(u_sc, _dot(he, wx_ref[:, lo:lo + cc]), masks)

        def conv(*taps, lo=lo):
            cv = cb_ref[:, lo:lo + cc]
            for k in range(4):
                cv = cv + taps[k] * cw_ref[k:k + 1, lo:lo + cc]
            return _silu(cv)

        act = jnp.concatenate([
            conv(down1(u_sc[r - 16:r - 8]), down1(u_sc[r - 8:r]), u_sc[0:8], u_sc[8:16]),
            conv(down1(u_sc[r - 8:r]), u_sc[0:8], u_sc[8:16], u_sc[16:24]),
            conv(u_sc[0:r - 24], u_sc[8:r - 16], u_sc[16:r - 8], u_sc[24:r]),
            conv(u_sc[r - 24:r - 16], u_sc[r - 16:r - 8], u_sc[r - 8:r], pltpu.roll(u_sc[0:8], 7, 0)),
        ], axis=0)
        _unpermute_into(y_sc, act)
        for c in range(cc // LANES):
            col = lo + c * LANES
            blk = y_sc[c, HALO:HALO + tm]
            if col < D_INNER:
                xs_ref[0, :, col:col + LANES] = blk
            elif col < D_INNER + BC_DIM:
                bm_ref[0, :, col - D_INNER:col - D_INNER + LANES] = blk.astype(BF16)
            else:
                cm_ref[0, :, col - D_INNER - BC_DIM:col - D_INNER - BC_DIM + LANES] = blk.astype(BF16)


def _ssm_in(x, shift, scale, nw, wz, wx, wdt, cw, cb, dtb, tm):
    b, t, d = x.shape
    shared = shift.shape[0] == 1
    cc = 512
    xm, xp, xn = _halo_specs(tm, d, t)
    outs = [(D_INNER, F32), (D_INNER, F32), (BC_DIM, BF16), (BC_DIM, BF16), (2 * LANES, F32)]
    return pl.pallas_call(
        functools.partial(_ssm_in_kernel, tm=tm, cc=cc),
        grid=(b, t // tm),
        in_specs=[
            xm, xp, xn, _row_spec(d, shared), _row_spec(d, shared), _const_spec((1, d)),
            _const_spec(wz.shape), _const_spec(wx.shape), _const_spec(wdt.shape),
            _const_spec(cw.shape), _const_spec(cb.shape), _const_spec(dtb.shape),
        ],
        out_specs=[pl.BlockSpec((1, tm, c), lambda bb, i: (bb, i, 0)) for c, _ in outs],
        out_shape=[jax.ShapeDtypeStruct((b, t, c), dt) for c, dt in outs],
        scratch_shapes=[pltpu.VMEM((d // LANES, tm + 2 * HALO, LANES), F32)]
        + [pltpu.VMEM((tm + 2 * HALO, cc), F32)] * 2 + [pltpu.VMEM((cc // LANES, tm + 2 * HALO, LANES), F32)] * 2,
        compiler_params=_params("parallel", "parallel"),
        name="ssm_in",
    )(x, x, x, shift, scale, nw, wz, wx, wdt, cw, cb, dtb)


DT_REP = 3


def _split3(v, lane):
    p1 = v.astype(BF16).astype(F32)
    r1 = v - p1
    p2 = r1.astype(BF16).astype(F32)
    r2 = r1 - p2
    packed = jnp.where(lane < 32, p1, jnp.where(lane < 64, p2, jnp.where(lane < 96, r2, 0.0)))
    return packed.astype(BF16)


def _sel_matrix(width):
    r = jnp.arange(LANES)[:, None]
    c = jnp.arange(SSM_HEADS * width)[None, :]
    return ((c // width == r % SSM_HEADS) & (r < DT_REP * SSM_HEADS)).astype(BF16)


def _ssd_chunk(xs_ref, dt_ref, bm_ref, cm_ref, rows, a_row, selx_ref, h_sc, reverse, y_store):
    q = SSD_CHUNK
    row = lax.broadcasted_iota(jnp.int32, (q, q), 0)
    col = lax.broadcasted_iota(jnp.int32, (q, q), 1)
    mask = (col >= row) if reverse else (col <= row)
    tri = jnp.where(mask, 1.0, 0.0).astype(BF16)
    lane = lax.broadcasted_iota(jnp.int32, (1, LANES), 1)
    lane_lo = lane < SSM_HEAD_DIM
    end = 0 if reverse else q - 1

    dt = dt_ref[0, rows]
    a = dt * a_row
    a1 = a.astype(BF16)
    r1 = a - a1.astype(F32)
    a2 = r1.astype(BF16)
    a3 = (r1 - a2.astype(F32)).astype(BF16)
    acum = _dot(tri, a1) + _dot(tri, a2) + _dot(tri, a3)
    dt_p = _split3(dt, lane)
    if y_store is not None:
        acum_t = acum.T

    for g in range(SSM_GROUPS):
        gs = slice(g * D_STATE, (g + 1) * D_STATE)
        b_g = bm_ref[0, rows, gs]
        c_g = cm_ref[0, rows, gs]
        h_g = h_sc[g]
        dt_x = _dot(dt_p, selx_ref[:, g * GROUP_W:(g + 1) * GROUP_W])
        if y_store is not None:
            cb = lax.dot_general(c_g, b_g, (((1,), (1,)), ((), ())), preferred_element_type=F32)
            y_inter = _dot(c_g, h_g.astype(BF16))
        xd_parts = []
        etot_parts = []
        for j in range(HEADS_PER_GROUP // 2):
            hd = g * HEADS_PER_GROUP + 2 * j
            pair = hd // 2
            pl_ = slice(j * LANES, (j + 1) * LANES)
            d2 = [jnp.broadcast_to(acum[:, hd + k:hd + k + 1], (q, q)) for k in range(2)]
            dmix = jnp.where(lane_lo, d2[0], d2[1])
            tot = dmix[end:end + 1]
            xdt = xs_ref[0, rows, pair * LANES:(pair + 1) * LANES] * dt_x[:, pl_]
            xd_parts.append((xdt * jnp.exp2(tot - dmix)).astype(BF16))
            etot_parts.append(jnp.exp2(tot))
            if y_store is not None:
                lm = []
                for k in range(2):
                    seg = d2[k] - acum_t[hd + k:hd + k + 1, :]
                    lm.append(cb * jnp.exp2(jnp.where(mask, seg, -jnp.inf)))
                m2 = jnp.concatenate(lm, axis=1).astype(BF16)
                xb = xdt.astype(BF16)
                zero = jnp.zeros_like(xb)
                rhs = jnp.concatenate([jnp.where(lane_lo, xb, zero), jnp.where(lane_lo, zero, xb)], axis=0)
                y_intra = _dot(m2, rhs)
                y_store(pair, y_intra + y_inter[:, pl_] * jnp.exp2(dmix))
        xd = jnp.concatenate(xd_parts, axis=1)
        upd = lax.dot_general(b_g, xd, (((0,), (0,)), ((), ())), preferred_element_type=F32)
        h_sc[g] = jnp.concatenate(etot_parts, axis=1) * h_g + upd


def _ssd_state_kernel(xs_ref, dtf_ref, dtb_ref, bm_ref, cm_ref, af_ref, ab_ref, selx_ref,
                      hf_ref, hb_ref, h_sc, *, nchunks):
    q = SSD_CHUNK
    for reverse, dt_ref, a_ref, out_ref in ((False, dtf_ref, af_ref, hf_ref), (True, dtb_ref, ab_ref, hb_ref)):
        h_sc[...] = jnp.zeros(h_sc.shape, F32)
        order = range(nchunks - 1, -1, -1) if reverse else range(nchunks)
        for ci in order:
            rows = slice(ci * q, (ci + 1) * q)
            _ssd_chunk(xs_ref, dt_ref, bm_ref, cm_ref, rows, a_ref[...], selx_ref, h_sc, reverse, None)
        out_ref[0] = h_sc[...]


def _dir_spec(rev):
    return pl.BlockSpec((1, LANES), lambda *_: (0, rev))


def _ssd_states(xs, dt, bm, cm, a_row, selx):
    b, t, _ = xs.shape
    st = jax.ShapeDtypeStruct((b, SSM_GROUPS, D_STATE, GROUP_W), F32)
    full = lambda c, j=0: pl.BlockSpec((1, t, c), lambda bb: (bb, 0, j))
    st_spec = pl.BlockSpec((1, SSM_GROUPS, D_STATE, GROUP_W), lambda bb: (bb, 0, 0, 0))
    return pl.pallas_call(
        functools.partial(_ssd_state_kernel, nchunks=t // SSD_CHUNK),
        grid=(b,),
        in_specs=[full(D_INNER), full(LANES, 0), full(LANES, 1), full(BC_DIM), full(BC_DIM),
                  _dir_spec(0), _dir_spec(1), _const_spec(selx.shape)],
        out_specs=[st_spec, st_spec],
        out_shape=[st, st],
        scratch_shapes=[pltpu.VMEM((SSM_GROUPS, D_STATE, GROUP_W), F32)],
        compiler_params=_params("parallel"),
        name="ssd_ctx_states",
    )(xs, dt, dt, bm, cm, a_row, a_row, selx)


def _ssd_fwd_kernel(xs_ref, dt_ref, bm_ref, cm_ref, a_ref, selx_ref, h0_ref, y_ref, h_sc, *, cps):
    q = SSD_CHUNK

    @pl.when(pl.program_id(1) == 0)
    def _():
        h_sc[...] = h0_ref[0]

    for ci in range(cps):
        rows = slice(ci * q, (ci + 1) * q)

        def store(pair, val, rows=rows):
            y_ref[0, rows, pair * LANES:(pair + 1) * LANES] = val

        _ssd_chunk(xs_ref, dt_ref, bm_ref, cm_ref, rows, a_ref[...], selx_ref, h_sc, False, store)


def _ssd_fwd(xs, dt, bm, cm, a_row, selx, h0, cps):
    b, t, _ = xs.shape
    tm = cps * SSD_CHUNK
    blk = lambda c: pl.BlockSpec((1, tm, c), lambda bb, i: (bb, i, 0))
    return pl.pallas_call(
        functools.partial(_ssd_fwd_kernel, cps=cps),
        grid=(b, t // tm),
        in_specs=[blk(D_INNER), blk(LANES), blk(BC_DIM), blk(BC_DIM), _dir_spec(0),
                  _const_spec(selx.shape),
                  pl.BlockSpec((1, SSM_GROUPS, D_STATE, GROUP_W), lambda bb, i: (bb, 0, 0, 0))],
        out_specs=blk(D_INNER),
        out_shape=jax.ShapeDtypeStruct((b, t, D_INNER), F32),
        scratch_shapes=[pltpu.VMEM((SSM_GROUPS, D_STATE, GROUP_W), F32)],
        compiler_params=_params("parallel", "arbitrary"),
        name="ssd_fwd",
    )(xs, dt, bm, cm, a_row, selx, h0)


def _ssd_bwd_out_kernel(xs_ref, dt_ref, bm_ref, cm_ref, a_ref, selx_ref, h0_ref, yf_ref, z_ref, x_ref,
                        g_ref, dsk_ref, snw_ref, wo_ref, nw_ref, out_ref, h_sc, yb_sc, *, cps):
    q = SSD_CHUNK

    @pl.when(pl.program_id(1) == 0)
    def _():
        h_sc[...] = h0_ref[0]

    for ci in range(cps - 1, -1, -1):
        rows = slice(ci * q, (ci + 1) * q)

        def store(pair, val, rows=rows):
            yb_sc[rows, pair * LANES:(pair + 1) * LANES] = val

        _ssd_chunk(xs_ref, dt_ref, bm_ref, cm_ref, rows, a_ref[...], selx_ref, h_sc, True, store)

    y = yf_ref[0] + yb_sc[...] + dsk_ref[...] * xs_ref[0]
    yn = _rms(y * _silu(z_ref[0]), snw_ref[...])
    o = _dot(yn.astype(BF16), wo_ref[...])
    out_ref[0] = x_ref[0] + g_ref[0] * _rms(o, nw_ref[...])


def _ssd_bwd_out(xs, dt, bm, cm, a_row, selx, h0, yf, z, x, gate, dskip, snw, w_out, nw, cps):
    b, t, d = x.shape
    tm = cps * SSD_CHUNK
    nblk = t // tm
    blk = lambda c, j=0: pl.BlockSpec((1, tm, c), lambda bb, i: (bb, nblk - 1 - i, j))
    return pl.pallas_call(
        functools.partial(_ssd_bwd_out_kernel, cps=cps),
        grid=(b, nblk),
        in_specs=[blk(D_INNER), blk(LANES, 1), blk(BC_DIM), blk(BC_DIM), _dir_spec(1),
                  _const_spec(selx.shape),
                  pl.BlockSpec((1, SSM_GROUPS, D_STATE, GROUP_W), lambda bb, i: (bb, 0, 0, 0)),
                  blk(D_INNER), blk(D_INNER), blk(d),
                  pl.BlockSpec((1, 1, d), lambda bb, i: (bb, 0, 0)),
                  _const_spec((1, D_INNER)), _const_spec((1, D_INNER)), _const_spec((D_INNER, d)),
                  _const_spec((1, d))],
        out_specs=blk(d),
        out_shape=jax.ShapeDtypeStruct((b, t, d), F32),
        scratch_shapes=[pltpu.VMEM((SSM_GROUPS, D_STATE, GROUP_W), F32), pltpu.VMEM((tm, D_INNER), F32)],
        compiler_params=_params("parallel", "arbitrary"),
        name="ssd_bwd_out",
    )(xs, dt, bm, cm, a_row, selx, h0, yf, z, x, gate, dskip, snw, w_out, nw)


def _rope_tables(t):
    rows = t // GRID_W
    row = jnp.repeat(jnp.arange(rows, dtype=F32), GRID_W)
    col = jnp.tile(jnp.arange(GRID_W, dtype=F32), rows)
    half = HEAD_DIM // 2
    inv_freq = ROPE_THETA ** (-jnp.arange(0, half, 2, dtype=F32) / half)
    ang = jnp.concatenate([row[:, None] * inv_freq, col[:, None] * inv_freq], axis=-1)
    cos, sin = jnp.cos(ang), jnp.sin(ang)
    return jnp.concatenate([cos, cos], axis=-1), jnp.concatenate([-sin, sin], axis=-1)


def _tile(t, pref):
    return pref if t % pref == 0 else t


def kernel(x, c, ctx, c_ctx, ada_w, ada_b, norm_w, attn_w_in, pool_w, pool_scale, q_gain, k_gain, attn_w_out,
           ssm_w_in, ssm_conv_w, ssm_conv_b, ssm_A_log, ssm_dt_bias, ssm_D, ssm_norm_w, ssm_w_out,
           ffn_w_up, ffn_conv_w, ffn_conv_b, ffn_w_down):
    bsz, t, d = x.shape
    lc = ctx.shape[1]
    depth = ada_w.shape[0]
    assert depth == 2 and attn_w_in.shape[0] == 1 and ssm_w_in.shape[0] == 1
    d_ff = ffn_w_down.shape[1]

    rows = -(-(bsz + 1) // HALO) * HALO
    cond = jnp.zeros((rows, d), F32).at[:bsz].set(c).at[bsz].set(c_ctx)
    ada = _adaln(cond, ada_w, ada_b)
    mod_l = [[ada[i, :bsz, k * d:(k + 1) * d].reshape(bsz, 1, d) for k in range(6)] for i in range(depth)]
    mod_c = [[ada[i, bsz:bsz + 1, k * d:(k + 1) * d].reshape(1, 1, d) for k in range(6)] for i in range(depth)]
    nw = lambda i, k: norm_w[i, k].reshape(1, d)

    fc = 256
    nch = d_ff // fc

    def ffn_weights(i):
        wu = ffn_w_up[i].astype(BF16)
        wup = jnp.concatenate([wu[:, :d_ff].reshape(d, nch, fc), wu[:, d_ff:].reshape(d, nch, fc)], axis=-1)
        wup = wup.transpose(1, 0, 2)
        cwt = ffn_conv_w[i].T
        cw = jnp.concatenate([cwt[:, :d_ff].reshape(3, nch, fc), cwt[:, d_ff:].reshape(3, nch, fc)], axis=-1)
        cw = cw.transpose(1, 0, 2)
        cb = jnp.concatenate([ffn_conv_b[i][:d_ff].reshape(nch, 1, fc), ffn_conv_b[i][d_ff:].reshape(nch, 1, fc)],
                             axis=-1)
        wdn = ffn_w_down[i].astype(BF16).reshape(nch, fc, d)
        return wup, cw, cb, wdn

    def ffn(xx, mods, i, tm):
        wup, cw, cb, wdn = ffn_weights(i)
        return _ffn(xx, mods[3], mods[4], mods[5], nw(i, 2), nw(i, 3), wup, cw, cb, wdn, tm)

    perm = jnp.concatenate([jnp.arange(0, HEAD_DIM, 2), jnp.arange(1, HEAD_DIM, 2)])
    cols = jnp.arange(attn_w_in.shape[2])
    qk_lo, qk_hi = POOL_DIM, POOL_DIM + ATTN_DIM + KV_DIM
    qk_cols = (qk_lo + (jnp.arange(qk_hi - qk_lo) // HEAD_DIM) * HEAD_DIM
               + perm[jnp.arange(qk_hi - qk_lo) % HEAD_DIM])
    cols = cols.at[qk_lo:qk_hi].set(qk_cols)
    w_in_a = attn_w_in[0][:, cols].astype(BF16)
    qg = (q_gain[0][perm] * SOFTMAX_C).reshape(1, HEAD_DIM)
    kg = k_gain[0][perm].reshape(1, HEAD_DIM)
    cos2, sin2 = _rope_tables(t)
    eye = jnp.eye(len(POOL_WINDOWS), dtype=F32)
    pool_bd = (eye[:, None, :, None] * pool_w[0][:, :, None, :]).reshape(POOL_DIM, POOL_DIM).astype(BF16)
    ps = pool_scale[0].reshape(1, POOL_DIM)
    w_out_a = attn_w_out[0][:POOL_DIM].astype(BF16)
    w_out_o = attn_w_out[0][POOL_DIM:].astype(BF16)

    tm_l = _tile(t, 512)
    tm_c = _tile(lc, 512)
    a_c, q_c, k_c, v_c = _attn_in(ctx, mod_c[0][0], mod_c[0][1], nw(0, 0), w_in_a, qg, kg, None, None, tm_c)
    a_l, q_l, k_l, v_l = _attn_in(x, mod_l[0][0], mod_l[0][1], nw(0, 0), w_in_a, qg, kg, cos2, sin2, tm_l)
    k_all = jnp.concatenate([k_c, k_l], axis=1)
    v_all = jnp.concatenate([v_c, v_l], axis=1)
    o_l = _attention(q_l, k_all, v_all, _tile(t, 256), 512)
    o_c = _attention(q_c, k_c, v_c, _tile(lc, 256), 512)
    x = _attn_out(x, mod_l[0][2], nw(0, 1), a_l, o_l, pool_bd, ps, w_out_a, w_out_o, tm_l)
    ctx = _attn_out(ctx, mod_c[0][2], nw(0, 1), a_c, o_c, pool_bd, ps, w_out_a, w_out_o, tm_c)
    x = ffn(x, mod_l[0], 0, tm_l)
    ctx = ffn(ctx, mod_c[0], 0, tm_c)

    w_in_c = ssm_w_in[0].astype(BF16)
    conv_dim = D_INNER + 2 * BC_DIM
    wz = w_in_c[:, :D_INNER]
    wx = w_in_c[:, D_INNER:D_INNER + conv_dim]

    def dir_lanes(v):
        pad = jnp.zeros(v.shape[:-2] + (LANES - DT_REP * SSM_HEADS,), v.dtype)
        return jnp.concatenate([p for r in range(2) for p in [v[..., r, :]] * DT_REP + [pad]], axis=-1)

    wdt = dir_lanes(w_in_c[:, D_INNER + conv_dim:].reshape(d, 2, SSM_HEADS))
    dtb = dir_lanes(ssm_dt_bias[0]).reshape(1, 2 * LANES)
    a_row = dir_lanes(-jnp.exp(ssm_A_log[0].astype(F32)) * math.log2(math.e)).reshape(1, 2 * LANES)
    selx = _sel_matrix(SSM_HEAD_DIM)
    scw = ssm_conv_w[0].T
    scb = ssm_conv_b[0].reshape(1, conv_dim)
    dskip = jnp.repeat(ssm_D[0], SSM_HEAD_DIM).reshape(1, D_INNER)
    snw = ssm_norm_w[0].reshape(1, D_INNER)
    w_out_c = ssm_w_out[0].astype(BF16)

    tm_s = _tile(t, 256)
    tm_sc = _tile(lc, 256)
    _, xs_c, bm_c, cm_c, dt_c = _ssm_in(ctx, mod_c[1][0], mod_c[1][1], nw(1, 0), wz, wx, wdt, scw, scb, dtb, tm_sc)
    hf0, hb0 = _ssd_states(xs_c, dt_c, bm_c, cm_c, a_row, selx)
    z_l, xs_l, bm_l, cm_l, dt_l = _ssm_in(x, mod_l[1][0], mod_l[1][1], nw(1, 0), wz, wx, wdt, scw, scb, dtb, tm_s)
    cps = lambda want: max(c for c in (1, 2, 4) if c <= want and t % (c * SSD_CHUNK) == 0)
    yf = _ssd_fwd(xs_l, dt_l, bm_l, cm_l, a_row, selx, hf0, cps(2))
    x = _ssd_bwd_out(xs_l, dt_l, bm_l, cm_l, a_row, selx, hb0, yf, z_l, x, mod_l[1][2], dskip, snw, w_out_c,
                     nw(1, 1), cps(4))
    x = ffn(x, mod_l[1], 1, tm_l)
    return x
```

```python
import functools
import math

import jax
import jax.numpy as jnp
from jax import lax
from jax.experimental import pallas as pl
from jax.experimental.pallas import tpu as pltpu

F32 = jnp.float32
BF16 = jnp.bfloat16

EPS = 1e-6
GRID_W = 64
ROPE_THETA = 10000.0

HEAD_DIM = 128
N_Q_HEADS = 6
N_KV_HEADS = 2
Q_PER_KV = N_Q_HEADS // N_KV_HEADS
POOL_WINDOWS = (2, 4, 8, 16)
POOL_GROUP_DIM = 64
POOL_DIM = 256
ATTN_DIM = N_Q_HEADS * HEAD_DIM
KV_DIM = N_KV_HEADS * HEAD_DIM

SSM_HEAD_DIM = 64
SSM_HEADS = 32
SSM_GROUPS = 4
HEADS_PER_GROUP = 8
D_STATE = 128
D_INNER = SSM_HEADS * SSM_HEAD_DIM
GROUP_W = HEADS_PER_GROUP * SSM_HEAD_DIM
BC_DIM = SSM_GROUPS * D_STATE
SSD_CHUNK = 128

HALO = 8
LANES = 128
VMEM_LIMIT = 56 * 1024 * 1024


def _params(*sem):
    return pltpu.CompilerParams(dimension_semantics=sem, vmem_limit_bytes=VMEM_LIMIT)


def _const_spec(shape):
    nd = len(shape)
    return pl.BlockSpec(shape, lambda *_: (0,) * nd, pipeline_mode=pl.Buffered(1))


def _row_spec(c, shared):
    if shared:
        return pl.BlockSpec((1, 1, c), lambda b, i: (0, 0, 0))
    return pl.BlockSpec((1, 1, c), lambda b, i: (b, 0, 0))


def _halo_specs(tm, c, t):
    nb = tm // HALO
    last = t // HALO - 1
    main = pl.BlockSpec((1, tm, c), lambda b, i: (b, i, 0))
    prev = pl.BlockSpec((1, HALO, c), lambda b, i: (b, jnp.maximum(i * nb - 1, 0), 0))
    nxt = pl.BlockSpec((1, HALO, c), lambda b, i: (b, jnp.minimum((i + 1) * nb, last), 0))
    return main, prev, nxt


def _rms(xf, w):
    ms = jnp.mean(xf * xf, axis=-1, keepdims=True)
    return xf * lax.rsqrt(ms + EPS) * w


def _silu(x):
    return x * jax.nn.sigmoid(x)


def _dot(a, b):
    return jnp.dot(a, b, preferred_element_type=F32)


def _ada_kernel(c_ref, w_ref, b_ref, o_ref):
    s = _silu(c_ref[...])
    o_ref[0] = _dot(s.astype(BF16), w_ref[0].astype(BF16)) + b_ref[0]


def _adaln(cond, ada_w, ada_b):
    depth, d, n = ada_w.shape
    rows = cond.shape[0]
    tn = 1536
    return pl.pallas_call(
        _ada_kernel,
        grid=(depth, n // tn),
        in_specs=[
            pl.BlockSpec((rows, d), lambda l, j: (0, 0)),
            pl.BlockSpec((1, d, tn), lambda l, j: (l, 0, j)),
            pl.BlockSpec((1, 1, tn), lambda l, j: (l, 0, j)),
        ],
        out_specs=pl.BlockSpec((1, rows, tn), lambda l, j: (l, 0, j)),
        out_shape=jax.ShapeDtypeStruct((depth, rows, n), F32),
        compiler_params=_params("parallel", "parallel"),
        name="adaln",
    )(cond, ada_w, ada_b.reshape(depth, 1, n))


def _attn_in_kernel(*refs, rope):
    if rope:
        (x_ref, sh_ref, sc_ref, nw_ref, w_ref, qg_ref, kg_ref, cos_ref, sin_ref,
         a_ref, q_ref, k_ref, v_ref) = refs
    else:
        (x_ref, sh_ref, sc_ref, nw_ref, w_ref, qg_ref, kg_ref,
         a_ref, q_ref, k_ref, v_ref) = refs
    tm = x_ref.shape[1]
    nblk = 2 if tm % (2 * 128) == 0 else 1
    rb = tm // nblk
    for blk in range(nblk):
        rows = slice(blk * rb, (blk + 1) * rb)
        h = _rms(x_ref[0, rows], nw_ref[...]) * (1.0 + sc_ref[0]) + sh_ref[0]
        u = _dot(h.astype(BF16), w_ref[...])
        a_ref[0, rows] = u[:, :POOL_DIM]

        def norm_rope(t, gain, rows=rows):
            y = _rms(t, gain)
            if rope:
                y = y * cos_ref[rows] + pltpu.roll(y, HEAD_DIM // 2, 1) * sin_ref[rows]
            return y.astype(BF16)

        off = POOL_DIM
        for hh in range(N_Q_HEADS):
            q_ref[0, rows, hh * HEAD_DIM:(hh + 1) * HEAD_DIM] = norm_rope(
                u[:, off + hh * HEAD_DIM: off + (hh + 1) * HEAD_DIM], qg_ref[...])
        off += ATTN_DIM
        for hh in range(N_KV_HEADS):
            k_ref[0, rows, hh * HEAD_DIM:(hh + 1) * HEAD_DIM] = norm_rope(
                u[:, off + hh * HEAD_DIM: off + (hh + 1) * HEAD_DIM], kg_ref[...])
        off += KV_DIM
        v_ref[0, rows] = u[:, off:].astype(BF16)


def _attn_in(x, shift, scale, nw, w_in, qg, kg, cos2, sin2, tm):
    b, t, d = x.shape
    shared = shift.shape[0] == 1
    rope = cos2 is not None
    width = w_in.shape[1]
    in_specs = [
        pl.BlockSpec((1, tm, d), lambda bb, i: (bb, i, 0)),
        _row_spec(d, shared), _row_spec(d, shared),
        _const_spec((1, d)), _const_spec((d, width)),
        _const_spec((1, HEAD_DIM)), _const_spec((1, HEAD_DIM)),
    ]
    args = [x, shift, scale, nw, w_in, qg, kg]
    if rope:
        in_specs += [pl.BlockSpec((tm, HEAD_DIM), lambda bb, i: (i, 0))] * 2
        args += [cos2, sin2]
    outs = [(POOL_DIM, F32), (ATTN_DIM, BF16), (KV_DIM, BF16), (KV_DIM, BF16)]
    return pl.pallas_call(
        functools.partial(_attn_in_kernel, rope=rope),
        grid=(b, t // tm),
        in_specs=in_specs,
        out_specs=[pl.BlockSpec((1, tm, c), lambda bb, i: (bb, i, 0)) for c, _ in outs],
        out_shape=[jax.ShapeDtypeStruct((b, t, c), dt) for c, dt in outs],
        compiler_params=_params("parallel", "parallel"),
        name="attn_in",
    )(*args)


SUM_ROWS = 16
SOFTMAX_C = (HEAD_DIM ** -0.5) * math.log2(math.e)
ATTN_UNROLL = 8


def _attn_kernel(q_ref, k_ref, vt_ref, o_ref, m_sc, acc_sc, s0_sc, mc0_sc, s1_sc, mc1_sc, *, n, tq):
    slots = ((s0_sc, mc0_sc), (s1_sc, mc1_sc))

    q = q_ref[0]
    qs = jnp.concatenate([q[:, g * HEAD_DIM:(g + 1) * HEAD_DIM] for g in range(Q_PER_KV)], axis=0)
    m_sc[...] = jnp.full(m_sc.shape, -jnp.inf, F32)
    acc_sc[...] = jnp.zeros(acc_sc.shape, F32)

    def produce(j, slot, g):
        s_sc, mc_sc = slots[slot]
        cols = slice(g * tq, (g + 1) * tq)
        s = lax.dot_general(k_ref[0, 0, j], qs[cols], (((1,), (1,)), ((), ())),
                            preferred_element_type=F32)
        s_sc[:, cols] = s
        mc_sc[:, cols] = jnp.max(s, axis=0, keepdims=True)

    def consume(j, slot, g):
        s_sc, mc_sc = slots[slot]
        cols = slice(g * tq, (g + 1) * tq)
        m_prev = m_sc[:, cols]
        m_new = jnp.maximum(m_prev, mc_sc[:, cols])
        alpha = jnp.exp2(m_prev - m_new)
        p = jnp.exp2(s_sc[:, cols] - m_new).astype(BF16)
        acc_sc[:, cols] = alpha * acc_sc[:, cols] + _dot(vt_ref[0, 0, j], p)
        m_sc[:, cols] = m_new

    def step(jp, sp, jc, sc):
        for g in range(Q_PER_KV):
            if jp is not None:
                produce(jp, sp, g)
            if jc is not None:
                consume(jc, sc, g)

    step(0, 0, None, None)

    def body(jj, carry):
        for u in range(ATTN_UNROLL):
            step(ATTN_UNROLL * jj + u + 1, (u + 1) % 2, ATTN_UNROLL * jj + u, u % 2)
        return carry

    trips = (n - 1) // ATTN_UNROLL
    lax.fori_loop(0, trips, body, 0)
    for j in range(trips * ATTN_UNROLL, n - 1):
        step(j + 1, (j + 1) % 2, j, j % 2)
    step(None, None, n - 1, (n - 1) % 2)

    acc = acc_sc[...]
    o_t = acc[:HEAD_DIM] / acc[HEAD_DIM:HEAD_DIM + 1]
    for g in range(Q_PER_KV):
        o_ref[0, :, g * HEAD_DIM:(g + 1) * HEAD_DIM] = o_t[:, g * tq:(g + 1) * tq].T.astype(BF16)


def _attention(q, k, v, tq, tk):
    b, t, _ = q.shape
    l = k.shape[1]
    gw = Q_PER_KV * HEAD_DIM
    tk = min(tk, l)
    pad = (-l) % tk
    assert pad <= l
    n = (l + pad) // tk
    k = jnp.concatenate([k, k[:, :pad]], axis=1)
    v = jnp.concatenate([v, jnp.zeros((b, pad, KV_DIM), BF16)], axis=1)
    live = (jnp.arange(l + pad) < l).astype(BF16).reshape(n, 1, tk)
    kc = k.reshape(b, n, tk, N_KV_HEADS, HEAD_DIM).transpose(0, 3, 1, 2, 4)
    vt = v.reshape(b, n, tk, N_KV_HEADS, HEAD_DIM).transpose(0, 3, 1, 4, 2)
    vt = jnp.concatenate([vt, jnp.broadcast_to(live, (b, N_KV_HEADS, n, SUM_ROWS, tk))], axis=3)
    m = Q_PER_KV * tq
    return pl.pallas_call(
        functools.partial(_attn_kernel, n=n, tq=tq),
        grid=(b, N_KV_HEADS, t // tq),
        in_specs=[
            pl.BlockSpec((1, tq, gw), lambda bb, g, i: (bb, i, g)),
            pl.BlockSpec((1, 1) + kc.shape[2:], lambda bb, g, i: (bb, g, 0, 0, 0)),
            pl.BlockSpec((1, 1) + vt.shape[2:], lambda bb, g, i: (bb, g, 0, 0, 0)),
        ],
        out_specs=pl.BlockSpec((1, tq, gw), lambda bb, g, i: (bb, i, g)),
        out_shape=jax.ShapeDtypeStruct((b, t, ATTN_DIM), BF16),
        scratch_shapes=[pltpu.VMEM((1, m), F32), pltpu.VMEM((HEAD_DIM + SUM_ROWS, m), F32)]
        + [pltpu.VMEM((tk, m), F32), pltpu.VMEM((1, m), F32)] * 2,
        compiler_params=_params("parallel", "parallel", "arbitrary"),
        name="attention",
    )(q, kc, vt)


def _attn_out_kernel(x_ref, g_ref, nw_ref, a_ref, ap_ref, an_ref, o_ref, pw_ref, ps_ref, wa_ref, wo_ref,
                     out_ref, abuf, *, tm, t):
    i = pl.program_id(1)
    last = pl.num_programs(1) - 1
    abuf[0:HALO] = jnp.where(i == 0, 0.0, ap_ref[0])
    abuf[HALO:HALO + tm] = a_ref[0]
    abuf[HALO + tm:] = jnp.where(i == last, 0.0, an_ref[0])

    lane = lax.broadcasted_iota(jnp.int32, (1, POOL_DIM), 1)
    nblk = 2 if tm % (2 * 128) == 0 else 1
    rb = tm // nblk
    for blk in range(nblk):
        r0 = blk * rb
        rows = slice(r0, r0 + rb)

        def sh(d, r0=r0):
            return abuf[HALO + d + r0:HALO + d + r0 + rb, :]

        x0 = sh(0)
        s2 = sh(-1) + x0
        s4 = s2 + sh(-2) + sh(1)
        s8 = s4 + sh(-4) + sh(-3) + sh(2) + sh(3)
        s16 = s8 + sh(-8) + sh(-7) + sh(-6) + sh(-5) + sh(4) + sh(5) + sh(6) + sh(7)
        pos = i * tm + r0 + lax.broadcasted_iota(jnp.int32, (rb, 1), 0)

        def mean(s, w, pos=pos):
            left = w // 2
            right = w - 1 - left
            cnt = jnp.minimum(pos + right, t - 1) - jnp.maximum(pos - left, 0) + 1
            return s / cnt.astype(F32)

        pm = jnp.where(lane < 64, mean(s2, 2),
                       jnp.where(lane < 128, mean(s4, 4),
                                 jnp.where(lane < 192, mean(s8, 8), mean(s16, 16))))
        p = pm - x0
        py = _dot(p.astype(BF16), pw_ref[...]) * ps_ref[...]
        y = _dot(py.astype(BF16), wa_ref[...]) + _dot(o_ref[0, rows], wo_ref[...])
        out_ref[0, rows] = x_ref[0, rows] + g_ref[0] * _rms(y, nw_ref[...])


def _attn_out(x, gate, nw, a, o, pool_bd, pool_scale, w_a, w_o, tm):
    b, t, d = x.shape
    shared = gate.shape[0] == 1
    am, ap, an = _halo_specs(tm, POOL_DIM, t)
    return pl.pallas_call(
        functools.partial(_attn_out_kernel, tm=tm, t=t),
        grid=(b, t // tm),
        in_specs=[
            pl.BlockSpec((1, tm, d), lambda bb, i: (bb, i, 0)),
            _row_spec(d, shared), _const_spec((1, d)),
            am, ap, an,
            pl.BlockSpec((1, tm, ATTN_DIM), lambda bb, i: (bb, i, 0)),
            _const_spec((POOL_DIM, POOL_DIM)), _const_spec((1, POOL_DIM)),
            _const_spec((POOL_DIM, d)), _const_spec((ATTN_DIM, d)),
        ],
        out_specs=pl.BlockSpec((1, tm, d), lambda bb, i: (bb, i, 0)),
        out_shape=jax.ShapeDtypeStruct((b, t, d), F32),
        scratch_shapes=[pltpu.VMEM((tm + 2 * HALO, POOL_DIM), F32)],
        compiler_params=_params("parallel", "parallel"),
        name="attn_out",
    )(x, gate, nw, a, a, a, o, pool_bd, pool_scale, w_a, w_o)


def _stage_tile(xe_sc, xp_ref, x_ref, xn_ref, tm):
    for c in range(xe_sc.shape[0]):
        cols = slice(c * LANES, (c + 1) * LANES)
        xe_sc[c, 0:HALO] = xp_ref[0, :, cols]
        xe_sc[c, HALO:HALO + tm] = x_ref[0, :, cols]
        xe_sc[c, HALO + tm:] = xn_ref[0, :, cols]


def _permuted_rows(xe_sc):
    nslab, r, _ = xe_sc.shape
    nv = r // 8
    return jnp.concatenate(
        [jnp.concatenate([xe_sc[c, pl.ds(v, 8, stride=nv), :] for c in range(nslab)], axis=1) for v in range(nv)],
        axis=0)


def _unpermute_into(ye_sc, val):
    nslab, r, _ = ye_sc.shape
    nv = r // 8
    for c in range(nslab):
        for v in range(nv):
            ye_sc[c, pl.ds(v, 8, stride=nv), :] = val[8 * v:8 * v + 8, c * LANES:(c + 1) * LANES]


def _edge_masks(r, at_start, at_end):
    sub = lax.broadcasted_iota(jnp.int32, (8 * HALO, 1), 0) % 8
    top = jnp.where(jnp.logical_and(sub == 0, at_start), 0.0, 1.0)
    bot = jnp.where(jnp.logical_and(sub == 7, at_end), 0.0, 1.0)
    return top, bot


def _store_masked(u_sc, u, masks):
    r = u.shape[0]
    edge = 8 * HALO
    u_sc[0:edge] = u[0:edge] * masks[0]
    u_sc[edge:r - edge] = u[edge:r - edge]
    u_sc[r - edge:] = u[r - edge:] * masks[1]


FFN_UNROLL = 2


def _ffn_kernel(x_ref, xp_ref, xn_ref, sh_ref, sc_ref, g_ref, nw2_ref, nw3_ref,
                wup_ref, cw_ref, cb_ref, wdn_ref, out_ref, xe_sc, h_sc, u0_sc, u1_sc, acc_sc, *, tm, fc, nch):
    i = pl.program_id(1)
    last = pl.num_programs(1) - 1
    r = tm + 2 * HALO
    xe_sc[0:HALO] = xp_ref[0]
    xe_sc[HALO:HALO + tm] = x_ref[0]
    xe_sc[HALO + tm:] = xn_ref[0]
    acc_sc[...] = jnp.zeros(acc_sc.shape, F32)
    keep_top = jnp.where(i == 0, 0.0, 1.0)
    keep_bot = jnp.where(i == last, 0.0, 1.0)
    slots = (u0_sc, u1_sc)
    half = tm // 2 if tm % 32 == 0 else tm

    def produce(j, slot, lo=0, hi=r):
        u = _dot(h_sc[lo:hi], wup_ref[j])
        u_sc = slots[slot]
        a, b = lo, hi
        if lo == 0:
            u_sc[0:HALO] = u[0:HALO] * keep_top
            a = HALO
        if hi == r:
            u_sc[r - HALO:r] = u[r - HALO - lo:r - lo] * keep_bot
            b = r - HALO
        u_sc[a:b] = u[a - lo:b - lo]

    def consume(j, slot, lo=0, hi=tm):
        u_sc = slots[slot]
        cw = cw_ref[j]
        cv = cb_ref[j]
        for k in range(3):
            cv = cv + u_sc[HALO - 1 + k + lo:HALO - 1 + k + hi, :] * cw[k:k + 1]
        g = _silu(cv[:, fc:]) * cv[:, :fc]
        acc_sc[lo:hi] += _dot(g.astype(BF16), wdn_ref[j])

    for lo, hi in ((0, half), (half, r)):
        h = _rms(xe_sc[lo:hi], nw2_ref[...]) * (1.0 + sc_ref[0]) + sh_ref[0]
        h_sc[lo:hi] = h.astype(BF16)
        produce(0, 0, lo, hi)

    def body(jj, carry):
        for u in range(FFN_UNROLL):
            produce(FFN_UNROLL * jj + u + 1, (u + 1) % 2)
            consume(FFN_UNROLL * jj + u, u % 2)
        return carry

    trips = (nch - 1) // FFN_UNROLL
    if trips > 1:
        lax.fori_loop(0, trips, body, 0)
    elif trips == 1:
        body(0, 0)
    for j in range(trips * FFN_UNROLL, nch - 1):
        produce(j + 1, (j + 1) % 2)
        consume(j, j % 2)
    for lo, hi in ((0, half), (half, tm)) if half < tm else ((0, tm),):
        consume(nch - 1, (nch - 1) % 2, lo, hi)
        out_ref[0, lo:hi] = x_ref[0, lo:hi] + g_ref[0] * _rms(acc_sc[lo:hi], nw3_ref[...])


def _ffn(x, shift, scale, gate, nw2, nw3, wup, cw, cb, wdn, tm):
    b, t, d = x.shape
    shared = shift.shape[0] == 1
    nch, _, fc2 = wup.shape
    fc = fc2 // 2
    xm, xp, xn = _halo_specs(tm, d, t)
    return pl.pallas_call(
        functools.partial(_ffn_kernel, tm=tm, fc=fc, nch=nch),
        grid=(b, t // tm),
        in_specs=[
            xm, xp, xn,
            _row_spec(d, shared), _row_spec(d, shared), _row_spec(d, shared),
            _const_spec((1, d)), _const_spec((1, d)),
            _const_spec(wup.shape), _const_spec(cw.shape), _const_spec(cb.shape), _const_spec(wdn.shape),
        ],
        out_specs=pl.BlockSpec((1, tm, d), lambda bb, i: (bb, i, 0)),
        out_shape=jax.ShapeDtypeStruct((b, t, d), F32),
        scratch_shapes=[
            pltpu.VMEM((tm + 2 * HALO, d), F32),
            pltpu.VMEM((tm + 2 * HALO, d), BF16),
            pltpu.VMEM((tm + 2 * HALO, fc2), F32),
            pltpu.VMEM((tm + 2 * HALO, fc2), F32),
            pltpu.VMEM((tm, d), F32),
        ],
        compiler_params=_params("parallel", "parallel"),
        name="conv_ffn",
    )(x, x, x, shift, scale, gate, nw2, nw3, wup, cw, cb, wdn)


def _ssm_in_kernel(x_ref, xp_ref, xn_ref, sh_ref, sc_ref, nw_ref, wz_ref, wx_ref, wdt_ref,
                   cw_ref, cb_ref, dtb_ref, z_ref, xs_ref, bm_ref, cm_ref, dt_ref,
                   xe_sc, u0_sc, u1_sc, y0_sc, y1_sc, *, tm, cc):
    u_scs = (u0_sc, u1_sc)
    y_scs = (y0_sc, y1_sc)
    i = pl.program_id(1)
    last = pl.num_programs(1) - 1
    r = tm + 2 * HALO
    _stage_tile(xe_sc, xp_ref, x_ref, xn_ref, tm)

    def mod(v):
        return (_rms(v, nw_ref[...]) * (1.0 + sc_ref[0]) + sh_ref[0]).astype(BF16)

    he = mod(_permuted_rows(xe_sc))
    hm = mod(x_ref[0])
    masks = _edge_masks(r, i == 0, i == last)

    for j in range(D_INNER // cc):
        z_ref[0, :, j * cc:(j + 1) * cc] = _dot(hm, wz_ref[:, j * cc:(j + 1) * cc])

    dt_raw = _dot(hm, wdt_ref[...]) + dtb_ref[...]
    dt_ref[0] = jnp.maximum(dt_raw, 0.0) + jnp.log1p(jnp.exp(-jnp.abs(dt_raw)))

    def down1(a):
        return pltpu.roll(a, 1, 0)

    conv_dim = D_INNER + 2 * BC_DIM
    for j in range(conv_dim // cc):
        lo = j * cc
        u_sc, y_sc = u_scs[j % 2], y_scs[j % 2]
        _store_masked(u_sc, _dot(he, wx_ref[:, lo:lo + cc]), masks)

        def conv(*taps, lo=lo):
            cv = cb_ref[:, lo:lo + cc]
            for k in range(4):
                cv = cv + taps[k] * cw_ref[k:k + 1, lo:lo + cc]
            return _silu(cv)

        act = jnp.concatenate([
            conv(down1(u_sc[r - 16:r - 8]), down1(u_sc[r - 8:r]), u_sc[0:8], u_sc[8:16]),
            conv(down1(u_sc[r - 8:r]), u_sc[0:8], u_sc[8:16], u_sc[16:24]),
            conv(u_sc[0:r - 24], u_sc[8:r - 16], u_sc[16:r - 8], u_sc[24:r]),
            conv(u_sc[r - 24:r - 16], u_sc[r - 16:r - 8], u_sc[r - 8:r], pltpu.roll(u_sc[0:8], 7, 0)),
        ], axis=0)
        _unpermute_into(y_sc, act)
        for c in range(cc // LANES):
            col = lo + c * LANES
            blk = y_sc[c, HALO:HALO + tm]
            if col < D_INNER:
                xs_ref[0, :, col:col + LANES] = blk
            elif col < D_INNER + BC_DIM:
                bm_ref[0, :, col - D_INNER:col - D_INNER + LANES] = blk.astype(BF16)
            else:
                cm_ref[0, :, col - D_INNER - BC_DIM:col - D_INNER - BC_DIM + LANES] = blk.astype(BF16)


def _ssm_in(x, shift, scale, nw, wz, wx, wdt, cw, cb, dtb, tm):
    b, t, d = x.shape
    shared = shift.shape[0] == 1
    cc = 512
    xm, xp, xn = _halo_specs(tm, d, t)
    outs = [(D_INNER, F32), (D_INNER, F32), (BC_DIM, BF16), (BC_DIM, BF16), (2 * LANES, F32)]
    return pl.pallas_call(
        functools.partial(_ssm_in_kernel, tm=tm, cc=cc),
        grid=(b, t // tm),
        in_specs=[
            xm, xp, xn, _row_spec(d, shared), _row_spec(d, shared), _const_spec((1, d)),
            _const_spec(wz.shape), _const_spec(wx.shape), _const_spec(wdt.shape),
            _const_spec(cw.shape), _const_spec(cb.shape), _const_spec(dtb.shape),
        ],
        out_specs=[pl.BlockSpec((1, tm, c), lambda bb, i: (bb, i, 0)) for c, _ in outs],
        out_shape=[jax.ShapeDtypeStruct((b, t, c), dt) for c, dt in outs],
        scratch_shapes=[pltpu.VMEM((d // LANES, tm + 2 * HALO, LANES), F32)]
        + [pltpu.VMEM((tm + 2 * HALO, cc), F32)] * 2 + [pltpu.VMEM((cc // LANES, tm + 2 * HALO, LANES), F32)] * 2,
        compiler_params=_params("parallel", "parallel"),
        name="ssm_in",
    )(x, x, x, shift, scale, nw, wz, wx, wdt, cw, cb, dtb)


DT_REP = 3


def _split3(v, lane):
    p1 = v.astype(BF16).astype(F32)
    r1 = v - p1
    p2 = r1.astype(BF16).astype(F32)
    r2 = r1 - p2
    packed = jnp.where(lane < 32, p1, jnp.where(lane < 64, p2, jnp.where(lane < 96, r2, 0.0)))
    return packed.astype(BF16)


def _sel_matrix(width):
    r = jnp.arange(LANES)[:, None]
    c = jnp.arange(SSM_HEADS * width)[None, :]
    return ((c // width == r % SSM_HEADS) & (r < DT_REP * SSM_HEADS)).astype(BF16)


def _ssd_chunk(xs_ref, dt_ref, bm_ref, cm_ref, rows, a_row, selx_ref, h_sc, reverse, y_store):
    q = SSD_CHUNK
    row = lax.broadcasted_iota(jnp.int32, (q, q), 0)
    col = lax.broadcasted_iota(jnp.int32, (q, q), 1)
    mask = (col >= row) if reverse else (col <= row)
    tri = jnp.where(mask, 1.0, 0.0).astype(BF16)
    lane = lax.broadcasted_iota(jnp.int32, (1, LANES), 1)
    lane_lo = lane < SSM_HEAD_DIM
    end = 0 if reverse else q - 1

    dt = dt_ref[0, rows]
    a = dt * a_row
    a1 = a.astype(BF16)
    r1 = a - a1.astype(F32)
    a2 = r1.astype(BF16)
    a3 = (r1 - a2.astype(F32)).astype(BF16)
    acum = _dot(tri, a1) + _dot(tri, a2) + _dot(tri, a3)
    dt_p = _split3(dt, lane)
    if y_store is not None:
        acum_t = acum.T

    for g in range(SSM_GROUPS):
        gs = slice(g * D_STATE, (g + 1) * D_STATE)
        b_g = bm_ref[0, rows, gs]
        c_g = cm_ref[0, rows, gs]
        h_g = h_sc[g]
        dt_x = _dot(dt_p, selx_ref[:, g * GROUP_W:(g + 1) * GROUP_W])
        if y_store is not None:
            cb = lax.dot_general(c_g, b_g, (((1,), (1,)), ((), ())), preferred_element_type=F32)
            y_inter = _dot(c_g, h_g.astype(BF16))
        xd_parts = []
        etot_parts = []
        for j in range(HEADS_PER_GROUP // 2):
            hd = g * HEADS_PER_GROUP + 2 * j
            pair = hd // 2
            pl_ = slice(j * LANES, (j + 1) * LANES)
            d2 = [jnp.broadcast_to(acum[:, hd + k:hd + k + 1], (q, q)) for k in range(2)]
            dmix = jnp.where(lane_lo, d2[0], d2[1])
            tot = dmix[end:end + 1]
            xdt = xs_ref[0, rows, pair * LANES:(pair + 1) * LANES] * dt_x[:, pl_]
            xd_parts.append((xdt * jnp.exp2(tot - dmix)).astype(BF16))
            etot_parts.append(jnp.exp2(tot))
            if y_store is not None:
                lm = []
                for k in range(2):
                    seg = d2[k] - acum_t[hd + k:hd + k + 1, :]
                    lm.append(cb * jnp.exp2(jnp.where(mask, seg, -jnp.inf)))
                m2 = jnp.concatenate(lm, axis=1).astype(BF16)
                xb = xdt.astype(BF16)
                zero = jnp.zeros_like(xb)
                rhs = jnp.concatenate([jnp.where(lane_lo, xb, zero), jnp.where(lane_lo, zero, xb)], axis=0)
                y_intra = _dot(m2, rhs)
                y_store(pair, y_intra + y_inter[:, pl_] * jnp.exp2(dmix))
        xd = jnp.concatenate(xd_parts, axis=1)
        upd = lax.dot_general(b_g, xd, (((0,), (0,)), ((), ())), preferred_element_type=F32)
        h_sc[g] = jnp.concatenate(etot_parts, axis=1) * h_g + upd


def _ssd_state_kernel(xs_ref, dtf_ref, dtb_ref, bm_ref, cm_ref, af_ref, ab_ref, selx_ref,
                      hf_ref, hb_ref, h_sc, *, nchunks):
    q = SSD_CHUNK
    for reverse, dt_ref, a_ref, out_ref in ((False, dtf_ref, af_ref, hf_ref), (True, dtb_ref, ab_ref, hb_ref)):
        h_sc[...] = jnp.zeros(h_sc.shape, F32)
        order = range(nchunks - 1, -1, -1) if reverse else range(nchunks)
        for ci in order:
            rows = slice(ci * q, (ci + 1) * q)
            _ssd_chunk(xs_ref, dt_ref, bm_ref, cm_ref, rows, a_ref[...], selx_ref, h_sc, reverse, None)
        out_ref[0] = h_sc[...]


def _dir_spec(rev):
    return pl.BlockSpec((1, LANES), lambda *_: (0, rev))


def _ssd_states(xs, dt, bm, cm, a_row, selx):
    b, t, _ = xs.shape
    st = jax.ShapeDtypeStruct((b, SSM_GROUPS, D_STATE, GROUP_W), F32)
    full = lambda c, j=0: pl.BlockSpec((1, t, c), lambda bb: (bb, 0, j))
    st_spec = pl.BlockSpec((1, SSM_GROUPS, D_STATE, GROUP_W), lambda bb: (bb, 0, 0, 0))
    return pl.pallas_call(
        functools.partial(_ssd_state_kernel, nchunks=t // SSD_CHUNK),
        grid=(b,),
        in_specs=[full(D_INNER), full(LANES, 0), full(LANES, 1), full(BC_DIM), full(BC_DIM),
                  _dir_spec(0), _dir_spec(1), _const_spec(selx.shape)],
        out_specs=[st_spec, st_spec],
        out_shape=[st, st],
        scratch_shapes=[pltpu.VMEM((SSM_GROUPS, D_STATE, GROUP_W), F32)],
        compiler_params=_params("parallel"),
        name="ssd_ctx_states",
    )(xs, dt, dt, bm, cm, a_row, a_row, selx)


def _ssd_fwd_kernel(xs_ref, dt_ref, bm_ref, cm_ref, a_ref, selx_ref, h0_ref, y_ref, h_sc, *, cps):
    q = SSD_CHUNK

    @pl.when(pl.program_id(1) == 0)
    def _():
        h_sc[...] = h0_ref[0]

    for ci in range(cps):
        rows = slice(ci * q, (ci + 1) * q)

        def store(pair, val, rows=rows):
            y_ref[0, rows, pair * LANES:(pair + 1) * LANES] = val

        _ssd_chunk(xs_ref, dt_ref, bm_ref, cm_ref, rows, a_ref[...], selx_ref, h_sc, False, store)


def _ssd_fwd(xs, dt, bm, cm, a_row, selx, h0, cps):
    b, t, _ = xs.shape
    tm = cps * SSD_CHUNK
    blk = lambda c: pl.BlockSpec((1, tm, c), lambda bb, i: (bb, i, 0))
    return pl.pallas_call(
        functools.partial(_ssd_fwd_kernel, cps=cps),
        grid=(b, t // tm),
        in_specs=[blk(D_INNER), blk(LANES), blk(BC_DIM), blk(BC_DIM), _dir_spec(0),
                  _const_spec(selx.shape),
                  pl.BlockSpec((1, SSM_GROUPS, D_STATE, GROUP_W), lambda bb, i: (bb, 0, 0, 0))],
        out_specs=blk(D_INNER),
        out_shape=jax.ShapeDtypeStruct((b, t, D_INNER), F32),
        scratch_shapes=[pltpu.VMEM((SSM_GROUPS, D_STATE, GROUP_W), F32)],
        compiler_params=_params("parallel", "arbitrary"),
        name="ssd_fwd",
    )(xs, dt, bm, cm, a_row, selx, h0)


def _ssd_bwd_out_kernel(xs_ref, dt_ref, bm_ref, cm_ref, a_ref, selx_ref, h0_ref, yf_ref, z_ref, x_ref,
                        g_ref, dsk_ref, snw_ref, wo_ref, nw_ref, out_ref, h_sc, yb_sc, *, cps):
    q = SSD_CHUNK

    @pl.when(pl.program_id(1) == 0)
    def _():
        h_sc[...] = h0_ref[0]

    for ci in range(cps - 1, -1, -1):
        rows = slice(ci * q, (ci + 1) * q)

        def store(pair, val, rows=rows):
            yb_sc[rows, pair * LANES:(pair + 1) * LANES] = val

        _ssd_chunk(xs_ref, dt_ref, bm_ref, cm_ref, rows, a_ref[...], selx_ref, h_sc, True, store)

    y = yf_ref[0] + yb_sc[...] + dsk_ref[...] * xs_ref[0]
    yn = _rms(y * _silu(z_ref[0]), snw_ref[...])
    o = _dot(yn.astype(BF16), wo_ref[...])
    out_ref[0] = x_ref[0] + g_ref[0] * _rms(o, nw_ref[...])


def _ssd_bwd_out(xs, dt, bm, cm, a_row, selx, h0, yf, z, x, gate, dskip, snw, w_out, nw, cps):
    b, t, d = x.shape
    tm = cps * SSD_CHUNK
    nblk = t // tm
    blk = lambda c, j=0: pl.BlockSpec((1, tm, c), lambda bb, i: (bb, nblk - 1 - i, j))
    return pl.pallas_call(
        functools.partial(_ssd_bwd_out_kernel, cps=cps),
        grid=(b, nblk),
        in_specs=[blk(D_INNER), blk(LANES, 1), blk(BC_DIM), blk(BC_DIM), _dir_spec(1),
                  _const_spec(selx.shape),
                  pl.BlockSpec((1, SSM_GROUPS, D_STATE, GROUP_W), lambda bb, i: (bb, 0, 0, 0)),
                  blk(D_INNER), blk(D_INNER), blk(d),
                  pl.BlockSpec((1, 1, d), lambda bb, i: (bb, 0, 0)),
                  _const_spec((1, D_INNER)), _const_spec((1, D_INNER)), _const_spec((D_INNER, d)),
                  _const_spec((1, d))],
        out_specs=blk(d),
        out_shape=jax.ShapeDtypeStruct((b, t, d), F32),
        scratch_shapes=[pltpu.VMEM((SSM_GROUPS, D_STATE, GROUP_W), F32), pltpu.VMEM((tm, D_INNER), F32)],
        compiler_params=_params("parallel", "arbitrary"),
        name="ssd_bwd_out",
    )(xs, dt, bm, cm, a_row, selx, h0, yf, z, x, gate, dskip, snw, w_out, nw)


def _rope_tables(t):
    rows = t // GRID_W
    row = jnp.repeat(jnp.arange(rows, dtype=F32), GRID_W)
    col = jnp.tile(jnp.arange(GRID_W, dtype=F32), rows)
    half = HEAD_DIM // 2
    inv_freq = ROPE_THETA ** (-jnp.arange(0, half, 2, dtype=F32) / half)
    ang = jnp.concatenate([row[:, None] * inv_freq, col[:, None] * inv_freq], axis=-1)
    cos, sin = jnp.cos(ang), jnp.sin(ang)
    return jnp.concatenate([cos, cos], axis=-1), jnp.concatenate([-sin, sin], axis=-1)


def _tile(t, pref):
    return pref if t % pref == 0 else t


def kernel(x, c, ctx, c_ctx, ada_w, ada_b, norm_w, attn_w_in, pool_w, pool_scale, q_gain, k_gain, attn_w_out,
           ssm_w_in, ssm_conv_w, ssm_conv_b, ssm_A_log, ssm_dt_bias, ssm_D, ssm_norm_w, ssm_w_out,
           ffn_w_up, ffn_conv_w, ffn_conv_b, ffn_w_down):
    bsz, t, d = x.shape
    lc = ctx.shape[1]
    depth = ada_w.shape[0]
    assert depth == 2 and attn_w_in.shape[0] == 1 and ssm_w_in.shape[0] == 1
    d_ff = ffn_w_down.shape[1]

    rows = -(-(bsz + 1) // HALO) * HALO
    cond = jnp.zeros((rows, d), F32).at[:bsz].set(c).at[bsz].set(c_ctx)
    ada = _adaln(cond, ada_w, ada_b)
    mod_l = [[ada[i, :bsz, k * d:(k + 1) * d].reshape(bsz, 1, d) for k in range(6)] for i in range(depth)]
    mod_c = [[ada[i, bsz:bsz + 1, k * d:(k + 1) * d].reshape(1, 1, d) for k in range(6)] for i in range(depth)]
    nw = lambda i, k: norm_w[i, k].reshape(1, d)

    fc = 256
    nch = d_ff // fc

    def ffn_weights(i):
        wu = ffn_w_up[i].astype(BF16)
        wup = jnp.concatenate([wu[:, :d_ff].reshape(d, nch, fc), wu[:, d_ff:].reshape(d, nch, fc)], axis=-1)
        wup = wup.transpose(1, 0, 2)
        cwt = ffn_conv_w[i].T
        cw = jnp.concatenate([cwt[:, :d_ff].reshape(3, nch, fc), cwt[:, d_ff:].reshape(3, nch, fc)], axis=-1)
        cw = cw.transpose(1, 0, 2)
        cb = jnp.concatenate([ffn_conv_b[i][:d_ff].reshape(nch, 1, fc), ffn_conv_b[i][d_ff:].reshape(nch, 1, fc)],
                             axis=-1)
        wdn = ffn_w_down[i].astype(BF16).reshape(nch, fc, d)
        return wup, cw, cb, wdn

    def ffn(xx, mods, i, tm):
        wup, cw, cb, wdn = ffn_weights(i)
        return _ffn(xx, mods[3], mods[4], mods[5], nw(i, 2), nw(i, 3), wup, cw, cb, wdn, tm)

    perm = jnp.concatenate([jnp.arange(0, HEAD_DIM, 2), jnp.arange(1, HEAD_DIM, 2)])
    cols = jnp.arange(attn_w_in.shape[2])
    qk_lo, qk_hi = POOL_DIM, POOL_DIM + ATTN_DIM + KV_DIM
    qk_cols = (qk_lo + (jnp.arange(qk_hi - qk_lo) // HEAD_DIM) * HEAD_DIM
               + perm[jnp.arange(qk_hi - qk_lo) % HEAD_DIM])
    cols = cols.at[qk_lo:qk_hi].set(qk_cols)
    w_in_a = attn_w_in[0][:, cols].astype(BF16)
    qg = (q_gain[0][perm] * SOFTMAX_C).reshape(1, HEAD_DIM)
    kg = k_gain[0][perm].reshape(1, HEAD_DIM)
    cos2, sin2 = _rope_tables(t)
    eye = jnp.eye(len(POOL_WINDOWS), dtype=F32)
    pool_bd = (eye[:, None, :, None] * pool_w[0][:, :, None, :]).reshape(POOL_DIM, POOL_DIM).astype(BF16)
    ps = pool_scale[0].reshape(1, POOL_DIM)
    w_out_a = attn_w_out[0][:POOL_DIM].astype(BF16)
    w_out_o = attn_w_out[0][POOL_DIM:].astype(BF16)

    tm_l = _tile(t, 512)
    tm_c = _tile(lc, 512)
    a_c, q_c, k_c, v_c = _attn_in(ctx, mod_c[0][0], mod_c[0][1], nw(0, 0), w_in_a, qg, kg, None, None, tm_c)
    a_l, q_l, k_l, v_l = _attn_in(x, mod_l[0][0], mod_l[0][1], nw(0, 0), w_in_a, qg, kg, cos2, sin2, tm_l)
    k_all = jnp.concatenate([k_c, k_l], axis=1)
    v_all = jnp.concatenate([v_c, v_l], axis=1)
    o_l = _attention(q_l, k_all, v_all, _tile(t, 256), 512)
    o_c = _attention(q_c, k_c, v_c, _tile(lc, 256), 512)
    x = _attn_out(x, mod_l[0][2], nw(0, 1), a_l, o_l, pool_bd, ps, w_out_a, w_out_o, tm_l)
    ctx = _attn_out(ctx, mod_c[0][2], nw(0, 1), a_c, o_c, pool_bd, ps, w_out_a, w_out_o, tm_c)
    x = ffn(x, mod_l[0], 0, tm_l)
    ctx = ffn(ctx, mod_c[0], 0, tm_c)

    w_in_c = ssm_w_in[0].astype(BF16)
    conv_dim = D_INNER + 2 * BC_DIM
    wz = w_in_c[:, :D_INNER]
    wx = w_in_c[:, D_INNER:D_INNER + conv_dim]

    def dir_lanes(v):
        pad = jnp.zeros(v.shape[:-2] + (LANES - DT_REP * SSM_HEADS,), v.dtype)
        return jnp.concatenate([p for r in range(2) for p in [v[..., r, :]] * DT_REP + [pad]], axis=-1)

    wdt = dir_lanes(w_in_c[:, D_INNER + conv_dim:].reshape(d, 2, SSM_HEADS))
    dtb = dir_lanes(ssm_dt_bias[0]).reshape(1, 2 * LANES)
    a_row = dir_lanes(-jnp.exp(ssm_A_log[0].astype(F32)) * math.log2(math.e)).reshape(1, 2 * LANES)
    selx = _sel_matrix(SSM_HEAD_DIM)
    scw = ssm_conv_w[0].T
    scb = ssm_conv_b[0].reshape(1, conv_dim)
    dskip = jnp.repeat(ssm_D[0], SSM_HEAD_DIM).reshape(1, D_INNER)
    snw = ssm_norm_w[0].reshape(1, D_INNER)
    w_out_c = ssm_w_out[0].astype(BF16)

    tm_s = _tile(t, 256)
    tm_sc = _tile(lc, 256)
    _, xs_c, bm_c, cm_c, dt_c = _ssm_in(ctx, mod_c[1][0], mod_c[1][1], nw(1, 0), wz, wx, wdt, scw, scb, dtb, tm_sc)
    hf0, hb0 = _ssd_states(xs_c, dt_c, bm_c, cm_c, a_row, selx)
    z_l, xs_l, bm_l, cm_l, dt_l = _ssm_in(x, mod_l[1][0], mod_l[1][1], nw(1, 0), wz, wx, wdt, scw, scb, dtb, tm_s)
    cps = lambda want: max(c for c in (1, 2, 4) if c <= want and t % (c * SSD_CHUNK) == 0)
    yf = _ssd_fwd(xs_l, dt_l, bm_l, cm_l, a_row, selx, hf0, cps(2))
    x = _ssd_bwd_out(xs_l, dt_l, bm_l, cm_l, a_row, selx, hb0, yf, z_l, x, mod_l[1][2], dskip, snw, w_out_c,
                     nw(1, 1), cps(4))
    x = ffn(x, mod_l[1], 1, tm_l)
    return x
```

```python
import functools
import math

import jax
import jax.numpy as jnp
from jax import lax
from jax.experimental import pallas as pl
from jax.experimental.pallas import tpu as pltpu

F32 = jnp.float32
BF16 = jnp.bfloat16

EPS = 1e-6
GRID_W = 64
ROPE_THETA = 10000.0

HEAD_DIM = 128
N_Q_HEADS = 6
N_KV_HEADS = 2
Q_PER_KV = N_Q_HEADS // N_KV_HEADS
POOL_WINDOWS = (2, 4, 8, 16)
POOL_GROUP_DIM = 64
POOL_DIM = 256
ATTN_DIM = N_Q_HEADS * HEAD_DIM
KV_DIM = N_KV_HEADS * HEAD_DIM

SSM_HEAD_DIM = 64
SSM_HEADS = 32
SSM_GROUPS = 4
HEADS_PER_GROUP = 8
D_STATE = 128
D_INNER = SSM_HEADS * SSM_HEAD_DIM
GROUP_W = HEADS_PER_GROUP * SSM_HEAD_DIM
BC_DIM = SSM_GROUPS * D_STATE
SSD_CHUNK = 128

HALO = 8
LANES = 128
VMEM_LIMIT = 56 * 1024 * 1024


def _params(*sem):
    return pltpu.CompilerParams(dimension_semantics=sem, vmem_limit_bytes=VMEM_LIMIT)


def _const_spec(shape):
    nd = len(shape)
    return pl.BlockSpec(shape, lambda *_: (0,) * nd, pipeline_mode=pl.Buffered(1))


def _row_spec(c, shared):
    if shared:
        return pl.BlockSpec((1, 1, c), lambda b, i: (0, 0, 0))
    return pl.BlockSpec((1, 1, c), lambda b, i: (b, 0, 0))


def _halo_specs(tm, c, t):
    nb = tm // HALO
    last = t // HALO - 1
    main = pl.BlockSpec((1, tm, c), lambda b, i: (b, i, 0))
    prev = pl.BlockSpec((1, HALO, c), lambda b, i: (b, jnp.maximum(i * nb - 1, 0), 0))
    nxt = pl.BlockSpec((1, HALO, c), lambda b, i: (b, jnp.minimum((i + 1) * nb, last), 0))
    return main, prev, nxt


def _rms(xf, w):
    ms = jnp.mean(xf * xf, axis=-1, keepdims=True)
    return xf * lax.rsqrt(ms + EPS) * w


def _silu(x):
    return x * jax.nn.sigmoid(x)


def _dot(a, b):
    return jnp.dot(a, b, preferred_element_type=F32)


def _ada_kernel(c_ref, w_ref, b_ref, o_ref):
    s = _silu(c_ref[...])
    o_ref[0] = _dot(s.astype(BF16), w_ref[0].astype(BF16)) + b_ref[0]


def _adaln(cond, ada_w, ada_b):
    depth, d, n = ada_w.shape
    rows = cond.shape[0]
    tn = 1536
    return pl.pallas_call(
        _ada_kernel,
        grid=(depth, n // tn),
        in_specs=[
            pl.BlockSpec((rows, d), lambda l, j: (0, 0)),
            pl.BlockSpec((1, d, tn), lambda l, j: (l, 0, j)),
            pl.BlockSpec((1, 1, tn), lambda l, j: (l, 0, j)),
        ],
        out_specs=pl.BlockSpec((1, rows, tn), lambda l, j: (l, 0, j)),
        out_shape=jax.ShapeDtypeStruct((depth, rows, n), F32),
        compiler_params=_params("parallel", "parallel"),
        name="adaln",
    )(cond, ada_w, ada_b.reshape(depth, 1, n))


def _attn_in_kernel(*refs, rope):
    if rope:
        (x_ref, sh_ref, sc_ref, nw_ref, w_ref, qg_ref, kg_ref, cos_ref, sin_ref,
         a_ref, q_ref, k_ref, v_ref) = refs
    else:
        (x_ref, sh_ref, sc_ref, nw_ref, w_ref, qg_ref, kg_ref,
         a_ref, q_ref, k_ref, v_ref) = refs
    tm = x_ref.shape[1]
    nblk = 2 if tm % (2 * 128) == 0 else 1
    rb = tm // nblk
    for blk in range(nblk):
        rows = slice(blk * rb, (blk + 1) * rb)
        h = _rms(x_ref[0, rows], nw_ref[...]) * (1.0 + sc_ref[0]) + sh_ref[0]
        u = _dot(h.astype(BF16), w_ref[...])
        a_ref[0, rows] = u[:, :POOL_DIM]

        def norm_rope(t, gain, rows=rows):
            y = _rms(t, gain)
            if rope:
                y = y * cos_ref[rows] + pltpu.roll(y, HEAD_DIM // 2, 1) * sin_ref[rows]
            return y.astype(BF16)

        off = POOL_DIM
        for hh in range(N_Q_HEADS):
            q_ref[0, rows, hh * HEAD_DIM:(hh + 1) * HEAD_DIM] = norm_rope(
                u[:, off + hh * HEAD_DIM: off + (hh + 1) * HEAD_DIM], qg_ref[...])
        off += ATTN_DIM
        for hh in range(N_KV_HEADS):
            k_ref[0, rows, hh * HEAD_DIM:(hh + 1) * HEAD_DIM] = norm_rope(
                u[:, off + hh * HEAD_DIM: off + (hh + 1) * HEAD_DIM], kg_ref[...])
        off += KV_DIM
        v_ref[0, rows] = u[:, off:].astype(BF16)


def _attn_in(x, shift, scale, nw, w_in, qg, kg, cos2, sin2, tm):
    b, t, d = x.shape
    shared = shift.shape[0] == 1
    rope = cos2 is not None
    width = w_in.shape[1]
    in_specs = [
        pl.BlockSpec((1, tm, d), lambda bb, i: (bb, i, 0)),
        _row_spec(d, shared), _row_spec(d, shared),
        _const_spec((1, d)), _const_spec((d, width)),
        _const_spec((1, HEAD_DIM)), _const_spec((1, HEAD_DIM)),
    ]
    args = [x, shift, scale, nw, w_in, qg, kg]
    if rope:
        in_specs += [pl.BlockSpec((tm, HEAD_DIM), lambda bb, i: (i, 0))] * 2
        args += [cos2, sin2]
    outs = [(POOL_DIM, F32), (ATTN_DIM, BF16), (KV_DIM, BF16), (KV_DIM, BF16)]
    return pl.pallas_call(
        functools.partial(_attn_in_kernel, rope=rope),
        grid=(b, t // tm),
        in_specs=in_specs,
        out_specs=[pl.BlockSpec((1, tm, c), lambda bb, i: (bb, i, 0)) for c, _ in outs],
        out_shape=[jax.ShapeDtypeStruct((b, t, c), dt) for c, dt in outs],
        compiler_params=_params("parallel", "parallel"),
        name="attn_in",
    )(*args)


SUM_ROWS = 16
SOFTMAX_C = (HEAD_DIM ** -0.5) * math.log2(math.e)
ATTN_QB = 256
ATTN_UNROLL = 8


def _attn_kernel(q_ref, k_ref, vt_ref, o_ref, m_sc, acc_sc, s0_sc, mc0_sc, s1_sc, mc1_sc, *, n, tq):
    slots = ((s0_sc, mc0_sc), (s1_sc, mc1_sc))

    qb = min(ATTN_QB, tq)
    blocks = [(sub, g) for sub in range(tq // qb) for g in range(Q_PER_KV)]
    q = q_ref[0]
    qs = jnp.concatenate([q[sub * qb:(sub + 1) * qb, g * HEAD_DIM:(g + 1) * HEAD_DIM] for sub, g in blocks], axis=0)
    m_sc[...] = jnp.full(m_sc.shape, -jnp.inf, F32)
    acc_sc[...] = jnp.zeros(acc_sc.shape, F32)

    def produce(j, slot, b):
        s_sc, mc_sc = slots[slot]
        cols = slice(b * qb, (b + 1) * qb)
        s = lax.dot_general(k_ref[0, 0, j], qs[cols], (((1,), (1,)), ((), ())),
                            preferred_element_type=F32)
        s_sc[:, cols] = s
        mc_sc[:, cols] = jnp.max(s, axis=0, keepdims=True)

    def consume(j, slot, b):
        s_sc, mc_sc = slots[slot]
        cols = slice(b * qb, (b + 1) * qb)
        m_prev = m_sc[:, cols]
        m_new = jnp.maximum(m_prev, mc_sc[:, cols])
        alpha = jnp.exp2(m_prev - m_new)
        p = jnp.exp2(s_sc[:, cols] - m_new).astype(BF16)
        acc_sc[:, cols] = alpha * acc_sc[:, cols] + _dot(vt_ref[0, 0, j], p)
        m_sc[:, cols] = m_new

    def step(jp, sp, jc, sc):
        for b in range(len(blocks)):
            if jp is not None:
                produce(jp, sp, b)
            if jc is not None:
                consume(jc, sc, b)

    step(0, 0, None, None)

    def body(jj, carry):
        for u in range(ATTN_UNROLL):
            step(ATTN_UNROLL * jj + u + 1, (u + 1) % 2, ATTN_UNROLL * jj + u, u % 2)
        return carry

    trips = (n - 1) // ATTN_UNROLL
    lax.fori_loop(0, trips, body, 0)
    for j in range(trips * ATTN_UNROLL, n - 1):
        step(j + 1, (j + 1) % 2, j, j % 2)
    step(None, None, n - 1, (n - 1) % 2)

    acc = acc_sc[...]
    o_t = acc[:HEAD_DIM] / acc[HEAD_DIM:HEAD_DIM + 1]
    for b, (sub, g) in enumerate(blocks):
        o_ref[0, sub * qb:(sub + 1) * qb, g * HEAD_DIM:(g + 1) * HEAD_DIM] = (
            o_t[:, b * qb:(b + 1) * qb].T.astype(BF16))


def _attention(q, kv_parts, tq, tk):
    b, t, _ = q.shape
    gw = Q_PER_KV * HEAD_DIM
    tk = min(tk, sum(k.shape[1] for k, _ in kv_parts))

    def chunked(k, v, live):
        n = k.shape[1] // tk
        kc = k.reshape(b, n, tk, N_KV_HEADS, HEAD_DIM).transpose(0, 3, 1, 2, 4)
        vt = v.reshape(b, n, tk, N_KV_HEADS, HEAD_DIM).transpose(0, 3, 1, 4, 2)
        ones = jnp.broadcast_to(live.astype(BF16).reshape(n, 1, tk), (b, N_KV_HEADS, n, SUM_ROWS, tk))
        return kc, jnp.concatenate([vt, ones], axis=3)

    pieces = [chunked(k, v, jnp.ones((k.shape[1],), BF16)) for k, v in kv_parts if k.shape[1] % tk == 0]
    rest = [(k, v) for k, v in kv_parts if k.shape[1] % tk != 0]
    if rest:
        kr = jnp.concatenate([k for k, _ in rest], axis=1)
        vr = jnp.concatenate([v for _, v in rest], axis=1)
        lr = kr.shape[1]
        pad = (-lr) % tk
        assert pad <= lr
        kr = jnp.concatenate([kr, kr[:, :pad]], axis=1)
        vr = jnp.concatenate([vr, jnp.zeros((b, pad, KV_DIM), BF16)], axis=1)
        pieces.append(chunked(kr, vr, jnp.arange(lr + pad) < lr))
    kc = jnp.concatenate([p[0] for p in pieces], axis=2)
    vt = jnp.concatenate([p[1] for p in pieces], axis=2)
    n = kc.shape[2]
    m = Q_PER_KV * tq
    return pl.pallas_call(
        functools.partial(_attn_kernel, n=n, tq=tq),
        grid=(b, N_KV_HEADS, t // tq),
        in_specs=[
            pl.BlockSpec((1, tq, gw), lambda bb, g, i: (bb, i, g)),
            pl.BlockSpec((1, 1) + kc.shape[2:], lambda bb, g, i: (bb, g, 0, 0, 0)),
            pl.BlockSpec((1, 1) + vt.shape[2:], lambda bb, g, i: (bb, g, 0, 0, 0)),
        ],
        out_specs=pl.BlockSpec((1, tq, gw), lambda bb, g, i: (bb, i, g)),
        out_shape=jax.ShapeDtypeStruct((b, t, ATTN_DIM), BF16),
        scratch_shapes=[pltpu.VMEM((1, m), F32), pltpu.VMEM((HEAD_DIM + SUM_ROWS, m), F32)]
        + [pltpu.VMEM((tk, m), F32), pltpu.VMEM((1, m), F32)] * 2,
        compiler_params=_params("parallel", "parallel", "arbitrary"),
        name="attention",
    )(q, kc, vt)


def _attn_out_kernel(x_ref, g_ref, nw_ref, a_ref, ap_ref, an_ref, o_ref, pw_ref, ps_ref, wa_ref, wo_ref,
                     out_ref, abuf, *, tm, t):
    i = pl.program_id(1)
    last = pl.num_programs(1) - 1
    abuf[0:HALO] = jnp.where(i == 0, 0.0, ap_ref[0])
    abuf[HALO:HALO + tm] = a_ref[0]
    abuf[HALO + tm:] = jnp.where(i == last, 0.0, an_ref[0])

    lane = lax.broadcasted_iota(jnp.int32, (1, POOL_DIM), 1)
    nblk = 2 if tm % (2 * 128) == 0 else 1
    rb = tm // nblk
    for blk in range(nblk):
        r0 = blk * rb
        rows = slice(r0, r0 + rb)

        def sh(d, r0=r0):
            return abuf[HALO + d + r0:HALO + d + r0 + rb, :]

        x0 = sh(0)
        s2 = sh(-1) + x0
        s4 = s2 + sh(-2) + sh(1)
        s8 = s4 + sh(-4) + sh(-3) + sh(2) + sh(3)
        s16 = s8 + sh(-8) + sh(-7) + sh(-6) + sh(-5) + sh(4) + sh(5) + sh(6) + sh(7)
        pos = i * tm + r0 + lax.broadcasted_iota(jnp.int32, (rb, 1), 0)

        def mean(s, w, pos=pos):
            left = w // 2
            right = w - 1 - left
            cnt = jnp.minimum(pos + right, t - 1) - jnp.maximum(pos - left, 0) + 1
            return s / cnt.astype(F32)

        pm = jnp.where(lane < 64, mean(s2, 2),
                       jnp.where(lane < 128, mean(s4, 4),
                                 jnp.where(lane < 192, mean(s8, 8), mean(s16, 16))))
        p = pm - x0
        py = _dot(p.astype(BF16), pw_ref[...]) * ps_ref[...]
        y = _dot(py.astype(BF16), wa_ref[...]) + _dot(o_ref[0, rows], wo_ref[...])
        out_ref[0, rows] = x_ref[0, rows] + g_ref[0] * _rms(y, nw_ref[...])


def _attn_out(x, gate, nw, a, o, pool_bd, pool_scale, w_a, w_o, tm):
    b, t, d = x.shape
    shared = gate.shape[0] == 1
    am, ap, an = _halo_specs(tm, POOL_DIM, t)
    return pl.pallas_call(
        functools.partial(_attn_out_kernel, tm=tm, t=t),
        grid=(b, t // tm),
        in_specs=[
            pl.BlockSpec((1, tm, d), lambda bb, i: (bb, i, 0)),
            _row_spec(d, shared), _const_spec((1, d)),
            am, ap, an,
            pl.BlockSpec((1, tm, ATTN_DIM), lambda bb, i: (bb, i, 0)),
            _const_spec((POOL_DIM, POOL_DIM)), _const_spec((1, POOL_DIM)),
            _const_spec((POOL_DIM, d)), _const_spec((ATTN_DIM, d)),
        ],
        out_specs=pl.BlockSpec((1, tm, d), lambda bb, i: (bb, i, 0)),
        out_shape=jax.ShapeDtypeStruct((b, t, d), F32),
        scratch_shapes=[pltpu.VMEM((tm + 2 * HALO, POOL_DIM), F32)],
        compiler_params=_params("parallel", "parallel"),
        name="attn_out",
    )(x, gate, nw, a, a, a, o, pool_bd, pool_scale, w_a, w_o)


def _stage_tile(xe_sc, xp_ref, x_ref, xn_ref, tm):
    for c in range(xe_sc.shape[0]):
        cols = slice(c * LANES, (c + 1) * LANES)
        xe_sc[c, 0:HALO] = xp_ref[0, :, cols]
        xe_sc[c, HALO:HALO + tm] = x_ref[0, :, cols]
        xe_sc[c, HALO + tm:] = xn_ref[0, :, cols]


def _permuted_rows(xe_sc):
    nslab, r, _ = xe_sc.shape
    nv = r // 8
    return jnp.concatenate(
        [jnp.concatenate([xe_sc[c, pl.ds(v, 8, stride=nv), :] for c in range(nslab)], axis=1) for v in range(nv)],
        axis=0)


def _unpermute_into(ye_sc, val):
    nslab, r, _ = ye_sc.shape
    nv = r // 8
    for c in range(nslab):
        for v in range(nv):
            ye_sc[c, pl.ds(v, 8, stride=nv), :] = val[8 * v:8 * v + 8, c * LANES:(c + 1) * LANES]


def _edge_masks(r, at_start, at_end):
    sub = lax.broadcasted_iota(jnp.int32, (8 * HALO, 1), 0) % 8
    top = jnp.where(jnp.logical_and(sub == 0, at_start), 0.0, 1.0)
    bot = jnp.where(jnp.logical_and(sub == 7, at_end), 0.0, 1.0)
    return top, bot


def _store_masked(u_sc, u, masks):
    r = u.shape[0]
    edge = 8 * HALO
    u_sc[0:edge] = u[0:edge] * masks[0]
    u_sc[edge:r - edge] = u[edge:r - edge]
    u_sc[r - edge:] = u[r - edge:] * masks[1]


FFN_UNROLL = 2


def _ffn_kernel(x_ref, xp_ref, xn_ref, sh_ref, sc_ref, g_ref, nw2_ref, nw3_ref,
                wup_ref, cw_ref, cb_ref, wdn_ref, out_ref, xe_sc, h_sc, u0_sc, u1_sc, acc_sc, *, tm, fc, nch):
    i = pl.program_id(1)
    last = pl.num_programs(1) - 1
    r = tm + 2 * HALO
    xe_sc[0:HALO] = xp_ref[0]
    xe_sc[HALO:HALO + tm] = x_ref[0]
    xe_sc[HALO + tm:] = xn_ref[0]
    acc_sc[...] = jnp.zeros(acc_sc.shape, F32)
    keep_top = jnp.where(i == 0, 0.0, 1.0)
    keep_bot = jnp.where(i == last, 0.0, 1.0)
    slots = (u0_sc, u1_sc)
    half = tm // 2 if tm % 32 == 0 else tm

    def produce(j, slot, lo=0, hi=r):
        u = _dot(h_sc[lo:hi], wup_ref[j])
        u_sc = slots[slot]
        a, b = lo, hi
        if lo == 0:
            u_sc[0:HALO] = u[0:HALO] * keep_top
            a = HALO
        if hi == r:
            u_sc[r - HALO:r] = u[r - HALO - lo:r - lo] * keep_bot
            b = r - HALO
        u_sc[a:b] = u[a - lo:b - lo]

    def consume(j, slot, lo=0, hi=tm):
        u_sc = slots[slot]
        cw = cw_ref[j]
        cv = cb_ref[j]
        for k in range(3):
            cv = cv + u_sc[HALO - 1 + k + lo:HALO - 1 + k + hi, :] * cw[k:k + 1]
        g = _silu(cv[:, fc:]) * cv[:, :fc]
        acc_sc[lo:hi] += _dot(g.astype(BF16), wdn_ref[j])

    for lo, hi in ((0, half), (half, r)):
        h = _rms(xe_sc[lo:hi], nw2_ref[...]) * (1.0 + sc_ref[0]) + sh_ref[0]
        h_sc[lo:hi] = h.astype(BF16)
        produce(0, 0, lo, hi)

    def body(jj, carry):
        for u in range(FFN_UNROLL):
            produce(FFN_UNROLL * jj + u + 1, (u + 1) % 2)
            consume(FFN_UNROLL * jj + u, u % 2)
        return carry

    trips = (nch - 1) // FFN_UNROLL
    if trips > 1:
        lax.fori_loop(0, trips, body, 0)
    elif trips == 1:
        body(0, 0)
    for j in range(trips * FFN_UNROLL, nch - 1):
        produce(j + 1, (j + 1) % 2)
        consume(j, j % 2)
    for lo, hi in ((0, half), (half, tm)) if half < tm else ((0, tm),):
        consume(nch - 1, (nch - 1) % 2, lo, hi)
        out_ref[0, lo:hi] = x_ref[0, lo:hi] + g_ref[0] * _rms(acc_sc[lo:hi], nw3_ref[...])


def _ffn(x, shift, scale, gate, nw2, nw3, wup, cw, cb, wdn, tm):
    b, t, d = x.shape
    shared = shift.shape[0] == 1
    nch, _, fc2 = wup.shape
    fc = fc2 // 2
    xm, xp, xn = _halo_specs(tm, d, t)
    return pl.pallas_call(
        functools.partial(_ffn_kernel, tm=tm, fc=fc, nch=nch),
        grid=(b, t // tm),
        in_specs=[
            xm, xp, xn,
            _row_spec(d, shared), _row_spec(d, shared), _row_spec(d, shared),
            _const_spec((1, d)), _const_spec((1, d)),
            _const_spec(wup.shape), _const_spec(cw.shape), _const_spec(cb.shape), _const_spec(wdn.shape),
        ],
        out_specs=pl.BlockSpec((1, tm, d), lambda bb, i: (bb, i, 0)),
        out_shape=jax.ShapeDtypeStruct((b, t, d), F32),
        scratch_shapes=[
            pltpu.VMEM((tm + 2 * HALO, d), F32),
            pltpu.VMEM((tm + 2 * HALO, d), BF16),
            pltpu.VMEM((tm + 2 * HALO, fc2), F32),
            pltpu.VMEM((tm + 2 * HALO, fc2), F32),
            pltpu.VMEM((tm, d), F32),
        ],
        compiler_params=_params("parallel", "parallel"),
        name="conv_ffn",
    )(x, x, x, shift, scale, gate, nw2, nw3, wup, cw, cb, wdn)


def _ssm_in_kernel(x_ref, xp_ref, xn_ref, sh_ref, sc_ref, nw_ref, wz_ref, wx_ref, wdt_ref,
                   cw_ref, cb_ref, dtb_ref, z_ref, xs_ref, bm_ref, cm_ref, dt_ref,
                   xe_sc, u0_sc, u1_sc, y0_sc, y1_sc, *, tm, cc):
    u_scs = (u0_sc, u1_sc)
    y_scs = (y0_sc, y1_sc)
    i = pl.program_id(1)
    last = pl.num_programs(1) - 1
    r = tm + 2 * HALO
    _stage_tile(xe_sc, xp_ref, x_ref, xn_ref, tm)

    def mod(v):
        return (_rms(v, nw_ref[...]) * (1.0 + sc_ref[0]) + sh_ref[0]).astype(BF16)

    he = mod(_permuted_rows(xe_sc))
    hm = mod(x_ref[0])
    masks = _edge_masks(r, i == 0, i == last)

    for j in range(D_INNER // cc):
        z_ref[0, :, j * cc:(j + 1) * cc] = _dot(hm, wz_ref[:, j * cc:(j + 1) * cc])

    dt_raw = _dot(hm, wdt_ref[...]) + dtb_ref[...]
    dt_ref[0] = jnp.maximum(dt_raw, 0.0) + jnp.log1p(jnp.exp(-jnp.abs(dt_raw)))

    def down1(a):
        return pltpu.roll(a, 1, 0)

    conv_dim = D_INNER + 2 * BC_DIM
    for j in range(conv_dim // cc):
        lo = j * cc
        u_sc, y_sc = u_scs[j % 2], y_scs[j % 2]
        _store_masked(u_sc, _dot(he, wx_ref[:, lo:lo + cc]), masks)

        def conv(*taps, lo=lo):
            cv = cb_ref[:, lo:lo + cc]
            for k in range(4):
                cv = cv + taps[k] * cw_ref[k:k + 1, lo:lo + cc]
            return _silu(cv)

        act = jnp.concatenate([
            conv(down1(u_sc[r - 16:r - 8]), down1(u_sc[r - 8:r]), u_sc[0:8], u_sc[8:16]),
            conv(down1(u_sc[r - 8:r]), u_sc[0:8], u_sc[8:16], u_sc[16:24]),
            conv(u_sc[0:r - 24], u_sc[8:r - 16], u_sc[16:r - 8], u_sc[24:r]),
            conv(u_sc[r - 24:r - 16], u_sc[r - 16:r - 8], u_sc[r - 8:r], pltpu.roll(u_sc[0:8], 7, 0)),
        ], axis=0)
        _unpermute_into(y_sc, act)
        for c in range(cc // LANES):
            col = lo + c * LANES
            blk = y_sc[c, HALO:HALO + tm]
            if col < D_INNER:
                xs_ref[0, :, col:col + LANES] = blk
            elif col < D_INNER + BC_DIM:
                bm_ref[0, :, col - D_INNER:col - D_INNER + LANES] = blk.astype(BF16)
            else:
                cm_ref[0, :, col - D_INNER - BC_DIM:col - D_INNER - BC_DIM + LANES] = blk.astype(BF16)


def _ssm_in(x, shift, scale, nw, wz, wx, wdt, cw, cb, dtb, tm):
    b, t, d = x.shape
    shared = shift.shape[0] == 1
    cc = 512
    xm, xp, xn = _halo_specs(tm, d, t)
    outs = [(D_INNER, F32), (D_INNER, F32), (BC_DIM, BF16), (BC_DIM, BF16), (2 * LANES, F32)]
    return pl.pallas_call(
        functools.partial(_ssm_in_kernel, tm=tm, cc=cc),
        grid=(b, t // tm),
        in_specs=[
            xm, xp, xn, _row_spec(d, shared), _row_spec(d, shared), _const_spec((1, d)),
            _const_spec(wz.shape), _const_spec(wx.shape), _const_spec(wdt.shape),
            _const_spec(cw.shape), _const_spec(cb.shape), _const_spec(dtb.shape),
        ],
        out_specs=[pl.BlockSpec((1, tm, c), lambda bb, i: (bb, i, 0)) for c, _ in outs],
        out_shape=[jax.ShapeDtypeStruct((b, t, c), dt) for c, dt in outs],
        scratch_shapes=[pltpu.VMEM((d // LANES, tm + 2 * HALO, LANES), F32)]
        + [pltpu.VMEM((tm + 2 * HALO, cc), F32)] * 2 + [pltpu.VMEM((cc // LANES, tm + 2 * HALO, LANES), F32)] * 2,
        compiler_params=_params("parallel", "parallel"),
        name="ssm_in",
    )(x, x, x, shift, scale, nw, wz, wx, wdt, cw, cb, dtb)


DT_REP = 3


def _split3(v, lane):
    p1 = v.astype(BF16).astype(F32)
    r1 = v - p1
    p2 = r1.astype(BF16).astype(F32)
    r2 = r1 - p2
    packed = jnp.where(lane < 32, p1, jnp.where(lane < 64, p2, jnp.where(lane < 96, r2, 0.0)))
    return packed.astype(BF16)


def _sel_matrix(width):
    r = jnp.arange(LANES)[:, None]
    c = jnp.arange(SSM_HEADS * width)[None, :]
    return ((c // width == r % SSM_HEADS) & (r < DT_REP * SSM_HEADS)).astype(BF16)


def _ssd_chunk(xs_ref, dt_ref, bm_ref, cm_ref, rows, a_row, selx_ref, h_sc, reverse, y_store):
    q = SSD_CHUNK
    row = lax.broadcasted_iota(jnp.int32, (q, q), 0)
    col = lax.broadcasted_iota(jnp.int32, (q, q), 1)
    mask = (col >= row) if reverse else (col <= row)
    tri = jnp.where(mask, 1.0, 0.0).astype(BF16)
    lane = lax.broadcasted_iota(jnp.int32, (1, LANES), 1)
    lane_lo = lane < SSM_HEAD_DIM
    end = 0 if reverse else q - 1

    dt = dt_ref[0, rows]
    a = dt * a_row
    a1 = a.astype(BF16)
    r1 = a - a1.astype(F32)
    a2 = r1.astype(BF16)
    a3 = (r1 - a2.astype(F32)).astype(BF16)
    acum = _dot(tri, a1) + _dot(tri, a2) + _dot(tri, a3)
    dt_p = _split3(dt, lane)
    if y_store is not None:
        acum_t = acum.T

    for g in range(SSM_GROUPS):
        gs = slice(g * D_STATE, (g + 1) * D_STATE)
        b_g = bm_ref[0, rows, gs]
        c_g = cm_ref[0, rows, gs]
        h_g = h_sc[g]
        dt_x = _dot(dt_p, selx_ref[:, g * GROUP_W:(g + 1) * GROUP_W])
        if y_store is not None:
            cb = lax.dot_general(c_g, b_g, (((1,), (1,)), ((), ())), preferred_element_type=F32)
            y_inter = _dot(c_g, h_g.astype(BF16))
        xd_parts = []
        etot_parts = []
        for j in range(HEADS_PER_GROUP // 2):
            hd = g * HEADS_PER_GROUP + 2 * j
            pair = hd // 2
            pl_ = slice(j * LANES, (j + 1) * LANES)
            d2 = [jnp.broadcast_to(acum[:, hd + k:hd + k + 1], (q, q)) for k in range(2)]
            dmix = jnp.where(lane_lo, d2[0], d2[1])
            tot = dmix[end:end + 1]
            xdt = xs_ref[0, rows, pair * LANES:(pair + 1) * LANES] * dt_x[:, pl_]
            xd_parts.append((xdt * jnp.exp2(tot - dmix)).astype(BF16))
            etot_parts.append(jnp.exp2(tot))
            if y_store is not None:
                lm = []
                for k in range(2):
                    seg = d2[k] - acum_t[hd + k:hd + k + 1, :]
                    lm.append(cb * jnp.exp2(jnp.where(mask, seg, -jnp.inf)))
                m2 = jnp.concatenate(lm, axis=1).astype(BF16)
                xb = xdt.astype(BF16)
                zero = jnp.zeros_like(xb)
                rhs = jnp.concatenate([jnp.where(lane_lo, xb, zero), jnp.where(lane_lo, zero, xb)], axis=0)
                y_intra = _dot(m2, rhs)
                y_store(pair, y_intra + y_inter[:, pl_] * jnp.exp2(dmix))
        xd = jnp.concatenate(xd_parts, axis=1)
        upd = lax.dot_general(b_g, xd, (((0,), (0,)), ((), ())), preferred_element_type=F32)
        h_sc[g] = jnp.concatenate(etot_parts, axis=1) * h_g + upd


def _ssd_state_kernel(xs_ref, dtf_ref, dtb_ref, bm_ref, cm_ref, af_ref, ab_ref, selx_ref,
                      hf_ref, hb_ref, h_sc, *, nchunks):
    q = SSD_CHUNK
    for reverse, dt_ref, a_ref, out_ref in ((False, dtf_ref, af_ref, hf_ref), (True, dtb_ref, ab_ref, hb_ref)):
        h_sc[...] = jnp.zeros(h_sc.shape, F32)
        order = range(nchunks - 1, -1, -1) if reverse else range(nchunks)
        for ci in order:
            rows = slice(ci * q, (ci + 1) * q)
            _ssd_chunk(xs_ref, dt_ref, bm_ref, cm_ref, rows, a_ref[...], selx_ref, h_sc, reverse, None)
        out_ref[0] = h_sc[...]


def _dir_spec(rev):
    return pl.BlockSpec((1, LANES), lambda *_: (0, rev))


def _ssd_states(xs, dt, bm, cm, a_row, selx):
    b, t, _ = xs.shape
    st = jax.ShapeDtypeStruct((b, SSM_GROUPS, D_STATE, GROUP_W), F32)
    full = lambda c, j=0: pl.BlockSpec((1, t, c), lambda bb: (bb, 0, j))
    st_spec = pl.BlockSpec((1, SSM_GROUPS, D_STATE, GROUP_W), lambda bb: (bb, 0, 0, 0))
    return pl.pallas_call(
        functools.partial(_ssd_state_kernel, nchunks=t // SSD_CHUNK),
        grid=(b,),
        in_specs=[full(D_INNER), full(LANES, 0), full(LANES, 1), full(BC_DIM), full(BC_DIM),
                  _dir_spec(0), _dir_spec(1), _const_spec(selx.shape)],
        out_specs=[st_spec, st_spec],
        out_shape=[st, st],
        scratch_shapes=[pltpu.VMEM((SSM_GROUPS, D_STATE, GROUP_W), F32)],
        compiler_params=_params("parallel"),
        name="ssd_ctx_states",
    )(xs, dt, dt, bm, cm, a_row, a_row, selx)


def _ssd_fwd_kernel(xs_ref, dt_ref, bm_ref, cm_ref, a_ref, selx_ref, h0_ref, y_ref, h_sc, *, cps):
    q = SSD_CHUNK

    @pl.when(pl.program_id(1) == 0)
    def _():
        h_sc[...] = h0_ref[0]

    for ci in range(cps):
        rows = slice(ci * q, (ci + 1) * q)

        def store(pair, val, rows=rows):
            y_ref[0, rows, pair * LANES:(pair + 1) * LANES] = val

        _ssd_chunk(xs_ref, dt_ref, bm_ref, cm_ref, rows, a_ref[...], selx_ref, h_sc, False, store)


def _ssd_fwd(xs, dt, bm, cm, a_row, selx, h0, cps):
    b, t, _ = xs.shape
    tm = cps * SSD_CHUNK
    blk = lambda c: pl.BlockSpec((1, tm, c), lambda bb, i: (bb, i, 0))
    return pl.pallas_call(
        functools.partial(_ssd_fwd_kernel, cps=cps),
        grid=(b, t // tm),
        in_specs=[blk(D_INNER), blk(LANES), blk(BC_DIM), blk(BC_DIM), _dir_spec(0),
                  _const_spec(selx.shape),
                  pl.BlockSpec((1, SSM_GROUPS, D_STATE, GROUP_W), lambda bb, i: (bb, 0, 0, 0))],
        out_specs=blk(D_INNER),
        out_shape=jax.ShapeDtypeStruct((b, t, D_INNER), F32),
        scratch_shapes=[pltpu.VMEM((SSM_GROUPS, D_STATE, GROUP_W), F32)],
        compiler_params=_params("parallel", "arbitrary"),
        name="ssd_fwd",
    )(xs, dt, bm, cm, a_row, selx, h0)


def _ssd_bwd_out_kernel(xs_ref, dt_ref, bm_ref, cm_ref, a_ref, selx_ref, h0_ref, yf_ref, z_ref, x_ref,
                        g_ref, dsk_ref, snw_ref, wo_ref, nw_ref, out_ref, h_sc, yb_sc, *, cps):
    q = SSD_CHUNK

    @pl.when(pl.program_id(1) == 0)
    def _():
        h_sc[...] = h0_ref[0]

    for ci in range(cps - 1, -1, -1):
        rows = slice(ci * q, (ci + 1) * q)

        def store(pair, val, rows=rows):
            yb_sc[rows, pair * LANES:(pair + 1) * LANES] = val

        _ssd_chunk(xs_ref, dt_ref, bm_ref, cm_ref, rows, a_ref[...], selx_ref, h_sc, True, store)

    y = yf_ref[0] + yb_sc[...] + dsk_ref[...] * xs_ref[0]
    yn = _rms(y * _silu(z_ref[0]), snw_ref[...])
    o = _dot(yn.astype(BF16), wo_ref[...])
    out_ref[0] = x_ref[0] + g_ref[0] * _rms(o, nw_ref[...])


def _ssd_bwd_out(xs, dt, bm, cm, a_row, selx, h0, yf, z, x, gate, dskip, snw, w_out, nw, cps):
    b, t, d = x.shape
    tm = cps * SSD_CHUNK
    nblk = t // tm
    blk = lambda c, j=0: pl.BlockSpec((1, tm, c), lambda bb, i: (bb, nblk - 1 - i, j))
    return pl.pallas_call(
        functools.partial(_ssd_bwd_out_kernel, cps=cps),
        grid=(b, nblk),
        in_specs=[blk(D_INNER), blk(LANES, 1), blk(BC_DIM), blk(BC_DIM), _dir_spec(1),
                  _const_spec(selx.shape),
                  pl.BlockSpec((1, SSM_GROUPS, D_STATE, GROUP_W), lambda bb, i: (bb, 0, 0, 0)),
                  blk(D_INNER), blk(D_INNER), blk(d),
                  pl.BlockSpec((1, 1, d), lambda bb, i: (bb, 0, 0)),
                  _const_spec((1, D_INNER)), _const_spec((1, D_INNER)), _const_spec((D_INNER, d)),
                  _const_spec((1, d))],
        out_specs=blk(d),
        out_shape=jax.ShapeDtypeStruct((b, t, d), F32),
        scratch_shapes=[pltpu.VMEM((SSM_GROUPS, D_STATE, GROUP_W), F32), pltpu.VMEM((tm, D_INNER), F32)],
        compiler_params=_params("parallel", "arbitrary"),
        name="ssd_bwd_out",
    )(xs, dt, bm, cm, a_row, selx, h0, yf, z, x, gate, dskip, snw, w_out, nw)


def _rope_tables(t):
    rows = t // GRID_W
    row = jnp.repeat(jnp.arange(rows, dtype=F32), GRID_W)
    col = jnp.tile(jnp.arange(GRID_W, dtype=F32), rows)
    half = HEAD_DIM // 2
    inv_freq = ROPE_THETA ** (-jnp.arange(0, half, 2, dtype=F32) / half)
    ang = jnp.concatenate([row[:, None] * inv_freq, col[:, None] * inv_freq], axis=-1)
    cos, sin = jnp.cos(ang), jnp.sin(ang)
    return jnp.concatenate([cos, cos], axis=-1), jnp.concatenate([-sin, sin], axis=-1)


def _tile(t, pref):
    return pref if t % pref == 0 else t


def kernel(x, c, ctx, c_ctx, ada_w, ada_b, norm_w, attn_w_in, pool_w, pool_scale, q_gain, k_gain, attn_w_out,
           ssm_w_in, ssm_conv_w, ssm_conv_b, ssm_A_log, ssm_dt_bias, ssm_D, ssm_norm_w, ssm_w_out,
           ffn_w_up, ffn_conv_w, ffn_conv_b, ffn_w_down):
    bsz, t, d = x.shape
    lc = ctx.shape[1]
    depth = ada_w.shape[0]
    assert depth == 2 and attn_w_in.shape[0] == 1 and ssm_w_in.shape[0] == 1
    d_ff = ffn_w_down.shape[1]

    rows = -(-(bsz + 1) // HALO) * HALO
    cond = jnp.zeros((rows, d), F32).at[:bsz].set(c).at[bsz].set(c_ctx)
    ada = _adaln(cond, ada_w, ada_b)
    mod_l = [[ada[i, :bsz, k * d:(k + 1) * d].reshape(bsz, 1, d) for k in range(6)] for i in range(depth)]
    mod_c = [[ada[i, bsz:bsz + 1, k * d:(k + 1) * d].reshape(1, 1, d) for k in range(6)] for i in range(depth)]
    nw = lambda i, k: norm_w[i, k].reshape(1, d)

    fc = 256
    nch = d_ff // fc

    def ffn_weights(i):
        wu = ffn_w_up[i].astype(BF16)
        wup = jnp.concatenate([wu[:, :d_ff].reshape(d, nch, fc), wu[:, d_ff:].reshape(d, nch, fc)], axis=-1)
        wup = wup.transpose(1, 0, 2)
        cwt = ffn_conv_w[i].T
        cw = jnp.concatenate([cwt[:, :d_ff].reshape(3, nch, fc), cwt[:, d_ff:].reshape(3, nch, fc)], axis=-1)
        cw = cw.transpose(1, 0, 2)
        cb = jnp.concatenate([ffn_conv_b[i][:d_ff].reshape(nch, 1, fc), ffn_conv_b[i][d_ff:].reshape(nch, 1, fc)],
                             axis=-1)
        wdn = ffn_w_down[i].astype(BF16).reshape(nch, fc, d)
        return wup, cw, cb, wdn

    def ffn(xx, mods, i, tm):
        wup, cw, cb, wdn = ffn_weights(i)
        return _ffn(xx, mods[3], mods[4], mods[5], nw(i, 2), nw(i, 3), wup, cw, cb, wdn, tm)

    perm = jnp.concatenate([jnp.arange(0, HEAD_DIM, 2), jnp.arange(1, HEAD_DIM, 2)])
    cols = jnp.arange(attn_w_in.shape[2])
    qk_lo, qk_hi = POOL_DIM, POOL_DIM + ATTN_DIM + KV_DIM
    qk_cols = (qk_lo + (jnp.arange(qk_hi - qk_lo) // HEAD_DIM) * HEAD_DIM
               + perm[jnp.arange(qk_hi - qk_lo) % HEAD_DIM])
    cols = cols.at[qk_lo:qk_hi].set(qk_cols)
    w_in_a = attn_w_in[0][:, cols].astype(BF16)
    qg = (q_gain[0][perm] * SOFTMAX_C).reshape(1, HEAD_DIM)
    kg = k_gain[0][perm].reshape(1, HEAD_DIM)
    cos2, sin2 = _rope_tables(t)
    eye = jnp.eye(len(POOL_WINDOWS), dtype=F32)
    pool_bd = (eye[:, None, :, None] * pool_w[0][:, :, None, :]).reshape(POOL_DIM, POOL_DIM).astype(BF16)
    ps = pool_scale[0].reshape(1, POOL_DIM)
    w_out_a = attn_w_out[0][:POOL_DIM].astype(BF16)
    w_out_o = attn_w_out[0][POOL_DIM:].astype(BF16)

    tm_l = _tile(t, 512)
    tm_c = _tile(lc, 512)
    a_c, q_c, k_c, v_c = _attn_in(ctx, mod_c[0][0], mod_c[0][1], nw(0, 0), w_in_a, qg, kg, None, None, tm_c)
    a_l, q_l, k_l, v_l = _attn_in(x, mod_l[0][0], mod_l[0][1], nw(0, 0), w_in_a, qg, kg, cos2, sin2, tm_l)
    o_l = _attention(q_l, [(k_c, v_c), (k_l, v_l)], _tile(t, 256), 512)
    o_c = _attention(q_c, [(k_c, v_c)], _tile(lc, 256), 512)
    x = _attn_out(x, mod_l[0][2], nw(0, 1), a_l, o_l, pool_bd, ps, w_out_a, w_out_o, tm_l)
    ctx = _attn_out(ctx, mod_c[0][2], nw(0, 1), a_c, o_c, pool_bd, ps, w_out_a, w_out_o, tm_c)
    x = ffn(x, mod_l[0], 0, tm_l)
    ctx = ffn(ctx, mod_c[0], 0, tm_c)

    w_in_c = ssm_w_in[0].astype(BF16)
    conv_dim = D_INNER + 2 * BC_DIM
    wz = w_in_c[:, :D_INNER]
    wx = w_in_c[:, D_INNER:D_INNER + conv_dim]

    def dir_lanes(v):
        pad = jnp.zeros(v.shape[:-2] + (LANES - DT_REP * SSM_HEADS,), v.dtype)
        return jnp.concatenate([p for r in range(2) for p in [v[..., r, :]] * DT_REP + [pad]], axis=-1)

    wdt = dir_lanes(w_in_c[:, D_INNER + conv_dim:].reshape(d, 2, SSM_HEADS))
    dtb = dir_lanes(ssm_dt_bias[0]).reshape(1, 2 * LANES)
    a_row = dir_lanes(-jnp.exp(ssm_A_log[0].astype(F32)) * math.log2(math.e)).reshape(1, 2 * LANES)
    selx = _sel_matrix(SSM_HEAD_DIM)
    scw = ssm_conv_w[0].T
    scb = ssm_conv_b[0].reshape(1, conv_dim)
    dskip = jnp.repeat(ssm_D[0], SSM_HEAD_DIM).reshape(1, D_INNER)
    snw = ssm_norm_w[0].reshape(1, D_INNER)
    w_out_c = ssm_w_out[0].astype(BF16)

    tm_s = _tile(t, 256)
    tm_sc = _tile(lc, 256)
    _, xs_c, bm_c, cm_c, dt_c = _ssm_in(ctx, mod_c[1][0], mod_c[1][1], nw(1, 0), wz, wx, wdt, scw, scb, dtb, tm_sc)
    hf0, hb0 = _ssd_states(xs_c, dt_c, bm_c, cm_c, a_row, selx)
    z_l, xs_l, bm_l, cm_l, dt_l = _ssm_in(x, mod_l[1][0], mod_l[1][1], nw(1, 0), wz, wx, wdt, scw, scb, dtb, tm_s)
    cps = lambda want: max(c for c in (1, 2, 4) if c <= want and t % (c * SSD_CHUNK) == 0)
    yf = _ssd_fwd(xs_l, dt_l, bm_l, cm_l, a_row, selx, hf0, cps(2))
    x = _ssd_bwd_out(xs_l, dt_l, bm_l, cm_l, a_row, selx, hb0, yf, z_l, x, mod_l[1][2], dskip, snw, w_out_c,
                     nw(1, 1), cps(4))
    x = ffn(x, mod_l[1], 1, tm_l)
    return x
```

```python
import functools
import math

import jax
import jax.numpy as jnp
from jax import lax
from jax.experimental import pallas as pl
from jax.experimental.pallas import tpu as pltpu

F32 = jnp.float32
BF16 = jnp.bfloat16

EPS = 1e-6
GRID_W = 64
ROPE_THETA = 10000.0

HEAD_DIM = 128
N_Q_HEADS = 6
N_KV_HEADS = 2
Q_PER_KV = N_Q_HEADS // N_KV_HEADS
POOL_WINDOWS = (2, 4, 8, 16)
POOL_GROUP_DIM = 64
POOL_DIM = 256
ATTN_DIM = N_Q_HEADS * HEAD_DIM
KV_DIM = N_KV_HEADS * HEAD_DIM

SSM_HEAD_DIM = 64
SSM_HEADS = 32
SSM_GROUPS = 4
HEADS_PER_GROUP = 8
D_STATE = 128
D_INNER = SSM_HEADS * SSM_HEAD_DIM
GROUP_W = HEADS_PER_GROUP * SSM_HEAD_DIM
BC_DIM = SSM_GROUPS * D_STATE
SSD_CHUNK = 128

HALO = 8
LANES = 128
VMEM_LIMIT = 56 * 1024 * 1024


def _params(*sem):
    return pltpu.CompilerParams(dimension_semantics=sem, vmem_limit_bytes=VMEM_LIMIT)


def _const_spec(shape):
    nd = len(shape)
    return pl.BlockSpec(shape, lambda *_: (0,) * nd, pipeline_mode=pl.Buffered(1))


def _row_spec(c, shared):
    if shared:
        return pl.BlockSpec((1, 1, c), lambda b, i: (0, 0, 0))
    return pl.BlockSpec((1, 1, c), lambda b, i: (b, 0, 0))


def _halo_specs(tm, c, t):
    nb = tm // HALO
    last = t // HALO - 1
    main = pl.BlockSpec((1, tm, c), lambda b, i: (b, i, 0))
    prev = pl.BlockSpec((1, HALO, c), lambda b, i: (b, jnp.maximum(i * nb - 1, 0), 0))
    nxt = pl.BlockSpec((1, HALO, c), lambda b, i: (b, jnp.minimum((i + 1) * nb, last), 0))
    return main, prev, nxt


def _rms(xf, w):
    ms = jnp.mean(xf * xf, axis=-1, keepdims=True)
    return xf * lax.rsqrt(ms + EPS) * w


def _silu(x):
    return x * jax.nn.sigmoid(x)


def _dot(a, b):
    return jnp.dot(a, b, preferred_element_type=F32)


def _ada_kernel(c_ref, w_ref, b_ref, o_ref):
    s = _silu(c_ref[...])
    o_ref[0] = _dot(s.astype(BF16), w_ref[0].astype(BF16)) + b_ref[0]


def _adaln(cond, ada_w, ada_b):
    depth, d, n = ada_w.shape
    rows = cond.shape[0]
    tn = 1536
    return pl.pallas_call(
        _ada_kernel,
        grid=(depth, n // tn),
        in_specs=[
            pl.BlockSpec((rows, d), lambda l, j: (0, 0)),
            pl.BlockSpec((1, d, tn), lambda l, j: (l, 0, j)),
            pl.BlockSpec((1, 1, tn), lambda l, j: (l, 0, j)),
        ],
        out_specs=pl.BlockSpec((1, rows, tn), lambda l, j: (l, 0, j)),
        out_shape=jax.ShapeDtypeStruct((depth, rows, n), F32),
        compiler_params=_params("parallel", "parallel"),
        name="adaln",
    )(cond, ada_w, ada_b.reshape(depth, 1, n))


SUM_ROWS = 16


def _attn_in_kernel(*refs, rope, chunk_kv):
    if rope:
        (x_ref, sh_ref, sc_ref, nw_ref, w_ref, qg_ref, kg_ref, cos_ref, sin_ref,
         a_ref, q_ref, k_ref, v_ref) = refs
    else:
        (x_ref, sh_ref, sc_ref, nw_ref, w_ref, qg_ref, kg_ref,
         a_ref, q_ref, k_ref, v_ref) = refs
    tm = x_ref.shape[1]
    nblk = 2 if tm % (2 * 128) == 0 else 1
    rb = tm // nblk
    for blk in range(nblk):
        rows = slice(blk * rb, (blk + 1) * rb)
        h = _rms(x_ref[0, rows], nw_ref[...]) * (1.0 + sc_ref[0]) + sh_ref[0]
        u = _dot(h.astype(BF16), w_ref[...])
        a_ref[0, rows] = u[:, :POOL_DIM]

        def norm_rope(t, gain, rows=rows):
            y = _rms(t, gain)
            if rope:
                y = y * cos_ref[rows] + pltpu.roll(y, HEAD_DIM // 2, 1) * sin_ref[rows]
            return y.astype(BF16)

        off = POOL_DIM
        for hh in range(N_Q_HEADS):
            q_ref[0, rows, hh * HEAD_DIM:(hh + 1) * HEAD_DIM] = norm_rope(
                u[:, off + hh * HEAD_DIM: off + (hh + 1) * HEAD_DIM], qg_ref[...])
        off += ATTN_DIM
        for hh in range(N_KV_HEADS):
            kh = norm_rope(u[:, off + hh * HEAD_DIM: off + (hh + 1) * HEAD_DIM], kg_ref[...])
            vh = u[:, off + KV_DIM + hh * HEAD_DIM: off + KV_DIM + (hh + 1) * HEAD_DIM]
            if chunk_kv:
                k_ref[0, hh, 0, rows] = kh
                v_ref[0, hh, 0, 0:HEAD_DIM, rows] = vh.T.astype(BF16)
            else:
                k_ref[0, rows, hh * HEAD_DIM:(hh + 1) * HEAD_DIM] = kh
                v_ref[0, rows, hh * HEAD_DIM:(hh + 1) * HEAD_DIM] = vh.astype(BF16)
    if chunk_kv:
        for hh in range(N_KV_HEADS):
            v_ref[0, hh, 0, HEAD_DIM:] = jnp.ones((SUM_ROWS, tm), BF16)


def _attn_in(x, shift, scale, nw, w_in, qg, kg, cos2, sin2, tm, chunk_kv=False):
    b, t, d = x.shape
    shared = shift.shape[0] == 1
    rope = cos2 is not None
    width = w_in.shape[1]
    in_specs = [
        pl.BlockSpec((1, tm, d), lambda bb, i: (bb, i, 0)),
        _row_spec(d, shared), _row_spec(d, shared),
        _const_spec((1, d)), _const_spec((d, width)),
        _const_spec((1, HEAD_DIM)), _const_spec((1, HEAD_DIM)),
    ]
    args = [x, shift, scale, nw, w_in, qg, kg]
    if rope:
        in_specs += [pl.BlockSpec((tm, HEAD_DIM), lambda bb, i: (i, 0))] * 2
        args += [cos2, sin2]
    outs = [(POOL_DIM, F32), (ATTN_DIM, BF16)] + ([] if chunk_kv else [(KV_DIM, BF16), (KV_DIM, BF16)])
    out_specs = [pl.BlockSpec((1, tm, c), lambda bb, i: (bb, i, 0)) for c, _ in outs]
    out_shape = [jax.ShapeDtypeStruct((b, t, c), dt) for c, dt in outs]
    if chunk_kv:
        for rows_, cols_ in ((tm, HEAD_DIM), (HEAD_DIM + SUM_ROWS, tm)):
            out_specs.append(pl.BlockSpec((1, N_KV_HEADS, 1, rows_, cols_), lambda bb, i: (bb, 0, i, 0, 0)))
            out_shape.append(jax.ShapeDtypeStruct((b, N_KV_HEADS, t // tm, rows_, cols_), BF16))
    return pl.pallas_call(
        functools.partial(_attn_in_kernel, rope=rope, chunk_kv=chunk_kv),
        grid=(b, t // tm),
        in_specs=in_specs,
        out_specs=out_specs,
        out_shape=out_shape,
        compiler_params=_params("parallel", "parallel"),
        name="attn_in",
    )(*args)


SOFTMAX_C = (HEAD_DIM ** -0.5) * math.log2(math.e)
ATTN_QB = 256
ATTN_UNROLL = 8


def _attn_kernel(q_ref, k_ref, vt_ref, o_ref, m_sc, acc_sc, s0_sc, mc0_sc, s1_sc, mc1_sc, *, n, tq):
    slots = ((s0_sc, mc0_sc), (s1_sc, mc1_sc))

    qb = min(ATTN_QB, tq)
    blocks = [(sub, g) for sub in range(tq // qb) for g in range(Q_PER_KV)]
    q = q_ref[0]
    qs = jnp.concatenate([q[sub * qb:(sub + 1) * qb, g * HEAD_DIM:(g + 1) * HEAD_DIM] for sub, g in blocks], axis=0)
    m_sc[...] = jnp.full(m_sc.shape, -jnp.inf, F32)
    acc_sc[...] = jnp.zeros(acc_sc.shape, F32)

    def produce(j, slot, b):
        s_sc, mc_sc = slots[slot]
        cols = slice(b * qb, (b + 1) * qb)
        s = lax.dot_general(k_ref[0, 0, j], qs[cols], (((1,), (1,)), ((), ())),
                            preferred_element_type=F32)
        s_sc[:, cols] = s
        mc_sc[:, cols] = jnp.max(s, axis=0, keepdims=True)

    def consume(j, slot, b):
        s_sc, mc_sc = slots[slot]
        cols = slice(b * qb, (b + 1) * qb)
        m_prev = m_sc[:, cols]
        m_new = jnp.maximum(m_prev, mc_sc[:, cols])
        alpha = jnp.exp2(m_prev - m_new)
        p = jnp.exp2(s_sc[:, cols] - m_new).astype(BF16)
        acc_sc[:, cols] = alpha * acc_sc[:, cols] + _dot(vt_ref[0, 0, j], p)
        m_sc[:, cols] = m_new

    def step(jp, sp, jc, sc):
        for b in range(len(blocks)):
            if jp is not None:
                produce(jp, sp, b)
            if jc is not None:
                consume(jc, sc, b)

    step(0, 0, None, None)

    def body(jj, carry):
        for u in range(ATTN_UNROLL):
            step(ATTN_UNROLL * jj + u + 1, (u + 1) % 2, ATTN_UNROLL * jj + u, u % 2)
        return carry

    trips = (n - 1) // ATTN_UNROLL
    if trips:
        lax.fori_loop(0, trips, body, 0)
    for j in range(trips * ATTN_UNROLL, n - 1):
        step(j + 1, (j + 1) % 2, j, j % 2)
    step(None, None, n - 1, (n - 1) % 2)

    acc = acc_sc[...]
    o_t = acc[:HEAD_DIM] / acc[HEAD_DIM:HEAD_DIM + 1]
    for b, (sub, g) in enumerate(blocks):
        o_ref[0, sub * qb:(sub + 1) * qb, g * HEAD_DIM:(g + 1) * HEAD_DIM] = (
            o_t[:, b * qb:(b + 1) * qb].T.astype(BF16))


def _kv_tail(k, v, tk):
    b, l, _ = k.shape
    pad = tk - l
    assert 0 <= pad <= l
    k = jnp.concatenate([k, k[:, :pad]], axis=1)
    v = jnp.concatenate([v, jnp.zeros((b, pad, KV_DIM), BF16)], axis=1)
    live = (jnp.arange(tk) < l).astype(BF16)
    kc = k.reshape(b, 1, tk, N_KV_HEADS, HEAD_DIM).transpose(0, 3, 1, 2, 4)
    vt = v.reshape(b, 1, tk, N_KV_HEADS, HEAD_DIM).transpose(0, 3, 1, 4, 2)
    ones = jnp.broadcast_to(live.reshape(1, 1, tk), (b, N_KV_HEADS, 1, SUM_ROWS, tk))
    return kc, jnp.concatenate([vt, ones], axis=3)


def _attention(q, kc, vt, tq):
    b, t, _ = q.shape
    gw = Q_PER_KV * HEAD_DIM
    n, tk = kc.shape[2], kc.shape[3]
    m = Q_PER_KV * tq
    kv_spec = lambda a: pl.BlockSpec((1, 1) + a.shape[2:], lambda bb, g, i: (bb, g, 0, 0, 0))
    kv_args = [kc, vt]
    return pl.pallas_call(
        functools.partial(_attn_kernel, n=n, tq=tq),
        grid=(b, N_KV_HEADS, t // tq),
        in_specs=[pl.BlockSpec((1, tq, gw), lambda bb, g, i: (bb, i, g))] + [kv_spec(a) for a in kv_args],
        out_specs=pl.BlockSpec((1, tq, gw), lambda bb, g, i: (bb, i, g)),
        out_shape=jax.ShapeDtypeStruct((b, t, ATTN_DIM), BF16),
        scratch_shapes=[pltpu.VMEM((1, m), F32), pltpu.VMEM((HEAD_DIM + SUM_ROWS, m), F32)]
        + [pltpu.VMEM((tk, m), F32), pltpu.VMEM((1, m), F32)] * 2,
        compiler_params=_params("parallel", "parallel", "arbitrary"),
        name="attention",
    )(q, *kv_args)


def _attn_out_kernel(x_ref, g_ref, nw_ref, a_ref, ap_ref, an_ref, o_ref, pw_ref, ps_ref, wa_ref, wo_ref,
                     out_ref, abuf, *, tm, t):
    i = pl.program_id(1)
    last = pl.num_programs(1) - 1
    abuf[0:HALO] = jnp.where(i == 0, 0.0, ap_ref[0])
    abuf[HALO:HALO + tm] = a_ref[0]
    abuf[HALO + tm:] = jnp.where(i == last, 0.0, an_ref[0])

    lane = lax.broadcasted_iota(jnp.int32, (1, POOL_DIM), 1)
    nblk = 2 if tm % (2 * 128) == 0 else 1
    rb = tm // nblk
    for blk in range(nblk):
        r0 = blk * rb
        rows = slice(r0, r0 + rb)

        def sh(d, r0=r0):
            return abuf[HALO + d + r0:HALO + d + r0 + rb, :]

        x0 = sh(0)
        s2 = sh(-1) + x0
        s4 = s2 + sh(-2) + sh(1)
        s8 = s4 + sh(-4) + sh(-3) + sh(2) + sh(3)
        s16 = s8 + sh(-8) + sh(-7) + sh(-6) + sh(-5) + sh(4) + sh(5) + sh(6) + sh(7)
        pos = i * tm + r0 + lax.broadcasted_iota(jnp.int32, (rb, 1), 0)

        def mean(s, w, pos=pos):
            left = w // 2
            right = w - 1 - left
            cnt = jnp.minimum(pos + right, t - 1) - jnp.maximum(pos - left, 0) + 1
            return s / cnt.astype(F32)

        pm = jnp.where(lane < 64, mean(s2, 2),
                       jnp.where(lane < 128, mean(s4, 4),
                                 jnp.where(lane < 192, mean(s8, 8), mean(s16, 16))))
        p = pm - x0
        py = _dot(p.astype(BF16), pw_ref[...]) * ps_ref[...]
        y = _dot(py.astype(BF16), wa_ref[...]) + _dot(o_ref[0, rows], wo_ref[...])
        out_ref[0, rows] = x_ref[0, rows] + g_ref[0] * _rms(y, nw_ref[...])


def _attn_out(x, gate, nw, a, o, pool_bd, pool_scale, w_a, w_o, tm):
    b, t, d = x.shape
    shared = gate.shape[0] == 1
    am, ap, an = _halo_specs(tm, POOL_DIM, t)
    return pl.pallas_call(
        functools.partial(_attn_out_kernel, tm=tm, t=t),
        grid=(b, t // tm),
        in_specs=[
            pl.BlockSpec((1, tm, d), lambda bb, i: (bb, i, 0)),
            _row_spec(d, shared), _const_spec((1, d)),
            am, ap, an,
            pl.BlockSpec((1, tm, ATTN_DIM), lambda bb, i: (bb, i, 0)),
            _const_spec((POOL_DIM, POOL_DIM)), _const_spec((1, POOL_DIM)),
            _const_spec((POOL_DIM, d)), _const_spec((ATTN_DIM, d)),
        ],
        out_specs=pl.BlockSpec((1, tm, d), lambda bb, i: (bb, i, 0)),
        out_shape=jax.ShapeDtypeStruct((b, t, d), F32),
        scratch_shapes=[pltpu.VMEM((tm + 2 * HALO, POOL_DIM), F32)],
        compiler_params=_params("parallel", "parallel"),
        name="attn_out",
    )(x, gate, nw, a, a, a, o, pool_bd, pool_scale, w_a, w_o)


def _stage_tile(xe_sc, xp_ref, x_ref, xn_ref, tm):
    for c in range(xe_sc.shape[0]):
        cols = slice(c * LANES, (c + 1) * LANES)
        xe_sc[c, 0:HALO] = xp_ref[0, :, cols]
        xe_sc[c, HALO:HALO + tm] = x_ref[0, :, cols]
        xe_sc[c, HALO + tm:] = xn_ref[0, :, cols]


def _permuted_rows(xe_sc):
    nslab, r, _ = xe_sc.shape
    nv = r // 8
    return jnp.concatenate(
        [jnp.concatenate([xe_sc[c, pl.ds(v, 8, stride=nv), :] for c in range(nslab)], axis=1) for v in range(nv)],
        axis=0)


def _unpermute_into(ye_sc, val):
    nslab, r, _ = ye_sc.shape
    nv = r // 8
    for c in range(nslab):
        for v in range(nv):
            ye_sc[c, pl.ds(v, 8, stride=nv), :] = val[8 * v:8 * v + 8, c * LANES:(c + 1) * LANES]


def _edge_masks(r, at_start, at_end):
    sub = lax.broadcasted_iota(jnp.int32, (8 * HALO, 1), 0) % 8
    top = jnp.where(jnp.logical_and(sub == 0, at_start), 0.0, 1.0)
    bot = jnp.where(jnp.logical_and(sub == 7, at_end), 0.0, 1.0)
    return top, bot


def _store_masked(u_sc, u, masks):
    r = u.shape[0]
    edge = 8 * HALO
    u_sc[0:edge] = u[0:edge] * masks[0]
    u_sc[edge:r - edge] = u[edge:r - edge]
    u_sc[r - edge:] = u[r - edge:] * masks[1]


FFN_UNROLL = 2


def _ffn_kernel(x_ref, xp_ref, xn_ref, sh_ref, sc_ref, g_ref, nw2_ref, nw3_ref,
                wup_ref, cw_ref, cb_ref, wdn_ref, out_ref, xe_sc, h_sc, u0_sc, u1_sc, acc_sc, *, tm, fc, nch):
    i = pl.program_id(1)
    last = pl.num_programs(1) - 1
    r = tm + 2 * HALO
    xe_sc[0:HALO] = xp_ref[0]
    xe_sc[HALO:HALO + tm] = x_ref[0]
    xe_sc[HALO + tm:] = xn_ref[0]
    acc_sc[...] = jnp.zeros(acc_sc.shape, F32)
    keep_top = jnp.where(i == 0, 0.0, 1.0)
    keep_bot = jnp.where(i == last, 0.0, 1.0)
    slots = (u0_sc, u1_sc)
    half = tm // 2 if tm % 32 == 0 else tm

    def produce(j, slot, lo=0, hi=r):
        u = _dot(h_sc[lo:hi], wup_ref[j])
        u_sc = slots[slot]
        a, b = lo, hi
        if lo == 0:
            u_sc[0:HALO] = u[0:HALO] * keep_top
            a = HALO
        if hi == r:
            u_sc[r - HALO:r] = u[r - HALO - lo:r - lo] * keep_bot
            b = r - HALO
        u_sc[a:b] = u[a - lo:b - lo]

    def consume(j, slot, lo=0, hi=tm):
        u_sc = slots[slot]
        cw = cw_ref[j]
        cv = cb_ref[j]
        for k in range(3):
            cv = cv + u_sc[HALO - 1 + k + lo:HALO - 1 + k + hi, :] * cw[k:k + 1]
        g = _silu(cv[:, fc:]) * cv[:, :fc]
        acc_sc[lo:hi] += _dot(g.astype(BF16), wdn_ref[j])

    for lo, hi in ((0, half), (half, r)):
        h = _rms(xe_sc[lo:hi], nw2_ref[...]) * (1.0 + sc_ref[0]) + sh_ref[0]
        h_sc[lo:hi] = h.astype(BF16)
        produce(0, 0, lo, hi)

    def body(jj, carry):
        for u in range(FFN_UNROLL):
            produce(FFN_UNROLL * jj + u + 1, (u + 1) % 2)
            consume(FFN_UNROLL * jj + u, u % 2)
        return carry

    trips = (nch - 1) // FFN_UNROLL
    if trips > 1:
        lax.fori_loop(0, trips, body, 0)
    elif trips == 1:
        body(0, 0)
    for j in range(trips * FFN_UNROLL, nch - 1):
        produce(j + 1, (j + 1) % 2)
        consume(j, j % 2)
    for lo, hi in ((0, half), (half, tm)) if half < tm else ((0, tm),):
        consume(nch - 1, (nch - 1) % 2, lo, hi)
        out_ref[0, lo:hi] = x_ref[0, lo:hi] + g_ref[0] * _rms(acc_sc[lo:hi], nw3_ref[...])


def _ffn(x, shift, scale, gate, nw2, nw3, wup, cw, cb, wdn, tm):
    b, t, d = x.shape
    shared = shift.shape[0] == 1
    nch, _, fc2 = wup.shape
    fc = fc2 // 2
    xm, xp, xn = _halo_specs(tm, d, t)
    return pl.pallas_call(
        functools.partial(_ffn_kernel, tm=tm, fc=fc, nch=nch),
        grid=(b, t // tm),
        in_specs=[
            xm, xp, xn,
            _row_spec(d, shared), _row_spec(d, shared), _row_spec(d, shared),
            _const_spec((1, d)), _const_spec((1, d)),
            _const_spec(wup.shape), _const_spec(cw.shape), _const_spec(cb.shape), _const_spec(wdn.shape),
        ],
        out_specs=pl.BlockSpec((1, tm, d), lambda bb, i: (bb, i, 0)),
        out_shape=jax.ShapeDtypeStruct((b, t, d), F32),
        scratch_shapes=[
            pltpu.VMEM((tm + 2 * HALO, d), F32),
            pltpu.VMEM((tm + 2 * HALO, d), BF16),
            pltpu.VMEM((tm + 2 * HALO, fc2), F32),
            pltpu.VMEM((tm + 2 * HALO, fc2), F32),
            pltpu.VMEM((tm, d), F32),
        ],
        compiler_params=_params("parallel", "parallel"),
        name="conv_ffn",
    )(x, x, x, shift, scale, gate, nw2, nw3, wup, cw, cb, wdn)


def _ssm_in_kernel(x_ref, xp_ref, xn_ref, sh_ref, sc_ref, nw_ref, wz_ref, wx_ref, wdt_ref,
                   cw_ref, cb_ref, dtb_ref, z_ref, xs_ref, bm_ref, cm_ref, dt_ref,
                   xe_sc, u0_sc, u1_sc, y0_sc, y1_sc, *, tm, cc):
    u_scs = (u0_sc, u1_sc)
    y_scs = (y0_sc, y1_sc)
    i = pl.program_id(1)
    last = pl.num_programs(1) - 1
    r = tm + 2 * HALO
    _stage_tile(xe_sc, xp_ref, x_ref, xn_ref, tm)

    def mod(v):
        return (_rms(v, nw_ref[...]) * (1.0 + sc_ref[0]) + sh_ref[0]).astype(BF16)

    he = mod(_permuted_rows(xe_sc))
    hm = mod(x_ref[0])
    masks = _edge_masks(r, i == 0, i == last)

    for j in range(D_INNER // cc):
        z_ref[0, :, j * cc:(j + 1) * cc] = _dot(hm, wz_ref[:, j * cc:(j + 1) * cc])

    dt_raw = _dot(hm, wdt_ref[...]) + dtb_ref[...]
    dt_ref[0] = jnp.maximum(dt_raw, 0.0) + jnp.log1p(jnp.exp(-jnp.abs(dt_raw)))

    def down1(a):
        return pltpu.roll(a, 1, 0)

    conv_dim = D_INNER + 2 * BC_DIM
    for j in range(conv_dim // cc):
        lo = j * cc
        u_sc, y_sc = u_scs[j % 2], y_scs[j % 2]
        _store_masked(u_sc, _dot(he, wx_ref[:, lo:lo + cc]), masks)

        def conv(*taps, lo=lo):
            cv = cb_ref[:, lo:lo + cc]
            for k in range(4):
                cv = cv + taps[k] * cw_ref[k:k + 1, lo:lo + cc]
            return _silu(cv)

        act = jnp.concatenate([
            conv(down1(u_sc[r - 16:r - 8]), down1(u_sc[r - 8:r]), u_sc[0:8], u_sc[8:16]),
            conv(down1(u_sc[r - 8:r]), u_sc[0:8], u_sc[8:16], u_sc[16:24]),
            conv(u_sc[0:r - 24], u_sc[8:r - 16], u_sc[16:r - 8], u_sc[24:r]),
            conv(u_sc[r - 24:r - 16], u_sc[r - 16:r - 8], u_sc[r - 8:r], pltpu.roll(u_sc[0:8], 7, 0)),
        ], axis=0)
        _unpermute_into(y_sc, act)
        for c in range(cc // LANES):
            col = lo + c * LANES
            blk = y_sc[c, HALO:HALO + tm]
            if col < D_INNER:
                xs_ref[0, :, col:col + LANES] = blk
            elif col < D_INNER + BC_DIM:
                bm_ref[0, :, col - D_INNER:col - D_INNER + LANES] = blk.astype(BF16)
            else:
                cm_ref[0, :, col - D_INNER - BC_DIM:col - D_INNER - BC_DIM + LANES] = blk.astype(BF16)


def _ssm_in(x, shift, scale, nw, wz, wx, wdt, cw, cb, dtb, tm):
    b, t, d = x.shape
    shared = shift.shape[0] == 1
    cc = 512
    xm, xp, xn = _halo_specs(tm, d, t)
    outs = [(D_INNER, F32), (D_INNER, F32), (BC_DIM, BF16), (BC_DIM, BF16), (2 * LANES, F32)]
    return pl.pallas_call(
        functools.partial(_ssm_in_kernel, tm=tm, cc=cc),
        grid=(b, t // tm),
        in_specs=[
            xm, xp, xn, _row_spec(d, shared), _row_spec(d, shared), _const_spec((1, d)),
            _const_spec(wz.shape), _const_spec(wx.shape), _const_spec(wdt.shape),
            _const_spec(cw.shape), _const_spec(cb.shape), _const_spec(dtb.shape),
        ],
        out_specs=[pl.BlockSpec((1, tm, c), lambda bb, i: (bb, i, 0)) for c, _ in outs],
        out_shape=[jax.ShapeDtypeStruct((b, t, c), dt) for c, dt in outs],
        scratch_shapes=[pltpu.VMEM((d // LANES, tm + 2 * HALO, LANES), F32)]
        + [pltpu.VMEM((tm + 2 * HALO, cc), F32)] * 2 + [pltpu.VMEM((cc // LANES, tm + 2 * HALO, LANES), F32)] * 2,
        compiler_params=_params("parallel", "parallel"),
        name="ssm_in",
    )(x, x, x, shift, scale, nw, wz, wx, wdt, cw, cb, dtb)


DT_REP = 3


def _split3(v, lane):
    p1 = v.astype(BF16).astype(F32)
    r1 = v - p1
    p2 = r1.astype(BF16).astype(F32)
    r2 = r1 - p2
    packed = jnp.where(lane < 32, p1, jnp.where(lane < 64, p2, jnp.where(lane < 96, r2, 0.0)))
    return packed.astype(BF16)


def _sel_matrix(width):
    r = jnp.arange(LANES)[:, None]
    c = jnp.arange(SSM_HEADS * width)[None, :]
    return ((c // width == r % SSM_HEADS) & (r < DT_REP * SSM_HEADS)).astype(BF16)


def _ssd_chunk(xs_ref, dt_ref, bm_ref, cm_ref, rows, a_row, selx_ref, h_sc, reverse, y_store):
    q = SSD_CHUNK
    row = lax.broadcasted_iota(jnp.int32, (q, q), 0)
    col = lax.broadcasted_iota(jnp.int32, (q, q), 1)
    mask = (col >= row) if reverse else (col <= row)
    tri = jnp.where(mask, 1.0, 0.0).astype(BF16)
    lane = lax.broadcasted_iota(jnp.int32, (1, LANES), 1)
    lane_lo = lane < SSM_HEAD_DIM
    end = 0 if reverse else q - 1

    dt = dt_ref[0, rows]
    a = dt * a_row
    a1 = a.astype(BF16)
    r1 = a - a1.astype(F32)
    a2 = r1.astype(BF16)
    a3 = (r1 - a2.astype(F32)).astype(BF16)
    acum = _dot(tri, a1) + _dot(tri, a2) + _dot(tri, a3)
    dt_p = _split3(dt, lane)
    if y_store is not None:
        acum_t = acum.T

    for g in range(SSM_GROUPS):
        gs = slice(g * D_STATE, (g + 1) * D_STATE)
        b_g = bm_ref[0, rows, gs]
        c_g = cm_ref[0, rows, gs]
        h_g = h_sc[g]
        dt_x = _dot(dt_p, selx_ref[:, g * GROUP_W:(g + 1) * GROUP_W])
        if y_store is not None:
            cb = lax.dot_general(c_g, b_g, (((1,), (1,)), ((), ())), preferred_element_type=F32)
            y_inter = _dot(c_g, h_g.astype(BF16))
        xd_parts = []
        etot_parts = []
        for j in range(HEADS_PER_GROUP // 2):
            hd = g * HEADS_PER_GROUP + 2 * j
            pair = hd // 2
            pl_ = slice(j * LANES, (j + 1) * LANES)
            d2 = [jnp.broadcast_to(acum[:, hd + k:hd + k + 1], (q, q)) for k in range(2)]
            dmix = jnp.where(lane_lo, d2[0], d2[1])
            tot = dmix[end:end + 1]
            xdt = xs_ref[0, rows, pair * LANES:(pair + 1) * LANES] * dt_x[:, pl_]
            xd_parts.append((xdt * jnp.exp2(tot - dmix)).astype(BF16))
            etot_parts.append(jnp.exp2(tot))
            if y_store is not None:
                lm = []
                for k in range(2):
                    seg = d2[k] - acum_t[hd + k:hd + k + 1, :]
                    lm.append(cb * jnp.exp2(jnp.where(mask, seg, -jnp.inf)))
                m2 = jnp.concatenate(lm, axis=1).astype(BF16)
                xb = xdt.astype(BF16)
                zero = jnp.zeros_like(xb)
                rhs = jnp.concatenate([jnp.where(lane_lo, xb, zero), jnp.where(lane_lo, zero, xb)], axis=0)
                y_intra = _dot(m2, rhs)
                y_store(pair, y_intra + y_inter[:, pl_] * jnp.exp2(dmix))
        xd = jnp.concatenate(xd_parts, axis=1)
        upd = lax.dot_general(b_g, xd, (((0,), (0,)), ((), ())), preferred_element_type=F32)
        h_sc[g] = jnp.concatenate(etot_parts, axis=1) * h_g + upd


def _ssd_state_kernel(xs_ref, dtf_ref, dtb_ref, bm_ref, cm_ref, af_ref, ab_ref, selx_ref,
                      hf_ref, hb_ref, h_sc, *, nchunks):
    q = SSD_CHUNK
    for reverse, dt_ref, a_ref, out_ref in ((False, dtf_ref, af_ref, hf_ref), (True, dtb_ref, ab_ref, hb_ref)):
        h_sc[...] = jnp.zeros(h_sc.shape, F32)
        order = range(nchunks - 1, -1, -1) if reverse else range(nchunks)
        for ci in order:
            rows = slice(ci * q, (ci + 1) * q)
            _ssd_chunk(xs_ref, dt_ref, bm_ref, cm_ref, rows, a_ref[...], selx_ref, h_sc, reverse, None)
        out_ref[0] = h_sc[...]


def _dir_spec(rev):
    return pl.BlockSpec((1, LANES), lambda *_: (0, rev))


def _ssd_states(xs, dt, bm, cm, a_row, selx):
    b, t, _ = xs.shape
    st = jax.ShapeDtypeStruct((b, SSM_GROUPS, D_STATE, GROUP_W), F32)
    full = lambda c, j=0: pl.BlockSpec((1, t, c), lambda bb: (bb, 0, j))
    st_spec = pl.BlockSpec((1, SSM_GROUPS, D_STATE, GROUP_W), lambda bb: (bb, 0, 0, 0))
    return pl.pallas_call(
        functools.partial(_ssd_state_kernel, nchunks=t // SSD_CHUNK),
        grid=(b,),
        in_specs=[full(D_INNER), full(LANES, 0), full(LANES, 1), full(BC_DIM), full(BC_DIM),
                  _dir_spec(0), _dir_spec(1), _const_spec(selx.shape)],
        out_specs=[st_spec, st_spec],
        out_shape=[st, st],
        scratch_shapes=[pltpu.VMEM((SSM_GROUPS, D_STATE, GROUP_W), F32)],
        compiler_params=_params("parallel"),
        name="ssd_ctx_states",
    )(xs, dt, dt, bm, cm, a_row, a_row, selx)


def _ssd_fwd_kernel(xs_ref, dt_ref, bm_ref, cm_ref, a_ref, selx_ref, h0_ref, y_ref, h_sc, *, cps):
    q = SSD_CHUNK

    @pl.when(pl.program_id(1) == 0)
    def _():
        h_sc[...] = h0_ref[0]

    for ci in range(cps):
        rows = slice(ci * q, (ci + 1) * q)

        def store(pair, val, rows=rows):
            y_ref[0, rows, pair * LANES:(pair + 1) * LANES] = val

        _ssd_chunk(xs_ref, dt_ref, bm_ref, cm_ref, rows, a_ref[...], selx_ref, h_sc, False, store)


def _ssd_fwd(xs, dt, bm, cm, a_row, selx, h0, cps):
    b, t, _ = xs.shape
    tm = cps * SSD_CHUNK
    blk = lambda c: pl.BlockSpec((1, tm, c), lambda bb, i: (bb, i, 0))
    return pl.pallas_call(
        functools.partial(_ssd_fwd_kernel, cps=cps),
        grid=(b, t // tm),
        in_specs=[blk(D_INNER), blk(LANES), blk(BC_DIM), blk(BC_DIM), _dir_spec(0),
                  _const_spec(selx.shape),
                  pl.BlockSpec((1, SSM_GROUPS, D_STATE, GROUP_W), lambda bb, i: (bb, 0, 0, 0))],
        out_specs=blk(D_INNER),
        out_shape=jax.ShapeDtypeStruct((b, t, D_INNER), F32),
        scratch_shapes=[pltpu.VMEM((SSM_GROUPS, D_STATE, GROUP_W), F32)],
        compiler_params=_params("parallel", "arbitrary"),
        name="ssd_fwd",
    )(xs, dt, bm, cm, a_row, selx, h0)


def _ssd_bwd_out_kernel(xs_ref, dt_ref, bm_ref, cm_ref, a_ref, selx_ref, h0_ref, yf_ref, z_ref, x_ref,
                        g_ref, dsk_ref, snw_ref, wo_ref, nw_ref, out_ref, h_sc, yb_sc, *, cps):
    q = SSD_CHUNK

    @pl.when(pl.program_id(1) == 0)
    def _():
        h_sc[...] = h0_ref[0]

    for ci in range(cps - 1, -1, -1):
        rows = slice(ci * q, (ci + 1) * q)

        def store(pair, val, rows=rows):
            yb_sc[rows, pair * LANES:(pair + 1) * LANES] = val

        _ssd_chunk(xs_ref, dt_ref, bm_ref, cm_ref, rows, a_ref[...], selx_ref, h_sc, True, store)

    y = yf_ref[0] + yb_sc[...] + dsk_ref[...] * xs_ref[0]
    yn = _rms(y * _silu(z_ref[0]), snw_ref[...])
    o = _dot(yn.astype(BF16), wo_ref[...])
    out_ref[0] = x_ref[0] + g_ref[0] * _rms(o, nw_ref[...])


def _ssd_bwd_out(xs, dt, bm, cm, a_row, selx, h0, yf, z, x, gate, dskip, snw, w_out, nw, cps):
    b, t, d = x.shape
    tm = cps * SSD_CHUNK
    nblk = t // tm
    blk = lambda c, j=0: pl.BlockSpec((1, tm, c), lambda bb, i: (bb, nblk - 1 - i, j))
    return pl.pallas_call(
        functools.partial(_ssd_bwd_out_kernel, cps=cps),
        grid=(b, nblk),
        in_specs=[blk(D_INNER), blk(LANES, 1), blk(BC_DIM), blk(BC_DIM), _dir_spec(1),
                  _const_spec(selx.shape),
                  pl.BlockSpec((1, SSM_GROUPS, D_STATE, GROUP_W), lambda bb, i: (bb, 0, 0, 0)),
                  blk(D_INNER), blk(D_INNER), blk(d),
                  pl.BlockSpec((1, 1, d), lambda bb, i: (bb, 0, 0)),
                  _const_spec((1, D_INNER)), _const_spec((1, D_INNER)), _const_spec((D_INNER, d)),
                  _const_spec((1, d))],
        out_specs=blk(d),
        out_shape=jax.ShapeDtypeStruct((b, t, d), F32),
        scratch_shapes=[pltpu.VMEM((SSM_GROUPS, D_STATE, GROUP_W), F32), pltpu.VMEM((tm, D_INNER), F32)],
        compiler_params=_params("parallel", "arbitrary"),
        name="ssd_bwd_out",
    )(xs, dt, bm, cm, a_row, selx, h0, yf, z, x, gate, dskip, snw, w_out, nw)


def _rope_tables(t):
    rows = t // GRID_W
    row = jnp.repeat(jnp.arange(rows, dtype=F32), GRID_W)
    col = jnp.tile(jnp.arange(GRID_W, dtype=F32), rows)
    half = HEAD_DIM // 2
    inv_freq = ROPE_THETA ** (-jnp.arange(0, half, 2, dtype=F32) / half)
    ang = jnp.concatenate([row[:, None] * inv_freq, col[:, None] * inv_freq], axis=-1)
    cos, sin = jnp.cos(ang), jnp.sin(ang)
    return jnp.concatenate([cos, cos], axis=-1), jnp.concatenate([-sin, sin], axis=-1)


def _tile(t, pref):
    return pref if t % pref == 0 else t


def kernel(x, c, ctx, c_ctx, ada_w, ada_b, norm_w, attn_w_in, pool_w, pool_scale, q_gain, k_gain, attn_w_out,
           ssm_w_in, ssm_conv_w, ssm_conv_b, ssm_A_log, ssm_dt_bias, ssm_D, ssm_norm_w, ssm_w_out,
           ffn_w_up, ffn_conv_w, ffn_conv_b, ffn_w_down):
    bsz, t, d = x.shape
    lc = ctx.shape[1]
    depth = ada_w.shape[0]
    assert depth == 2 and attn_w_in.shape[0] == 1 and ssm_w_in.shape[0] == 1
    d_ff = ffn_w_down.shape[1]

    rows = -(-(bsz + 1) // HALO) * HALO
    cond = jnp.zeros((rows, d), F32).at[:bsz].set(c).at[bsz].set(c_ctx)
    ada = _adaln(cond, ada_w, ada_b)
    mod_l = [[ada[i, :bsz, k * d:(k + 1) * d].reshape(bsz, 1, d) for k in range(6)] for i in range(depth)]
    mod_c = [[ada[i, bsz:bsz + 1, k * d:(k + 1) * d].reshape(1, 1, d) for k in range(6)] for i in range(depth)]
    nw = lambda i, k: norm_w[i, k].reshape(1, d)

    fc = 256
    nch = d_ff // fc

    def ffn_weights(i):
        wu = ffn_w_up[i].astype(BF16)
        wup = jnp.concatenate([wu[:, :d_ff].reshape(d, nch, fc), wu[:, d_ff:].reshape(d, nch, fc)], axis=-1)
        wup = wup.transpose(1, 0, 2)
        cwt = ffn_conv_w[i].T
        cw = jnp.concatenate([cwt[:, :d_ff].reshape(3, nch, fc), cwt[:, d_ff:].reshape(3, nch, fc)], axis=-1)
        cw = cw.transpose(1, 0, 2)
        cb = jnp.concatenate([ffn_conv_b[i][:d_ff].reshape(nch, 1, fc), ffn_conv_b[i][d_ff:].reshape(nch, 1, fc)],
                             axis=-1)
        wdn = ffn_w_down[i].astype(BF16).reshape(nch, fc, d)
        return wup, cw, cb, wdn

    def ffn(xx, mods, i, tm):
        wup, cw, cb, wdn = ffn_weights(i)
        return _ffn(xx, mods[3], mods[4], mods[5], nw(i, 2), nw(i, 3), wup, cw, cb, wdn, tm)

    perm = jnp.concatenate([jnp.arange(0, HEAD_DIM, 2), jnp.arange(1, HEAD_DIM, 2)])
    cols = jnp.arange(attn_w_in.shape[2])
    qk_lo, qk_hi = POOL_DIM, POOL_DIM + ATTN_DIM + KV_DIM
    qk_cols = (qk_lo + (jnp.arange(qk_hi - qk_lo) // HEAD_DIM) * HEAD_DIM
               + perm[jnp.arange(qk_hi - qk_lo) % HEAD_DIM])
    cols = cols.at[qk_lo:qk_hi].set(qk_cols)
    w_in_a = attn_w_in[0][:, cols].astype(BF16)
    qg = (q_gain[0][perm] * SOFTMAX_C).reshape(1, HEAD_DIM)
    kg = k_gain[0][perm].reshape(1, HEAD_DIM)
    cos2, sin2 = _rope_tables(t)
    eye = jnp.eye(len(POOL_WINDOWS), dtype=F32)
    pool_bd = (eye[:, None, :, None] * pool_w[0][:, :, None, :]).reshape(POOL_DIM, POOL_DIM).astype(BF16)
    ps = pool_scale[0].reshape(1, POOL_DIM)
    w_out_a = attn_w_out[0][:POOL_DIM].astype(BF16)
    w_out_o = attn_w_out[0][POOL_DIM:].astype(BF16)

    tm_l = _tile(t, 512)
    tm_c = _tile(lc, 512)
    a_c, q_c, k_c, v_c = _attn_in(ctx, mod_c[0][0], mod_c[0][1], nw(0, 0), w_in_a, qg, kg, None, None, tm_c)
    a_l, q_l, kc_l, vt_l = _attn_in(x, mod_l[0][0], mod_l[0][1], nw(0, 0), w_in_a, qg, kg, cos2, sin2, tm_l,
                                    chunk_kv=True)
    assert lc <= tm_l
    kc_t, vt_t = _kv_tail(k_c, v_c, tm_l)
    o_l = _attention(q_l, jnp.concatenate([kc_l, kc_t], axis=2), jnp.concatenate([vt_l, vt_t], axis=2),
                     _tile(t, 256))
    o_c = _attention(q_c, *_kv_tail(k_c, v_c, lc), _tile(lc, 256))
    x = _attn_out(x, mod_l[0][2], nw(0, 1), a_l, o_l, pool_bd, ps, w_out_a, w_out_o, tm_l)
    ctx = _attn_out(ctx, mod_c[0][2], nw(0, 1), a_c, o_c, pool_bd, ps, w_out_a, w_out_o, tm_c)
    x = ffn(x, mod_l[0], 0, tm_l)
    ctx = ffn(ctx, mod_c[0], 0, tm_c)

    w_in_c = ssm_w_in[0].astype(BF16)
    conv_dim = D_INNER + 2 * BC_DIM
    wz = w_in_c[:, :D_INNER]
    wx = w_in_c[:, D_INNER:D_INNER + conv_dim]

    def dir_lanes(v):
        pad = jnp.zeros(v.shape[:-2] + (LANES - DT_REP * SSM_HEADS,), v.dtype)
        return jnp.concatenate([p for r in range(2) for p in [v[..., r, :]] * DT_REP + [pad]], axis=-1)

    wdt = dir_lanes(w_in_c[:, D_INNER + conv_dim:].reshape(d, 2, SSM_HEADS))
    dtb = dir_lanes(ssm_dt_bias[0]).reshape(1, 2 * LANES)
    a_row = dir_lanes(-jnp.exp(ssm_A_log[0].astype(F32)) * math.log2(math.e)).reshape(1, 2 * LANES)
    selx = _sel_matrix(SSM_HEAD_DIM)
    scw = ssm_conv_w[0].T
    scb = ssm_conv_b[0].reshape(1, conv_dim)
    dskip = jnp.repeat(ssm_D[0], SSM_HEAD_DIM).reshape(1, D_INNER)
    snw = ssm_norm_w[0].reshape(1, D_INNER)
    w_out_c = ssm_w_out[0].astype(BF16)

    tm_s = _tile(t, 256)
    tm_sc = _tile(lc, 256)
    _, xs_c, bm_c, cm_c, dt_c = _ssm_in(ctx, mod_c[1][0], mod_c[1][1], nw(1, 0), wz, wx, wdt, scw, scb, dtb, tm_sc)
    hf0, hb0 = _ssd_states(xs_c, dt_c, bm_c, cm_c, a_row, selx)
    z_l, xs_l, bm_l, cm_l, dt_l = _ssm_in(x, mod_l[1][0], mod_l[1][1], nw(1, 0), wz, wx, wdt, scw, scb, dtb, tm_s)
    cps = lambda want: max(c for c in (1, 2, 4) if c <= want and t % (c * SSD_CHUNK) == 0)
    yf = _ssd_fwd(xs_l, dt_l, bm_l, cm_l, a_row, selx, hf0, cps(2))
    x = _ssd_bwd_out(xs_l, dt_l, bm_l, cm_l, a_row, selx, hb0, yf, z_l, x, mod_l[1][2], dskip, snw, w_out_c,
                     nw(1, 1), cps(4))
    x = ffn(x, mod_l[1], 1, tm_l)
    return x
```

```python
import functools
import math

import jax
import jax.numpy as jnp
from jax import lax
from jax.experimental import pallas as pl
from jax.experimental.pallas import tpu as pltpu

F32 = jnp.float32
BF16 = jnp.bfloat16

EPS = 1e-6
GRID_W = 64
ROPE_THETA = 10000.0

HEAD_DIM = 128
N_Q_HEADS = 6
N_KV_HEADS = 2
Q_PER_KV = N_Q_HEADS // N_KV_HEADS
POOL_WINDOWS = (2, 4, 8, 16)
POOL_GROUP_DIM = 64
POOL_DIM = 256
ATTN_DIM = N_Q_HEADS * HEAD_DIM
KV_DIM = N_KV_HEADS * HEAD_DIM

SSM_HEAD_DIM = 64
SSM_HEADS = 32
SSM_GROUPS = 4
HEADS_PER_GROUP = 8
D_STATE = 128
D_INNER = SSM_HEADS * SSM_HEAD_DIM
GROUP_W = HEADS_PER_GROUP * SSM_HEAD_DIM
BC_DIM = SSM_GROUPS * D_STATE
SSD_CHUNK = 128

HALO = 8
LANES = 128
VMEM_LIMIT = 56 * 1024 * 1024


def _params(*sem):
    return pltpu.CompilerParams(dimension_semantics=sem, vmem_limit_bytes=VMEM_LIMIT)


def _const_spec(shape):
    nd = len(shape)
    return pl.BlockSpec(shape, lambda *_: (0,) * nd, pipeline_mode=pl.Buffered(1))


def _row_spec(c, shared):
    if shared:
        return pl.BlockSpec((1, 1, c), lambda b, i: (0, 0, 0))
    return pl.BlockSpec((1, 1, c), lambda b, i: (b, 0, 0))


def _halo_specs(tm, c, t):
    nb = tm // HALO
    last = t // HALO - 1
    main = pl.BlockSpec((1, tm, c), lambda b, i: (b, i, 0))
    prev = pl.BlockSpec((1, HALO, c), lambda b, i: (b, jnp.maximum(i * nb - 1, 0), 0))
    nxt = pl.BlockSpec((1, HALO, c), lambda b, i: (b, jnp.minimum((i + 1) * nb, last), 0))
    return main, prev, nxt


def _rms(xf, w):
    ms = jnp.mean(xf * xf, axis=-1, keepdims=True)
    return xf * lax.rsqrt(ms + EPS) * w


def _silu(x):
    return x * jax.nn.sigmoid(x)


def _dot(a, b):
    return jnp.dot(a, b, preferred_element_type=F32)


def _ada_kernel(c_ref, w_ref, b_ref, o_ref):
    s = _silu(c_ref[...])
    o_ref[0] = _dot(s.astype(BF16), w_ref[0].astype(BF16)) + b_ref[0]


def _adaln(cond, ada_w, ada_b):
    depth, d, n = ada_w.shape
    rows = cond.shape[0]
    tn = 1536
    return pl.pallas_call(
        _ada_kernel,
        grid=(depth, n // tn),
        in_specs=[
            pl.BlockSpec((rows, d), lambda l, j: (0, 0)),
            pl.BlockSpec((1, d, tn), lambda l, j: (l, 0, j)),
            pl.BlockSpec((1, 1, tn), lambda l, j: (l, 0, j)),
        ],
        out_specs=pl.BlockSpec((1, rows, tn), lambda l, j: (l, 0, j)),
        out_shape=jax.ShapeDtypeStruct((depth, rows, n), F32),
        compiler_params=_params("parallel", "parallel"),
        name="adaln",
    )(cond, ada_w, ada_b.reshape(depth, 1, n))


SUM_ROWS = 16


def _attn_in_kernel(*refs, rope, chunk_kv):
    if rope:
        (x_ref, sh_ref, sc_ref, nw_ref, w_ref, qg_ref, kg_ref, cos_ref, sin_ref,
         a_ref, q_ref, k_ref, v_ref) = refs
    else:
        (x_ref, sh_ref, sc_ref, nw_ref, w_ref, qg_ref, kg_ref,
         a_ref, q_ref, k_ref, v_ref) = refs
    tm = x_ref.shape[1]
    nblk = 2 if tm % (2 * 128) == 0 else 1
    rb = tm // nblk
    for blk in range(nblk):
        rows = slice(blk * rb, (blk + 1) * rb)
        h = _rms(x_ref[0, rows], nw_ref[...]) * (1.0 + sc_ref[0]) + sh_ref[0]
        u = _dot(h.astype(BF16), w_ref[...])
        a_ref[0, rows] = u[:, :POOL_DIM]

        def norm_rope(t, gain, rows=rows):
            y = _rms(t, gain)
            if rope:
                y = y * cos_ref[rows] + pltpu.roll(y, HEAD_DIM // 2, 1) * sin_ref[rows]
            return y.astype(BF16)

        off = POOL_DIM
        for hh in range(N_Q_HEADS):
            q_ref[0, rows, hh * HEAD_DIM:(hh + 1) * HEAD_DIM] = norm_rope(
                u[:, off + hh * HEAD_DIM: off + (hh + 1) * HEAD_DIM], qg_ref[...])
        off += ATTN_DIM
        for hh in range(N_KV_HEADS):
            kh = norm_rope(u[:, off + hh * HEAD_DIM: off + (hh + 1) * HEAD_DIM], kg_ref[...])
            vh = u[:, off + KV_DIM + hh * HEAD_DIM: off + KV_DIM + (hh + 1) * HEAD_DIM]
            if chunk_kv:
                k_ref[0, hh, 0, rows] = kh
                v_ref[0, hh, 0, 0:HEAD_DIM, rows] = vh.T.astype(BF16)
            else:
                k_ref[0, rows, hh * HEAD_DIM:(hh + 1) * HEAD_DIM] = kh
                v_ref[0, rows, hh * HEAD_DIM:(hh + 1) * HEAD_DIM] = vh.astype(BF16)
    if chunk_kv:
        for hh in range(N_KV_HEADS):
            v_ref[0, hh, 0, HEAD_DIM:] = jnp.ones((SUM_ROWS, tm), BF16)


def _attn_in(x, shift, scale, nw, w_in, qg, kg, cos2, sin2, tm, chunk_kv=False):
    b, t, d = x.shape
    shared = shift.shape[0] == 1
    rope = cos2 is not None
    width = w_in.shape[1]
    in_specs = [
        pl.BlockSpec((1, tm, d), lambda bb, i: (bb, i, 0)),
        _row_spec(d, shared), _row_spec(d, shared),
        _const_spec((1, d)), _const_spec((d, width)),
        _const_spec((1, HEAD_DIM)), _const_spec((1, HEAD_DIM)),
    ]
    args = [x, shift, scale, nw, w_in, qg, kg]
    if rope:
        in_specs += [pl.BlockSpec((tm, HEAD_DIM), lambda bb, i: (i, 0))] * 2
        args += [cos2, sin2]
    outs = [(POOL_DIM, F32), (ATTN_DIM, BF16)] + ([] if chunk_kv else [(KV_DIM, BF16), (KV_DIM, BF16)])
    out_specs = [pl.BlockSpec((1, tm, c), lambda bb, i: (bb, i, 0)) for c, _ in outs]
    out_shape = [jax.ShapeDtypeStruct((b, t, c), dt) for c, dt in outs]
    if chunk_kv:
        for rows_, cols_ in ((tm, HEAD_DIM), (HEAD_DIM + SUM_ROWS, tm)):
            out_specs.append(pl.BlockSpec((1, N_KV_HEADS, 1, rows_, cols_), lambda bb, i: (bb, 0, i, 0, 0)))
            out_shape.append(jax.ShapeDtypeStruct((b, N_KV_HEADS, t // tm, rows_, cols_), BF16))
    return pl.pallas_call(
        functools.partial(_attn_in_kernel, rope=rope, chunk_kv=chunk_kv),
        grid=(b, t // tm),
        in_specs=in_specs,
        out_specs=out_specs,
        out_shape=out_shape,
        compiler_params=_params("parallel", "parallel"),
        name="attn_in",
    )(*args)


SOFTMAX_C = (HEAD_DIM ** -0.5) * math.log2(math.e)
ATTN_QB = 256
ATTN_UNROLL = 8


def _attn_kernel(q_ref, k_ref, vt_ref, o_ref, m_sc, acc_sc, s0_sc, mc0_sc, s1_sc, mc1_sc, *, n, tq):
    slots = ((s0_sc, mc0_sc), (s1_sc, mc1_sc))

    qb = min(ATTN_QB, tq)
    blocks = [(sub, g) for sub in range(tq // qb) for g in range(Q_PER_KV)]
    q = q_ref[0]
    qs = jnp.concatenate([q[sub * qb:(sub + 1) * qb, g * HEAD_DIM:(g + 1) * HEAD_DIM] for sub, g in blocks], axis=0)
    m_sc[...] = jnp.full(m_sc.shape, -jnp.inf, F32)
    acc_sc[...] = jnp.zeros(acc_sc.shape, F32)

    def produce(j, slot, b):
        s_sc, mc_sc = slots[slot]
        cols = slice(b * qb, (b + 1) * qb)
        s = lax.dot_general(k_ref[0, 0, j], qs[cols], (((1,), (1,)), ((), ())),
                            preferred_element_type=F32)
        s_sc[:, cols] = s
        mc_sc[:, cols] = jnp.max(s, axis=0, keepdims=True)

    def consume(j, slot, b):
        s_sc, mc_sc = slots[slot]
        cols = slice(b * qb, (b + 1) * qb)
        m_prev = m_sc[:, cols]
        m_new = jnp.maximum(m_prev, mc_sc[:, cols])
        alpha = jnp.exp2(m_prev - m_new)
        p = jnp.exp2(s_sc[:, cols] - m_new).astype(BF16)
        acc_sc[:, cols] = alpha * acc_sc[:, cols] + _dot(vt_ref[0, 0, j], p)
        m_sc[:, cols] = m_new

    def step(jp, sp, jc, sc):
        for b in range(len(blocks)):
            if jp is not None:
                produce(jp, sp, b)
            if jc is not None:
                consume(jc, sc, b)

    step(0, 0, None, None)

    def body(jj, carry):
        for u in range(ATTN_UNROLL):
            step(ATTN_UNROLL * jj + u + 1, (u + 1) % 2, ATTN_UNROLL * jj + u, u % 2)
        return carry

    trips = (n - 1) // ATTN_UNROLL
    if trips:
        lax.fori_loop(0, trips, body, 0)
    for j in range(trips * ATTN_UNROLL, n - 1):
        step(j + 1, (j + 1) % 2, j, j % 2)
    step(None, None, n - 1, (n - 1) % 2)

    acc = acc_sc[...]
    o_t = acc[:HEAD_DIM] / acc[HEAD_DIM:HEAD_DIM + 1]
    for b, (sub, g) in enumerate(blocks):
        o_ref[0, sub * qb:(sub + 1) * qb, g * HEAD_DIM:(g + 1) * HEAD_DIM] = (
            o_t[:, b * qb:(b + 1) * qb].T.astype(BF16))


def _kv_tail(k, v, tk):
    b, l, _ = k.shape
    pad = tk - l
    assert 0 <= pad <= l
    k = jnp.concatenate([k, k[:, :pad]], axis=1)
    v = jnp.concatenate([v, jnp.zeros((b, pad, KV_DIM), BF16)], axis=1)
    live = (jnp.arange(tk) < l).astype(BF16)
    kc = k.reshape(b, 1, tk, N_KV_HEADS, HEAD_DIM).transpose(0, 3, 1, 2, 4)
    vt = v.reshape(b, 1, tk, N_KV_HEADS, HEAD_DIM).transpose(0, 3, 1, 4, 2)
    ones = jnp.broadcast_to(live.reshape(1, 1, tk), (b, N_KV_HEADS, 1, SUM_ROWS, tk))
    return kc, jnp.concatenate([vt, ones], axis=3)


def _attention(q, kc, vt, tq):
    b, t, _ = q.shape
    gw = Q_PER_KV * HEAD_DIM
    n, tk = kc.shape[2], kc.shape[3]
    m = Q_PER_KV * tq
    kv_spec = lambda a: pl.BlockSpec((1, 1) + a.shape[2:], lambda bb, g, i: (bb, g, 0, 0, 0))
    kv_args = [kc, vt]
    return pl.pallas_call(
        functools.partial(_attn_kernel, n=n, tq=tq),
        grid=(b, N_KV_HEADS, t // tq),
        in_specs=[pl.BlockSpec((1, tq, gw), lambda bb, g, i: (bb, i, g))] + [kv_spec(a) for a in kv_args],
        out_specs=pl.BlockSpec((1, tq, gw), lambda bb, g, i: (bb, i, g)),
        out_shape=jax.ShapeDtypeStruct((b, t, ATTN_DIM), BF16),
        scratch_shapes=[pltpu.VMEM((1, m), F32), pltpu.VMEM((HEAD_DIM + SUM_ROWS, m), F32)]
        + [pltpu.VMEM((tk, m), F32), pltpu.VMEM((1, m), F32)] * 2,
        compiler_params=_params("parallel", "parallel", "arbitrary"),
        name="attention",
    )(q, *kv_args)


def _attn_out_kernel(x_ref, g_ref, nw_ref, a_ref, ap_ref, an_ref, o_ref, pw_ref, ps_ref, wa_ref, wo_ref,
                     out_ref, abuf, *, tm, t):
    i = pl.program_id(1)
    last = pl.num_programs(1) - 1
    abuf[0:HALO] = jnp.where(i == 0, 0.0, ap_ref[0])
    abuf[HALO:HALO + tm] = a_ref[0]
    abuf[HALO + tm:] = jnp.where(i == last, 0.0, an_ref[0])

    lane = lax.broadcasted_iota(jnp.int32, (1, POOL_DIM), 1)
    nblk = 2 if tm % (2 * 128) == 0 else 1
    rb = tm // nblk
    for blk in range(nblk):
        r0 = blk * rb
        rows = slice(r0, r0 + rb)

        def sh(d, r0=r0):
            return abuf[HALO + d + r0:HALO + d + r0 + rb, :]

        x0 = sh(0)
        s2 = sh(-1) + x0
        s4 = s2 + sh(-2) + sh(1)
        s8 = s4 + sh(-4) + sh(-3) + sh(2) + sh(3)
        s16 = s8 + sh(-8) + sh(-7) + sh(-6) + sh(-5) + sh(4) + sh(5) + sh(6) + sh(7)
        pos = i * tm + r0 + lax.broadcasted_iota(jnp.int32, (rb, 1), 0)

        def mean(s, w, pos=pos):
            left = w // 2
            right = w - 1 - left
            cnt = jnp.minimum(pos + right, t - 1) - jnp.maximum(pos - left, 0) + 1
            return s / cnt.astype(F32)

        pm = jnp.where(lane < 64, mean(s2, 2),
                       jnp.where(lane < 128, mean(s4, 4),
                                 jnp.where(lane < 192, mean(s8, 8), mean(s16, 16))))
        p = pm - x0
        py = _dot(p.astype(BF16), pw_ref[...]) * ps_ref[...]
        y = _dot(py.astype(BF16), wa_ref[...]) + _dot(o_ref[0, rows], wo_ref[...])
        out_ref[0, rows] = x_ref[0, rows] + g_ref[0] * _rms(y, nw_ref[...])


def _attn_out(x, gate, nw, a, o, pool_bd, pool_scale, w_a, w_o, tm):
    b, t, d = x.shape
    shared = gate.shape[0] == 1
    am, ap, an = _halo_specs(tm, POOL_DIM, t)
    return pl.pallas_call(
        functools.partial(_attn_out_kernel, tm=tm, t=t),
        grid=(b, t // tm),
        in_specs=[
            pl.BlockSpec((1, tm, d), lambda bb, i: (bb, i, 0)),
            _row_spec(d, shared), _const_spec((1, d)),
            am, ap, an,
            pl.BlockSpec((1, tm, ATTN_DIM), lambda bb, i: (bb, i, 0)),
            _const_spec((POOL_DIM, POOL_DIM)), _const_spec((1, POOL_DIM)),
            _const_spec((POOL_DIM, d)), _const_spec((ATTN_DIM, d)),
        ],
        out_specs=pl.BlockSpec((1, tm, d), lambda bb, i: (bb, i, 0)),
        out_shape=jax.ShapeDtypeStruct((b, t, d), F32),
        scratch_shapes=[pltpu.VMEM((tm + 2 * HALO, POOL_DIM), F32)],
        compiler_params=_params("parallel", "parallel"),
        name="attn_out",
    )(x, gate, nw, a, a, a, o, pool_bd, pool_scale, w_a, w_o)


def _stage_tile(xe_sc, xp_ref, x_ref, xn_ref, tm):
    for c in range(xe_sc.shape[0]):
        cols = slice(c * LANES, (c + 1) * LANES)
        xe_sc[c, 0:HALO] = xp_ref[0, :, cols]
        xe_sc[c, HALO:HALO + tm] = x_ref[0, :, cols]
        xe_sc[c, HALO + tm:] = xn_ref[0, :, cols]


def _permuted_rows(xe_sc):
    nslab, r, _ = xe_sc.shape
    nv = r // 8
    return jnp.concatenate(
        [jnp.concatenate([xe_sc[c, pl.ds(v, 8, stride=nv), :] for c in range(nslab)], axis=1) for v in range(nv)],
        axis=0)


def _unpermute_into(ye_sc, val):
    nslab, r, _ = ye_sc.shape
    nv = r // 8
    for c in range(nslab):
        for v in range(nv):
            ye_sc[c, pl.ds(v, 8, stride=nv), :] = val[8 * v:8 * v + 8, c * LANES:(c + 1) * LANES]


def _edge_masks(r, at_start, at_end):
    sub = lax.broadcasted_iota(jnp.int32, (8 * HALO, 1), 0) % 8
    top = jnp.where(jnp.logical_and(sub == 0, at_start), 0.0, 1.0)
    bot = jnp.where(jnp.logical_and(sub == 7, at_end), 0.0, 1.0)
    return top, bot


def _store_masked(u_sc, u, masks):
    r = u.shape[0]
    edge = 8 * HALO
    u_sc[0:edge] = u[0:edge] * masks[0]
    u_sc[edge:r - edge] = u[edge:r - edge]
    u_sc[r - edge:] = u[r - edge:] * masks[1]


FFN_UNROLL = 2


def _ffn_kernel(x_ref, xp_ref, xn_ref, sh_ref, sc_ref, g_ref, nw2_ref, nw3_ref,
                wup_ref, cw_ref, cb_ref, wdn_ref, out_ref, xe_sc, h_sc, u0_sc, u1_sc, acc_sc, *, tm, fc, nch):
    i = pl.program_id(1)
    last = pl.num_programs(1) - 1
    r = tm + 2 * HALO
    xe_sc[0:HALO] = xp_ref[0]
    xe_sc[HALO:HALO + tm] = x_ref[0]
    xe_sc[HALO + tm:] = xn_ref[0]
    acc_sc[...] = jnp.zeros(acc_sc.shape, F32)
    keep_top = jnp.where(i == 0, 0.0, 1.0)
    keep_bot = jnp.where(i == last, 0.0, 1.0)
    slots = (u0_sc, u1_sc)
    half = tm // 2 if tm % 32 == 0 else tm

    def produce(j, slot, lo=0, hi=r):
        u = _dot(h_sc[lo:hi], wup_ref[j])
        u_sc = slots[slot]
        a, b = lo, hi
        if lo == 0:
            u_sc[0:HALO] = u[0:HALO] * keep_top
            a = HALO
        if hi == r:
            u_sc[r - HALO:r] = u[r - HALO - lo:r - lo] * keep_bot
            b = r - HALO
        u_sc[a:b] = u[a - lo:b - lo]

    def consume(j, slot, lo=0, hi=tm):
        u_sc = slots[slot]
        cw = cw_ref[j]
        cv = cb_ref[j]
        for k in range(3):
            cv = cv + u_sc[HALO - 1 + k + lo:HALO - 1 + k + hi, :] * cw[k:k + 1]
        g = _silu(cv[:, fc:]) * cv[:, :fc]
        acc_sc[lo:hi] += _dot(g.astype(BF16), wdn_ref[j])

    for lo, hi in ((0, half), (half, r)):
        h = _rms(xe_sc[lo:hi], nw2_ref[...]) * (1.0 + sc_ref[0]) + sh_ref[0]
        h_sc[lo:hi] = h.astype(BF16)
        produce(0, 0, lo, hi)

    def body(jj, carry):
        for u in range(FFN_UNROLL):
            produce(FFN_UNROLL * jj + u + 1, (u + 1) % 2)
            consume(FFN_UNROLL * jj + u, u % 2)
        return carry

    trips = (nch - 1) // FFN_UNROLL
    if trips > 1:
        lax.fori_loop(0, trips, body, 0)
    elif trips == 1:
        body(0, 0)
    for j in range(trips * FFN_UNROLL, nch - 1):
        produce(j + 1, (j + 1) % 2)
        consume(j, j % 2)
    for lo, hi in ((0, half), (half, tm)) if half < tm else ((0, tm),):
        consume(nch - 1, (nch - 1) % 2, lo, hi)
        out_ref[0, lo:hi] = x_ref[0, lo:hi] + g_ref[0] * _rms(acc_sc[lo:hi], nw3_ref[...])


def _ffn(x, shift, scale, gate, nw2, nw3, wup, cw, cb, wdn, tm):
    b, t, d = x.shape
    shared = shift.shape[0] == 1
    nch, _, fc2 = wup.shape
    fc = fc2 // 2
    xm, xp, xn = _halo_specs(tm, d, t)
    return pl.pallas_call(
        functools.partial(_ffn_kernel, tm=tm, fc=fc, nch=nch),
        grid=(b, t // tm),
        in_specs=[
            xm, xp, xn,
            _row_spec(d, shared), _row_spec(d, shared), _row_spec(d, shared),
            _const_spec((1, d)), _const_spec((1, d)),
            _const_spec(wup.shape), _const_spec(cw.shape), _const_spec(cb.shape), _const_spec(wdn.shape),
        ],
        out_specs=pl.BlockSpec((1, tm, d), lambda bb, i: (bb, i, 0)),
        out_shape=jax.ShapeDtypeStruct((b, t, d), F32),
        scratch_shapes=[
            pltpu.VMEM((tm + 2 * HALO, d), F32),
            pltpu.VMEM((tm + 2 * HALO, d), BF16),
            pltpu.VMEM((tm + 2 * HALO, fc2), F32),
            pltpu.VMEM((tm + 2 * HALO, fc2), F32),
            pltpu.VMEM((tm, d), F32),
        ],
        compiler_params=_params("parallel", "parallel"),
        name="conv_ffn",
    )(x, x, x, shift, scale, gate, nw2, nw3, wup, cw, cb, wdn)


def _ssm_in_kernel(x_ref, xp_ref, xn_ref, sh_ref, sc_ref, nw_ref, wz_ref, wx_ref, wdt_ref,
                   cw_ref, cb_ref, dtb_ref, z_ref, xs_ref, bm_ref, cm_ref, dt_ref,
                   xe_sc, u0_sc, u1_sc, y0_sc, y1_sc, *, tm, cc):
    u_scs = (u0_sc, u1_sc)
    y_scs = (y0_sc, y1_sc)
    i = pl.program_id(1)
    last = pl.num_programs(1) - 1
    r = tm + 2 * HALO
    _stage_tile(xe_sc, xp_ref, x_ref, xn_ref, tm)

    def mod(v):
        return (_rms(v, nw_ref[...]) * (1.0 + sc_ref[0]) + sh_ref[0]).astype(BF16)

    he = mod(_permuted_rows(xe_sc))
    hm = mod(x_ref[0])
    masks = _edge_masks(r, i == 0, i == last)

    for j in range(D_INNER // cc):
        z_ref[0, :, j * cc:(j + 1) * cc] = _dot(hm, wz_ref[:, j * cc:(j + 1) * cc])

    dt_raw = _dot(hm, wdt_ref[...]) + dtb_ref[...]
    dt_ref[0] = jnp.maximum(dt_raw, 0.0) + jnp.log1p(jnp.exp(-jnp.abs(dt_raw)))

    def down1(a):
        return pltpu.roll(a, 1, 0)

    conv_dim = D_INNER + 2 * BC_DIM
    for j in range(conv_dim // cc):
        lo = j * cc
        u_sc, y_sc = u_scs[j % 2], y_scs[j % 2]
        _store_masked(u_sc, _dot(he, wx_ref[:, lo:lo + cc]), masks)

        def conv(*taps, lo=lo):
            cv = cb_ref[:, lo:lo + cc]
            for k in range(4):
                cv = cv + taps[k] * cw_ref[k:k + 1, lo:lo + cc]
            return _silu(cv)

        act = jnp.concatenate([
            conv(down1(u_sc[r - 16:r - 8]), down1(u_sc[r - 8:r]), u_sc[0:8], u_sc[8:16]),
            conv(down1(u_sc[r - 8:r]), u_sc[0:8], u_sc[8:16], u_sc[16:24]),
            conv(u_sc[0:r - 24], u_sc[8:r - 16], u_sc[16:r - 8], u_sc[24:r]),
            conv(u_sc[r - 24:r - 16], u_sc[r - 16:r - 8], u_sc[r - 8:r], pltpu.roll(u_sc[0:8], 7, 0)),
        ], axis=0)
        _unpermute_into(y_sc, act)
        for c in range(cc // LANES):
            col = lo + c * LANES
            blk = y_sc[c, HALO:HALO + tm]
            if col < D_INNER:
                xs_ref[0, :, col:col + LANES] = blk
            elif col < D_INNER + BC_DIM:
                bm_ref[0, :, col - D_INNER:col - D_INNER + LANES] = blk.astype(BF16)
            else:
                cm_ref[0, :, col - D_INNER - BC_DIM:col - D_INNER - BC_DIM + LANES] = blk.astype(BF16)


def _ssm_in(x, shift, scale, nw, wz, wx, wdt, cw, cb, dtb, tm):
    b, t, d = x.shape
    shared = shift.shape[0] == 1
    cc = 512
    xm, xp, xn = _halo_specs(tm, d, t)
    outs = [(D_INNER, F32), (D_INNER, F32), (BC_DIM, BF16), (BC_DIM, BF16), (2 * LANES, F32)]
    return pl.pallas_call(
        functools.partial(_ssm_in_kernel, tm=tm, cc=cc),
        grid=(b, t // tm),
        in_specs=[
            xm, xp, xn, _row_spec(d, shared), _row_spec(d, shared), _const_spec((1, d)),
            _const_spec(wz.shape), _const_spec(wx.shape), _const_spec(wdt.shape),
            _const_spec(cw.shape), _const_spec(cb.shape), _const_spec(dtb.shape),
        ],
        out_specs=[pl.BlockSpec((1, tm, c), lambda bb, i: (bb, i, 0)) for c, _ in outs],
        out_shape=[jax.ShapeDtypeStruct((b, t, c), dt) for c, dt in outs],
        scratch_shapes=[pltpu.VMEM((d // LANES, tm + 2 * HALO, LANES), F32)]
        + [pltpu.VMEM((tm + 2 * HALO, cc), F32)] * 2 + [pltpu.VMEM((cc // LANES, tm + 2 * HALO, LANES), F32)] * 2,
        compiler_params=_params("parallel", "parallel"),
        name="ssm_in",
    )(x, x, x, shift, scale, nw, wz, wx, wdt, cw, cb, dtb)


DT_REP = 3


def _split3(v, lane):
    p1 = v.astype(BF16).astype(F32)
    r1 = v - p1
    p2 = r1.astype(BF16).astype(F32)
    r2 = r1 - p2
    packed = jnp.where(lane < 32, p1, jnp.where(lane < 64, p2, jnp.where(lane < 96, r2, 0.0)))
    return packed.astype(BF16)


def _sel_matrix(width):
    r = jnp.arange(LANES)[:, None]
    c = jnp.arange(SSM_HEADS * width)[None, :]
    return ((c // width == r % SSM_HEADS) & (r < DT_REP * SSM_HEADS)).astype(BF16)


def _ssd_chunk(xs_ref, dt_ref, bm_ref, cm_ref, rows, a_row, selx_ref, h_sc, reverse, y_store):
    q = SSD_CHUNK
    row = lax.broadcasted_iota(jnp.int32, (q, q), 0)
    col = lax.broadcasted_iota(jnp.int32, (q, q), 1)
    mask = (col >= row) if reverse else (col <= row)
    tri = jnp.where(mask, 1.0, 0.0).astype(BF16)
    lane = lax.broadcasted_iota(jnp.int32, (1, LANES), 1)
    lane_lo = lane < SSM_HEAD_DIM
    end = 0 if reverse else q - 1

    dt = dt_ref[0, rows]
    a = dt * a_row
    a1 = a.astype(BF16)
    r1 = a - a1.astype(F32)
    a2 = r1.astype(BF16)
    a3 = (r1 - a2.astype(F32)).astype(BF16)
    acum = _dot(tri, a1) + _dot(tri, a2) + _dot(tri, a3)
    dt_p = _split3(dt, lane)
    if y_store is not None:
        acum_t = acum.T

    for g in range(SSM_GROUPS):
        gs = slice(g * D_STATE, (g + 1) * D_STATE)
        b_g = bm_ref[0, rows, gs]
        c_g = cm_ref[0, rows, gs]
        h_g = h_sc[g]
        dt_x = _dot(dt_p, selx_ref[:, g * GROUP_W:(g + 1) * GROUP_W])
        if y_store is not None:
            cb = lax.dot_general(c_g, b_g, (((1,), (1,)), ((), ())), preferred_element_type=F32)
            y_inter = _dot(c_g, h_g.astype(BF16))
        xd_parts = []
        etot_parts = []
        for j in range(HEADS_PER_GROUP // 2):
            hd = g * HEADS_PER_GROUP + 2 * j
            pair = hd // 2
            pl_ = slice(j * LANES, (j + 1) * LANES)
            d2 = [jnp.broadcast_to(acum[:, hd + k:hd + k + 1], (q, q)) for k in range(2)]
            dmix = jnp.where(lane_lo, d2[0], d2[1])
            tot = dmix[end:end + 1]
            xdt = xs_ref[0, rows, pair * LANES:(pair + 1) * LANES] * dt_x[:, pl_]
            xd_parts.append((xdt * jnp.exp2(tot - dmix)).astype(BF16))
            etot_parts.append(jnp.exp2(tot))
            if y_store is not None:
                lm = []
                for k in range(2):
                    seg = d2[k] - acum_t[hd + k:hd + k + 1, :]
                    lm.append(cb * jnp.exp2(jnp.where(mask, seg, -jnp.inf)))
                m2 = jnp.concatenate(lm, axis=1).astype(BF16)
                xb = xdt.astype(BF16)
                zero = jnp.zeros_like(xb)
                rhs = jnp.concatenate([jnp.where(lane_lo, xb, zero), jnp.where(lane_lo, zero, xb)], axis=0)
                y_intra = _dot(m2, rhs)
                y_store(pair, y_intra + y_inter[:, pl_] * jnp.exp2(dmix))
        xd = jnp.concatenate(xd_parts, axis=1)
        upd = lax.dot_general(b_g, xd, (((0,), (0,)), ((), ())), preferred_element_type=F32)
        h_sc[g] = jnp.concatenate(etot_parts, axis=1) * h_g + upd


def _ssd_state_kernel(xs_ref, dtf_ref, dtb_ref, bm_ref, cm_ref, af_ref, ab_ref, selx_ref,
                      hf_ref, hb_ref, h_sc, *, nchunks):
    q = SSD_CHUNK
    for reverse, dt_ref, a_ref, out_ref in ((False, dtf_ref, af_ref, hf_ref), (True, dtb_ref, ab_ref, hb_ref)):
        h_sc[...] = jnp.zeros(h_sc.shape, F32)
        order = range(nchunks - 1, -1, -1) if reverse else range(nchunks)
        for ci in order:
            rows = slice(ci * q, (ci + 1) * q)
            _ssd_chunk(xs_ref, dt_ref, bm_ref, cm_ref, rows, a_ref[...], selx_ref, h_sc, reverse, None)
        out_ref[0] = h_sc[...]


def _dir_spec(rev):
    return pl.BlockSpec((1, LANES), lambda *_: (0, rev))


def _ssd_states(xs, dt, bm, cm, a_row, selx):
    b, t, _ = xs.shape
    st = jax.ShapeDtypeStruct((b, SSM_GROUPS, D_STATE, GROUP_W), F32)
    full = lambda c, j=0: pl.BlockSpec((1, t, c), lambda bb: (bb, 0, j))
    st_spec = pl.BlockSpec((1, SSM_GROUPS, D_STATE, GROUP_W), lambda bb: (bb, 0, 0, 0))
    return pl.pallas_call(
        functools.partial(_ssd_state_kernel, nchunks=t // SSD_CHUNK),
        grid=(b,),
        in_specs=[full(D_INNER), full(LANES, 0), full(LANES, 1), full(BC_DIM), full(BC_DIM),
                  _dir_spec(0), _dir_spec(1), _const_spec(selx.shape)],
        out_specs=[st_spec, st_spec],
        out_shape=[st, st],
        scratch_shapes=[pltpu.VMEM((SSM_GROUPS, D_STATE, GROUP_W), F32)],
        compiler_params=_params("parallel"),
        name="ssd_ctx_states",
    )(xs, dt, dt, bm, cm, a_row, a_row, selx)


def _ssd_fwd_kernel(xs_ref, dt_ref, bm_ref, cm_ref, a_ref, selx_ref, h0_ref, y_ref, h_sc, *, cps):
    q = SSD_CHUNK

    @pl.when(pl.program_id(1) == 0)
    def _():
        h_sc[...] = h0_ref[0]

    for ci in range(cps):
        rows = slice(ci * q, (ci + 1) * q)

        def store(pair, val, rows=rows):
            y_ref[0, rows, pair * LANES:(pair + 1) * LANES] = val

        _ssd_chunk(xs_ref, dt_ref, bm_ref, cm_ref, rows, a_ref[...], selx_ref, h_sc, False, store)


def _ssd_fwd(xs, dt, bm, cm, a_row, selx, h0, cps):
    b, t, _ = xs.shape
    tm = cps * SSD_CHUNK
    blk = lambda c: pl.BlockSpec((1, tm, c), lambda bb, i: (bb, i, 0))
    return pl.pallas_call(
        functools.partial(_ssd_fwd_kernel, cps=cps),
        grid=(b, t // tm),
        in_specs=[blk(D_INNER), blk(LANES), blk(BC_DIM), blk(BC_DIM), _dir_spec(0),
                  _const_spec(selx.shape),
                  pl.BlockSpec((1, SSM_GROUPS, D_STATE, GROUP_W), lambda bb, i: (bb, 0, 0, 0))],
        out_specs=blk(D_INNER),
        out_shape=jax.ShapeDtypeStruct((b, t, D_INNER), F32),
        scratch_shapes=[pltpu.VMEM((SSM_GROUPS, D_STATE, GROUP_W), F32)],
        compiler_params=_params("parallel", "arbitrary"),
        name="ssd_fwd",
    )(xs, dt, bm, cm, a_row, selx, h0)


def _ssd_bwd_out_kernel(xs_ref, dt_ref, bm_ref, cm_ref, a_ref, selx_ref, h0_ref, yf_ref, z_ref, x_ref,
                        g_ref, dsk_ref, snw_ref, wo_ref, nw_ref, out_ref, h_sc, yb_sc, *, cps):
    q = SSD_CHUNK

    @pl.when(pl.program_id(1) == 0)
    def _():
        h_sc[...] = h0_ref[0]

    for ci in range(cps - 1, -1, -1):
        rows = slice(ci * q, (ci + 1) * q)

        def store(pair, val, rows=rows):
            yb_sc[rows, pair * LANES:(pair + 1) * LANES] = val

        _ssd_chunk(xs_ref, dt_ref, bm_ref, cm_ref, rows, a_ref[...], selx_ref, h_sc, True, store)

    y = yf_ref[0] + yb_sc[...] + dsk_ref[...] * xs_ref[0]
    yn = _rms(y * _silu(z_ref[0]), snw_ref[...])
    o = _dot(yn.astype(BF16), wo_ref[...])
    out_ref[0] = x_ref[0] + g_ref[0] * _rms(o, nw_ref[...])


def _ssd_bwd_out(xs, dt, bm, cm, a_row, selx, h0, yf, z, x, gate, dskip, snw, w_out, nw, cps):
    b, t, d = x.shape
    tm = cps * SSD_CHUNK
    nblk = t // tm
    blk = lambda c, j=0: pl.BlockSpec((1, tm, c), lambda bb, i: (bb, nblk - 1 - i, j))
    return pl.pallas_call(
        functools.partial(_ssd_bwd_out_kernel, cps=cps),
        grid=(b, nblk),
        in_specs=[blk(D_INNER), blk(LANES, 1), blk(BC_DIM), blk(BC_DIM), _dir_spec(1),
                  _const_spec(selx.shape),
                  pl.BlockSpec((1, SSM_GROUPS, D_STATE, GROUP_W), lambda bb, i: (bb, 0, 0, 0)),
                  blk(D_INNER), blk(D_INNER), blk(d),
                  pl.BlockSpec((1, 1, d), lambda bb, i: (bb, 0, 0)),
                  _const_spec((1, D_INNER)), _const_spec((1, D_INNER)), _const_spec((D_INNER, d)),
                  _const_spec((1, d))],
        out_specs=blk(d),
        out_shape=jax.ShapeDtypeStruct((b, t, d), F32),
        scratch_shapes=[pltpu.VMEM((SSM_GROUPS, D_STATE, GROUP_W), F32), pltpu.VMEM((tm, D_INNER), F32)],
        compiler_params=_params("parallel", "arbitrary"),
        name="ssd_bwd_out",
    )(xs, dt, bm, cm, a_row, selx, h0, yf, z, x, gate, dskip, snw, w_out, nw)


def _rope_tables(t):
    rows = t // GRID_W
    row = jnp.repeat(jnp.arange(rows, dtype=F32), GRID_W)
    col = jnp.tile(jnp.arange(GRID_W, dtype=F32), rows)
    half = HEAD_DIM // 2
    inv_freq = ROPE_THETA ** (-jnp.arange(0, half, 2, dtype=F32) / half)
    ang = jnp.concatenate([row[:, None] * inv_freq, col[:, None] * inv_freq], axis=-1)
    cos, sin = jnp.cos(ang), jnp.sin(ang)
    return jnp.concatenate([cos, cos], axis=-1), jnp.concatenate([-sin, sin], axis=-1)


def _tile(t, pref):
    return pref if t % pref == 0 else t


def kernel(x, c, ctx, c_ctx, ada_w, ada_b, norm_w, attn_w_in, pool_w, pool_scale, q_gain, k_gain, attn_w_out,
           ssm_w_in, ssm_conv_w, ssm_conv_b, ssm_A_log, ssm_dt_bias, ssm_D, ssm_norm_w, ssm_w_out,
           ffn_w_up, ffn_conv_w, ffn_conv_b, ffn_w_down):
    bsz, t, d = x.shape
    lc = ctx.shape[1]
    depth = ada_w.shape[0]
    assert depth == 2 and attn_w_in.shape[0] == 1 and ssm_w_in.shape[0] == 1
    d_ff = ffn_w_down.shape[1]

    rows = -(-(bsz + 1) // HALO) * HALO
    cond = jnp.zeros((rows, d), F32).at[:bsz].set(c).at[bsz].set(c_ctx)
    ada = _adaln(cond, ada_w, ada_b)
    mod_l = [[ada[i, :bsz, k * d:(k + 1) * d].reshape(bsz, 1, d) for k in range(6)] for i in range(depth)]
    mod_c = [[ada[i, bsz:bsz + 1, k * d:(k + 1) * d].reshape(1, 1, d) for k in range(6)] for i in range(depth)]
    nw = lambda i, k: norm_w[i, k].reshape(1, d)

    fc = 256
    nch = d_ff // fc

    def ffn_weights(i):
        wu = ffn_w_up[i].astype(BF16)
        wup = jnp.concatenate([wu[:, :d_ff].reshape(d, nch, fc), wu[:, d_ff:].reshape(d, nch, fc)], axis=-1)
        wup = wup.transpose(1, 0, 2)
        cwt = ffn_conv_w[i].T
        cw = jnp.concatenate([cwt[:, :d_ff].reshape(3, nch, fc), cwt[:, d_ff:].reshape(3, nch, fc)], axis=-1)
        cw = cw.transpose(1, 0, 2)
        cb = jnp.concatenate([ffn_conv_b[i][:d_ff].reshape(nch, 1, fc), ffn_conv_b[i][d_ff:].reshape(nch, 1, fc)],
                             axis=-1)
        wdn = ffn_w_down[i].astype(BF16).reshape(nch, fc, d)
        return wup, cw, cb, wdn

    def ffn(xx, mods, i, tm):
        wup, cw, cb, wdn = ffn_weights(i)
        return _ffn(xx, mods[3], mods[4], mods[5], nw(i, 2), nw(i, 3), wup, cw, cb, wdn, tm)

    perm = jnp.concatenate([jnp.arange(0, HEAD_DIM, 2), jnp.arange(1, HEAD_DIM, 2)])
    cols = jnp.arange(attn_w_in.shape[2])
    qk_lo, qk_hi = POOL_DIM, POOL_DIM + ATTN_DIM + KV_DIM
    qk_cols = (qk_lo + (jnp.arange(qk_hi - qk_lo) // HEAD_DIM) * HEAD_DIM
               + perm[jnp.arange(qk_hi - qk_lo) % HEAD_DIM])
    cols = cols.at[qk_lo:qk_hi].set(qk_cols)
    w_in_a = attn_w_in[0][:, cols].astype(BF16)
    qg = (q_gain[0][perm] * SOFTMAX_C).reshape(1, HEAD_DIM)
    kg = k_gain[0][perm].reshape(1, HEAD_DIM)
    cos2, sin2 = _rope_tables(t)
    eye = jnp.eye(len(POOL_WINDOWS), dtype=F32)
    pool_bd = (eye[:, None, :, None] * pool_w[0][:, :, None, :]).reshape(POOL_DIM, POOL_DIM).astype(BF16)
    ps = pool_scale[0].reshape(1, POOL_DIM)
    w_out_a = attn_w_out[0][:POOL_DIM].astype(BF16)
    w_out_o = attn_w_out[0][POOL_DIM:].astype(BF16)

    tm_l = _tile(t, 512)
    tm_c = _tile(lc, 512)
    a_c, q_c, k_c, v_c = _attn_in(ctx, mod_c[0][0], mod_c[0][1], nw(0, 0), w_in_a, qg, kg, None, None, tm_c)
    a_l, q_l, kc_l, vt_l = _attn_in(x, mod_l[0][0], mod_l[0][1], nw(0, 0), w_in_a, qg, kg, cos2, sin2, tm_l,
                                    chunk_kv=True)
    assert lc <= tm_l
    kc_t, vt_t = _kv_tail(k_c, v_c, tm_l)
    o_l = _attention(q_l, jnp.concatenate([kc_l, kc_t], axis=2), jnp.concatenate([vt_l, vt_t], axis=2),
                     _tile(t, 512))
    o_c = _attention(q_c, *_kv_tail(k_c, v_c, lc), _tile(lc, 256))
    x = _attn_out(x, mod_l[0][2], nw(0, 1), a_l, o_l, pool_bd, ps, w_out_a, w_out_o, tm_l)
    ctx = _attn_out(ctx, mod_c[0][2], nw(0, 1), a_c, o_c, pool_bd, ps, w_out_a, w_out_o, tm_c)
    x = ffn(x, mod_l[0], 0, tm_l)
    ctx = ffn(ctx, mod_c[0], 0, tm_c)

    w_in_c = ssm_w_in[0].astype(BF16)
    conv_dim = D_INNER + 2 * BC_DIM
    wz = w_in_c[:, :D_INNER]
    wx = w_in_c[:, D_INNER:D_INNER + conv_dim]

    def dir_lanes(v):
        pad = jnp.zeros(v.shape[:-2] + (LANES - DT_REP * SSM_HEADS,), v.dtype)
        return jnp.concatenate([p for r in range(2) for p in [v[..., r, :]] * DT_REP + [pad]], axis=-1)

    wdt = dir_lanes(w_in_c[:, D_INNER + conv_dim:].reshape(d, 2, SSM_HEADS))
    dtb = dir_lanes(ssm_dt_bias[0]).reshape(1, 2 * LANES)
    a_row = dir_lanes(-jnp.exp(ssm_A_log[0].astype(F32)) * math.log2(math.e)).reshape(1, 2 * LANES)
    selx = _sel_matrix(SSM_HEAD_DIM)
    scw = ssm_conv_w[0].T
    scb = ssm_conv_b[0].reshape(1, conv_dim)
    dskip = jnp.repeat(ssm_D[0], SSM_HEAD_DIM).reshape(1, D_INNER)
    snw = ssm_norm_w[0].reshape(1, D_INNER)
    w_out_c = ssm_w_out[0].astype(BF16)

    tm_s = _tile(t, 512)
    tm_sc = _tile(lc, 256)
    _, xs_c, bm_c, cm_c, dt_c = _ssm_in(ctx, mod_c[1][0], mod_c[1][1], nw(1, 0), wz, wx, wdt, scw, scb, dtb, tm_sc)
    hf0, hb0 = _ssd_states(xs_c, dt_c, bm_c, cm_c, a_row, selx)
    z_l, xs_l, bm_l, cm_l, dt_l = _ssm_in(x, mod_l[1][0], mod_l[1][1], nw(1, 0), wz, wx, wdt, scw, scb, dtb, tm_s)
    cps = lambda want: max(c for c in (1, 2, 4) if c <= want and t % (c * SSD_CHUNK) == 0)
    yf = _ssd_fwd(xs_l, dt_l, bm_l, cm_l, a_row, selx, hf0, cps(4))
    x = _ssd_bwd_out(xs_l, dt_l, bm_l, cm_l, a_row, selx, hb0, yf, z_l, x, mod_l[1][2], dskip, snw, w_out_c,
                     nw(1, 1), cps(4))
    x = ffn(x, mod_l[1], 1, tm_l)
    return x
```
